```python
import jax, jax.numpy as jnp
from jax import lax
import numpy as np

D_MODEL = 2048
BATCH = 8
SEQ = 4096
DEPTH = 4

GRID_W = 64
CTX_LEN = 256
N_MIXERS = 2
N_RET_LAYERS = (DEPTH + 1) // 2
N_MLA_LAYERS = DEPTH // 2

RET_HEADS = 8
RET_DK = D_MODEL // RET_HEADS
RET_DV = 2 * D_MODEL // RET_HEADS
RET_CHUNK = 128
RET_ROPE_BASE = 10000.0
GN_EPS = 1e-6

MLA_HEADS = 16
MLA_Q_LORA = 512
MLA_KV_LORA = 512
MLA_D_NOPE = 128
MLA_D_ROPE = 64
MLA_D_V = 128
MLA_Q_BLOCK = 128
MLA_SCALE = (MLA_D_NOPE + MLA_D_ROPE) ** -0.5
AXIAL_ROPE_BASE = 10000.0
RMS_EPS = 1e-6

FFN_HIDDEN = (((8 * D_MODEL + 2) // 3) + 255) // 256 * 256

DEEPNORM_ALPHA = (2 * DEPTH) ** 0.25
DEEPNORM_BETA = (8 * DEPTH) ** -0.25
LN_EPS = 1e-5

kernel_name = 'hybrid_retention_mla_dit_trunk'


def layer_norm(x, g, b):
    xf = x.astype(jnp.float32)
    mu = jnp.mean(xf, axis=-1, keepdims=True)
    var = jnp.mean(jnp.square(xf - mu), axis=-1, keepdims=True)
    return ((xf - mu) * lax.rsqrt(var + LN_EPS) * g + b).astype(x.dtype)


def rms_norm(x, g):
    xf = x.astype(jnp.float32)
    return (xf * lax.rsqrt(jnp.mean(jnp.square(xf), axis=-1, keepdims=True) + RMS_EPS) * g).astype(x.dtype)


def rope_tables(pos, inv_freq):
    ang = pos.astype(jnp.float32)[:, None] * inv_freq[None, :]
    return jnp.cos(ang), jnp.sin(ang)


def apply_rope(x, cos, sin):
    x1, x2 = jnp.split(x, 2, axis=-1)
    c = cos[:, None, :]
    s = sin[:, None, :]
    return jnp.concatenate([x1 * c - x2 * s, x1 * s + x2 * c], axis=-1).astype(x.dtype)


def apply_rope_2d(x, row_cos, row_sin, col_cos, col_sin):
    xr, xc = jnp.split(x, 2, axis=-1)
    return jnp.concatenate([apply_rope(xr, row_cos, row_sin), apply_rope(xc, col_cos, col_sin)], axis=-1)


def modulation(cond, w, b):
    return jnp.split(jax.nn.silu(cond) @ w + b, 6, axis=-1)


def retention_chunkwise(q, k, v, log_gamma, s0):
    B, H, N, dk = q.shape
    dv = v.shape[-1]
    C = RET_CHUNK
    nc = N // C

    def to_chunks(t):
        return t.reshape(B, H, nc, C, t.shape[-1]).transpose(2, 0, 1, 3, 4)

    idx = jnp.arange(C, dtype=jnp.float32)
    diff = idx[:, None] - idx[None, :]
    intra = jnp.where(diff >= 0, jnp.exp(log_gamma[:, None, None] * jnp.maximum(diff, 0.0)), 0.0)
    q_dec = jnp.exp(log_gamma[:, None] * (idx + 1.0))[:, :, None]
    k_dec = jnp.exp(log_gamma[:, None] * (C - 1.0 - idx))[:, :, None]
    c_dec = jnp.exp(log_gamma * C)[:, None, None]

    def step(S, qkv):
        qc, kc, vc = qkv
        scores = jnp.einsum('bhid,bhjd->bhij', qc, kc) * intra
        o = jnp.einsum('bhij,bhjv->bhiv', scores, vc) + jnp.einsum('bhid,bhdv->bhiv', qc * q_dec, S)
        S = S * c_dec + jnp.einsum('bhjd,bhjv->bhdv', kc * k_dec, vc)
        return S, o

    S, o = lax.scan(step, s0, (to_chunks(q), to_chunks(k), to_chunks(v)))
    return o.transpose(1, 2, 0, 3, 4).reshape(B, H, N, dv), S


def head_group_norm(o):
    mu = jnp.mean(o, axis=-1, keepdims=True)
    var = jnp.mean(jnp.square(o - mu), axis=-1, keepdims=True)
    o = (o - mu) * lax.rsqrt(var + GN_EPS)
    B, H, N, dv = o.shape
    return o.transpose(0, 2, 1, 3).reshape(B, N, H * dv)


def retention_mixer(u_x, u_c, ret_cos, ret_sin, w_qkv, w_g, decay_logit, w_o, ctx_out):
    H, dk, dv = RET_HEADS, RET_DK, RET_DV

    def project(u):
        B, N, _ = u.shape
        q, k, v = jnp.split(u @ w_qkv, [H * dk, 2 * H * dk], axis=-1)
        return q.reshape(B, N, H, dk), k.reshape(B, N, H, dk) * (dk ** -0.5), v.reshape(B, N, H, dv)

    def heads_first(t):
        return t.astype(jnp.float32).transpose(0, 2, 1, 3)

    q_x, k_x, v_x = project(u_x)
    q_x = apply_rope(q_x, ret_cos, ret_sin)
    k_x = apply_rope(k_x, ret_cos, ret_sin)
    q_x, k_x, v_x = heads_first(q_x), heads_first(k_x), heads_first(v_x)
    q_c, k_c, v_c = project(u_c)
    q_c, k_c, v_c = heads_first(q_c), heads_first(k_c), heads_first(v_c)

    log_gamma = jax.nn.log_sigmoid(decay_logit.astype(jnp.float32))
    s0 = jnp.zeros((q_x.shape[0], H, dk, dv), jnp.float32)
    flip = lambda t: jnp.flip(t, axis=2)
    oc_f, sc_f = retention_chunkwise(q_c, k_c, v_c, log_gamma[0], s0)
    ox_f, _ = retention_chunkwise(q_x, k_x, v_x, log_gamma[0], sc_f)
    oc_b, sc_b = retention_chunkwise(flip(q_c), flip(k_c), flip(v_c), log_gamma[1], s0)
    ox_b, _ = retention_chunkwise(flip(q_x), flip(k_x), flip(v_x), log_gamma[1], sc_b)

    def combine(u, o_f, o_b):
        g_f, g_b = jnp.split(u @ w_g, 2, axis=-1)
        y = (jax.nn.silu(g_f) * head_group_norm(o_f).astype(u.dtype)
             + jax.nn.silu(g_b) * head_group_norm(o_b).astype(u.dtype))
        return y @ w_o

    y_x = combine(u_x, ox_f, flip(ox_b))
    y_c = combine(u_c, oc_f, flip(oc_b)) if ctx_out else None
    return y_x, y_c


def mla_attend(qn, qr, kn, kr, v):
    s = jnp.einsum('bqhd,bkhd->bhqk', qn, kn) + jnp.einsum('bqhd,bkd->bhqk', qr, kr)
    p = jax.nn.softmax(s.astype(jnp.float32) * MLA_SCALE, axis=-1)
    return jnp.einsum('bhqk,bkhd->bqhd', p.astype(v.dtype), v)


def mla_block_attention(qn, qr, kn, kr, v):
    B, N, H, _ = qn.shape
    nb = N // MLA_Q_BLOCK

    def blocks(t):
        return t.reshape(B, nb, MLA_Q_BLOCK, *t.shape[2:]).swapaxes(0, 1)

    o = lax.map(lambda qs: mla_attend(qs[0], qs[1], kn, kr, v), (blocks(qn), blocks(qr)))
    return o.swapaxes(0, 1).reshape(B, N, H, MLA_D_V)


def mla_mixer(u_x, u_c, row_cos, row_sin, col_cos, col_sin,
              w_dq, g_q, w_uq, w_dkv, g_kv, w_ukv, w_o, ctx_out):
    H, dn, dr, dv = MLA_HEADS, MLA_D_NOPE, MLA_D_ROPE, MLA_D_V

    def queries(u):
        B, N, _ = u.shape
        q = (rms_norm(u @ w_dq, g_q) @ w_uq).reshape(B, N, H, dn + dr)
        return q[..., :dn], q[..., dn:]

    def keys_values(u):
        B, N, _ = u.shape
        ckv = u @ w_dkv
        c_kv = rms_norm(ckv[..., :MLA_KV_LORA], g_kv)
        kr = ckv[..., MLA_KV_LORA:]
        kv = (c_kv @ w_ukv).reshape(B, N, H, dn + dv)
        return kv[..., :dn], kr, kv[..., dn:]

    qn_x, qr_x = queries(u_x)
    qr_x = apply_rope_2d(qr_x, row_cos, row_sin, col_cos, col_sin)
    kn_x, kr_x, v_x = keys_values(u_x)
    kr_x = apply_rope_2d(kr_x[:, :, None, :], row_cos, row_sin, col_cos, col_sin)[:, :, 0, :]
    kn_c, kr_c, v_c = keys_values(u_c)

    kn = jnp.concatenate([kn_x, kn_c], axis=1)
    kr = jnp.concatenate([kr_x, kr_c], axis=1)
    v = jnp.concatenate([v_x, v_c], axis=1)
    B, N, _ = u_x.shape
    o_x = mla_block_attention(qn_x, qr_x, kn, kr, v)
    y_x = o_x.reshape(B, N, H * dv) @ w_o
    y_c = None
    if ctx_out:
        qn_c, qr_c = queries(u_c)
        o_c = mla_attend(qn_c, qr_c, kn_c, kr_c, v_c)
        y_c = o_c.reshape(B, u_c.shape[1], H * dv) @ w_o
    return y_x, y_c


def swiglu(u, w_in, w_out):
    a, b = jnp.split(u @ w_in, 2, axis=-1)
    return (jax.nn.silu(a) * b) @ w_out


def _fwd_setup_inputs(seed: int = 0) -> dict:
    key = jax.random.key(seed)
    ks = jax.random.split(key, 21)
    f32 = jnp.float32
    D, F = D_MODEL, FFN_HIDDEN

    def nrm(k, shape, scale):
        return jax.random.normal(k, shape, f32) * scale

    ret_qkv_w = 2 * RET_HEADS * RET_DK + RET_HEADS * RET_DV
    base_logit = jnp.log(2.0 ** (5.0 + jnp.arange(RET_HEADS, dtype=f32)) - 1.0)
    return {
        'x': nrm(ks[0], (BATCH, SEQ, D), 1.0),
        'c': nrm(ks[1], (BATCH, D), 1.0),
        'ctx': nrm(ks[2], (BATCH, CTX_LEN, D), 1.0),
        'c_ctx': nrm(ks[3], (D,), 1.0),
        'ada_w': nrm(ks[4], (DEPTH, D, 6 * D), 0.5 * D ** -0.5),
        'ada_b': nrm(ks[5], (DEPTH, 6 * D), 0.02),
        'ln_g': 1.0 + nrm(ks[6], (DEPTH, 2, D), 0.02),
        'ln_b': nrm(ks[7], (DEPTH, 2, D), 0.02),
        'ret_w_qkv': nrm(ks[8], (N_RET_LAYERS, D, ret_qkv_w), D ** -0.5),
        'ret_w_g': nrm(ks[9], (N_RET_LAYERS, D, 2 * RET_HEADS * RET_DV), D ** -0.5),
        'ret_decay_logit': base_logit + nrm(ks[10], (N_RET_LAYERS, 2, RET_HEADS), 0.1),
        'ret_w_o': nrm(ks[11], (N_RET_LAYERS, RET_HEADS * RET_DV, D), DEEPNORM_BETA * (RET_HEADS * RET_DV) ** -0.5),
        'mla_w_dq': nrm(ks[12], (N_MLA_LAYERS, D, MLA_Q_LORA), D ** -0.5),
        'mla_g_q': 1.0 + nrm(ks[13], (N_MLA_LAYERS, MLA_Q_LORA), 0.02),
        'mla_w_uq': nrm(ks[14], (N_MLA_LAYERS, MLA_Q_LORA, MLA_HEADS * (MLA_D_NOPE + MLA_D_ROPE)), MLA_Q_LORA ** -0.5),
        'mla_w_dkv': nrm(ks[15], (N_MLA_LAYERS, D, MLA_KV_LORA + MLA_D_ROPE), D ** -0.5),
        'mla_g_kv': 1.0 + nrm(ks[16], (N_MLA_LAYERS, MLA_KV_LORA), 0.02),
        'mla_w_ukv': nrm(ks[17], (N_MLA_LAYERS, MLA_KV_LORA, MLA_HEADS * (MLA_D_NOPE + MLA_D_V)), MLA_KV_LORA ** -0.5),
        'mla_w_o': nrm(ks[18], (N_MLA_LAYERS, MLA_HEADS * MLA_D_V, D), DEEPNORM_BETA * (MLA_HEADS * MLA_D_V) ** -0.5),
        'ffn_w_in': nrm(ks[19], (DEPTH, D, 2 * F), D ** -0.5),
        'ffn_w_out': nrm(ks[20], (DEPTH, F, D), DEEPNORM_BETA * F ** -0.5),
    }


def _fwd_reference(x, c, ctx, c_ctx, ada_w, ada_b, ln_g, ln_b,
              ret_w_qkv, ret_w_g, ret_decay_logit, ret_w_o,
              mla_w_dq, mla_g_q, mla_w_uq, mla_w_dkv, mla_g_kv, mla_w_ukv, mla_w_o,
              ffn_w_in, ffn_w_out):
    N = x.shape[1]
    ROWS = N // GRID_W
    rows = jnp.repeat(jnp.arange(ROWS), GRID_W)
    cols = jnp.tile(jnp.arange(GRID_W), ROWS)
    t = jnp.arange(N)

    axial_dim = MLA_D_ROPE // 2
    axial_inv = AXIAL_ROPE_BASE ** (-jnp.arange(axial_dim // 2, dtype=jnp.float32) * 2.0 / axial_dim)
    row_cos, row_sin = rope_tables(rows, axial_inv)
    col_cos, col_sin = rope_tables(cols, axial_inv)
    ret_inv = RET_ROPE_BASE ** (-jnp.linspace(0.0, 1.0, RET_DK // 2, dtype=jnp.float32))
    ret_cos, ret_sin = rope_tables(t, ret_inv)

    h_x, h_c = x, ctx
    for i in range(DEPTH):
        ctx_out = i < DEPTH - 1
        sh_a, sc_a, g_a, sh_f, sc_f, g_f = [m[:, None, :] for m in modulation(c, ada_w[i], ada_b[i])]
        csh_a, csc_a, cg_a, csh_f, csc_f, cg_f = modulation(c_ctx, ada_w[i], ada_b[i])

        u_x = h_x * (1.0 + sc_a) + sh_a
        u_c = h_c * (1.0 + csc_a) + csh_a
        j = i // N_MIXERS
        if i % N_MIXERS == 0:
            y_x, y_c = retention_mixer(u_x, u_c, ret_cos, ret_sin, ret_w_qkv[j], ret_w_g[j],
                                       ret_decay_logit[j], ret_w_o[j], ctx_out)
        else:
            y_x, y_c = mla_mixer(u_x, u_c, row_cos, row_sin, col_cos, col_sin,
                                 mla_w_dq[j], mla_g_q[j], mla_w_uq[j], mla_w_dkv[j], mla_g_kv[j],
                                 mla_w_ukv[j], mla_w_o[j], ctx_out)
        h_x = layer_norm(DEEPNORM_ALPHA * h_x + g_a * y_x, ln_g[i, 0], ln_b[i, 0])
        f_x = swiglu(h_x * (1.0 + sc_f) + sh_f, ffn_w_in[i], ffn_w_out[i])
        h_x = layer_norm(DEEPNORM_ALPHA * h_x + g_f * f_x, ln_g[i, 1], ln_b[i, 1])
        if ctx_out:
            h_c = layer_norm(DEEPNORM_ALPHA * h_c + cg_a * y_c, ln_g[i, 0], ln_b[i, 0])
            f_c = swiglu(h_c * (1.0 + csc_f) + csh_f, ffn_w_in[i], ffn_w_out[i])
            h_c = layer_norm(DEEPNORM_ALPHA * h_c + cg_f * f_c, ln_g[i, 1], ln_b[i, 1])
    return h_x


import jax as _jax
import jax.numpy as _jnp

TWIN_FORMAT = 'train_step'
FWD_PARAMS = ['x', 'c', 'ctx', 'c_ctx', 'ada_w', 'ada_b', 'ln_g', 'ln_b', 'ret_w_qkv', 'ret_w_g', 'ret_decay_logit', 'ret_w_o', 'mla_w_dq', 'mla_g_q', 'mla_w_uq', 'mla_w_dkv', 'mla_g_kv', 'mla_w_ukv', 'mla_w_o', 'ffn_w_in', 'ffn_w_out']
TWIN_WEIGHTS = ['c_ctx', 'ada_w', 'ada_b', 'ln_g', 'ln_b', 'ret_w_qkv', 'ret_w_g', 'ret_decay_logit', 'ret_w_o', 'mla_w_dq', 'mla_g_q', 'mla_w_uq', 'mla_w_dkv', 'mla_g_kv', 'mla_w_ukv', 'mla_w_o', 'ffn_w_in', 'ffn_w_out']
TWIN_DIFF_INPUT = 'x'
TWIN_INPUTS = ['x', 'c', 'ctx', 'c_ctx', 'ada_w', 'ada_b', 'ln_g', 'ln_b', 'ret_w_qkv', 'ret_w_g', 'ret_decay_logit', 'ret_w_o', 'mla_w_dq', 'mla_g_q', 'mla_w_uq', 'mla_w_dkv', 'mla_g_kv', 'mla_w_ukv', 'mla_w_o', 'ffn_w_in', 'ffn_w_out', 'loss_target', 'm_c_ctx', 'm_ada_w', 'm_ada_b', 'm_ln_g', 'm_ln_b', 'm_ret_w_qkv', 'm_ret_w_g', 'm_ret_decay_logit', 'm_ret_w_o', 'm_mla_w_dq', 'm_mla_g_q', 'm_mla_w_uq', 'm_mla_w_dkv', 'm_mla_g_kv', 'm_mla_w_ukv', 'm_mla_w_o', 'm_ffn_w_in', 'm_ffn_w_out', 'v_c_ctx', 'v_ada_w', 'v_ada_b', 'v_ln_g', 'v_ln_b', 'v_ret_w_qkv', 'v_ret_w_g', 'v_ret_decay_logit', 'v_ret_w_o', 'v_mla_w_dq', 'v_mla_g_q', 'v_mla_w_uq', 'v_mla_w_dkv', 'v_mla_g_kv', 'v_mla_w_ukv', 'v_mla_w_o', 'v_ffn_w_in', 'v_ffn_w_out']
TWIN_OUTPUTS = ['loss', 'grad_x', 'grad_c_ctx', 'grad_ada_w', 'grad_ada_b', 'grad_ln_g', 'grad_ln_b', 'grad_ret_w_qkv', 'grad_ret_w_g', 'grad_ret_decay_logit', 'grad_ret_w_o', 'grad_mla_w_dq', 'grad_mla_g_q', 'grad_mla_w_uq', 'grad_mla_w_dkv', 'grad_mla_g_kv', 'grad_mla_w_ukv', 'grad_mla_w_o', 'grad_ffn_w_in', 'grad_ffn_w_out', 'delta_c_ctx', 'delta_ada_w', 'delta_ada_b', 'delta_ln_g', 'delta_ln_b', 'delta_ret_w_qkv', 'delta_ret_w_g', 'delta_ret_decay_logit', 'delta_ret_w_o', 'delta_mla_w_dq', 'delta_mla_g_q', 'delta_mla_w_uq', 'delta_mla_w_dkv', 'delta_mla_g_kv', 'delta_mla_w_ukv', 'delta_mla_w_o', 'delta_ffn_w_in', 'delta_ffn_w_out', 'new_m_c_ctx', 'new_m_ada_w', 'new_m_ada_b', 'new_m_ln_g', 'new_m_ln_b', 'new_m_ret_w_qkv', 'new_m_ret_w_g', 'new_m_ret_decay_logit', 'new_m_ret_w_o', 'new_m_mla_w_dq', 'new_m_mla_g_q', 'new_m_mla_w_uq', 'new_m_mla_w_dkv', 'new_m_mla_g_kv', 'new_m_mla_w_ukv', 'new_m_mla_w_o', 'new_m_ffn_w_in', 'new_m_ffn_w_out', 'new_v_c_ctx', 'new_v_ada_w', 'new_v_ada_b', 'new_v_ln_g', 'new_v_ln_b', 'new_v_ret_w_qkv', 'new_v_ret_w_g', 'new_v_ret_decay_logit', 'new_v_ret_w_o', 'new_v_mla_w_dq', 'new_v_mla_g_q', 'new_v_mla_w_uq', 'new_v_mla_w_dkv', 'new_v_mla_g_kv', 'new_v_mla_w_ukv', 'new_v_mla_w_o', 'new_v_ffn_w_in', 'new_v_ffn_w_out']
TWIN_LEAF_KINDS = {'loss': 'loss', 'grad_x': 'grad_x', 'grad_c_ctx': 'grad_w', 'grad_ada_w': 'grad_w', 'grad_ada_b': 'grad_w', 'grad_ln_g': 'grad_w', 'grad_ln_b': 'grad_w', 'grad_ret_w_qkv': 'grad_w', 'grad_ret_w_g': 'grad_w', 'grad_ret_decay_logit': 'grad_w', 'grad_ret_w_o': 'grad_w', 'grad_mla_w_dq': 'grad_w', 'grad_mla_g_q': 'grad_w', 'grad_mla_w_uq': 'grad_w', 'grad_mla_w_dkv': 'grad_w', 'grad_mla_g_kv': 'grad_w', 'grad_mla_w_ukv': 'grad_w', 'grad_mla_w_o': 'grad_w', 'grad_ffn_w_in': 'grad_w', 'grad_ffn_w_out': 'grad_w', 'delta_c_ctx': 'delta_w', 'delta_ada_w': 'delta_w', 'delta_ada_b': 'delta_w', 'delta_ln_g': 'delta_w', 'delta_ln_b': 'delta_w', 'delta_ret_w_qkv': 'delta_w', 'delta_ret_w_g': 'delta_w', 'delta_ret_decay_logit': 'delta_w', 'delta_ret_w_o': 'delta_w', 'delta_mla_w_dq': 'delta_w', 'delta_mla_g_q': 'delta_w', 'delta_mla_w_uq': 'delta_w', 'delta_mla_w_dkv': 'delta_w', 'delta_mla_g_kv': 'delta_w', 'delta_mla_w_ukv': 'delta_w', 'delta_mla_w_o': 'delta_w', 'delta_ffn_w_in': 'delta_w', 'delta_ffn_w_out': 'delta_w', 'new_m_c_ctx': 'new_m', 'new_m_ada_w': 'new_m', 'new_m_ada_b': 'new_m', 'new_m_ln_g': 'new_m', 'new_m_ln_b': 'new_m', 'new_m_ret_w_qkv': 'new_m', 'new_m_ret_w_g': 'new_m', 'new_m_ret_decay_logit': 'new_m', 'new_m_ret_w_o': 'new_m', 'new_m_mla_w_dq': 'new_m', 'new_m_mla_g_q': 'new_m', 'new_m_mla_w_uq': 'new_m', 'new_m_mla_w_dkv': 'new_m', 'new_m_mla_g_kv': 'new_m', 'new_m_mla_w_ukv': 'new_m', 'new_m_mla_w_o': 'new_m', 'new_m_ffn_w_in': 'new_m', 'new_m_ffn_w_out': 'new_m', 'new_v_c_ctx': 'new_v', 'new_v_ada_w': 'new_v', 'new_v_ada_b': 'new_v', 'new_v_ln_g': 'new_v', 'new_v_ln_b': 'new_v', 'new_v_ret_w_qkv': 'new_v', 'new_v_ret_w_g': 'new_v', 'new_v_ret_decay_logit': 'new_v', 'new_v_ret_w_o': 'new_v', 'new_v_mla_w_dq': 'new_v', 'new_v_mla_g_q': 'new_v', 'new_v_mla_w_uq': 'new_v', 'new_v_mla_w_dkv': 'new_v', 'new_v_mla_g_kv': 'new_v', 'new_v_mla_w_ukv': 'new_v', 'new_v_mla_w_o': 'new_v', 'new_v_ffn_w_in': 'new_v', 'new_v_ffn_w_out': 'new_v'}


def _forward(args):
    return _fwd_reference(*[args[k] for k in FWD_PARAMS])


def _output_shape():
    def fwd():
        inp = _fwd_setup_inputs(0)
        return _fwd_reference(*[inp[k] for k in FWD_PARAMS])
    out = _jax.eval_shape(fwd)
    return out.shape, out.dtype

N_MICROBATCH = 1
ADAM_LR = 0.001
ADAM_B1 = 0.9
ADAM_B2 = 0.999
ADAM_EPS = 1e-08
ADAM_WD = 0.01
ADAM_STEP = 10
PER_EXAMPLE_BATCH_AXIS = {'x': 0, 'c': 0, 'ctx': 0, 'loss_target': 0}
SHARED_INPUTS = []
_WEIGHT_DTYPES = {'c_ctx': _jnp.float32, 'ada_w': _jnp.float32, 'ada_b': _jnp.float32, 'ln_g': _jnp.float32, 'ln_b': _jnp.float32, 'ret_w_qkv': _jnp.float32, 'ret_w_g': _jnp.float32, 'ret_decay_logit': _jnp.float32, 'ret_w_o': _jnp.float32, 'mla_w_dq': _jnp.float32, 'mla_g_q': _jnp.float32, 'mla_w_uq': _jnp.float32, 'mla_w_dkv': _jnp.float32, 'mla_g_kv': _jnp.float32, 'mla_w_ukv': _jnp.float32, 'mla_w_o': _jnp.float32, 'ffn_w_in': _jnp.float32, 'ffn_w_out': _jnp.float32}
MOMENT_SCALE = {'c_ctx': 6.920298e-03, 'ada_w': 7.332083e-03, 'ada_b': 1.250074e-02, 'ln_g': 5.710767e+00, 'ln_b': 3.987988e-01, 'ret_w_qkv': 5.738574e-03, 'ret_w_g': 3.191288e-03, 'ret_decay_logit': 3.808603e-02, 'ret_w_o': 1.511959e-02, 'mla_w_dq': 1.172159e-03, 'mla_g_q': 1.156809e-03, 'mla_w_uq': 4.774483e-04, 'mla_w_dkv': 4.538925e-03, 'mla_g_kv': 4.900684e-03, 'mla_w_ukv': 1.687941e-03, 'mla_w_o': 5.820294e-03, 'ffn_w_in': 2.917258e-03, 'ffn_w_out': 1.133282e-02}


def _to_microbatches(a, axis):
    t = _jnp.moveaxis(a, axis, 0)
    t = t.reshape((N_MICROBATCH, t.shape[0] // N_MICROBATCH) + t.shape[1:])
    return _jnp.moveaxis(t, 1, axis + 1)


def setup_inputs(seed: int = 0) -> dict:
    inp = _fwd_setup_inputs(seed)
    key = _jax.random.fold_in(_jax.random.key(seed), 7919)
    shape, _ = _output_shape()
    out = dict(inp)
    out["loss_target"] = _jax.random.normal(_jax.random.fold_in(key, 0), shape, _jnp.float32)
    for i, name in enumerate(TWIN_WEIGHTS):
        w = inp[name].astype(_jnp.float32)
        if MOMENT_SCALE is None:
            s = _jnp.sqrt(_jnp.mean(_jnp.square(w)) + 1e-30)
        else:
            s = MOMENT_SCALE[name]
        km, kv = _jax.random.split(_jax.random.fold_in(key, i + 1))
        out[name] = w
        out["m_" + name] = s * _jax.random.normal(km, w.shape, _jnp.float32)
        out["v_" + name] = (s * s) * _jax.random.uniform(kv, w.shape, _jnp.float32, 0.5, 1.5)
    if N_MICROBATCH > 1:
        for name, axis in PER_EXAMPLE_BATCH_AXIS.items():
            out[name] = _to_microbatches(out[name], axis)
    return {'x': out['x'], 'c': out['c'], 'ctx': out['ctx'], 'c_ctx': out['c_ctx'], 'ada_w': out['ada_w'], 'ada_b': out['ada_b'], 'ln_g': out['ln_g'], 'ln_b': out['ln_b'], 'ret_w_qkv': out['ret_w_qkv'], 'ret_w_g': out['ret_w_g'], 'ret_decay_logit': out['ret_decay_logit'], 'ret_w_o': out['ret_w_o'], 'mla_w_dq': out['mla_w_dq'], 'mla_g_q': out['mla_g_q'], 'mla_w_uq': out['mla_w_uq'], 'mla_w_dkv': out['mla_w_dkv'], 'mla_g_kv': out['mla_g_kv'], 'mla_w_ukv': out['mla_w_ukv'], 'mla_w_o': out['mla_w_o'], 'ffn_w_in': out['ffn_w_in'], 'ffn_w_out': out['ffn_w_out'], 'loss_target': out['loss_target'], 'm_c_ctx': out['m_c_ctx'], 'm_ada_w': out['m_ada_w'], 'm_ada_b': out['m_ada_b'], 'm_ln_g': out['m_ln_g'], 'm_ln_b': out['m_ln_b'], 'm_ret_w_qkv': out['m_ret_w_qkv'], 'm_ret_w_g': out['m_ret_w_g'], 'm_ret_decay_logit': out['m_ret_decay_logit'], 'm_ret_w_o': out['m_ret_w_o'], 'm_mla_w_dq': out['m_mla_w_dq'], 'm_mla_g_q': out['m_mla_g_q'], 'm_mla_w_uq': out['m_mla_w_uq'], 'm_mla_w_dkv': out['m_mla_w_dkv'], 'm_mla_g_kv': out['m_mla_g_kv'], 'm_mla_w_ukv': out['m_mla_w_ukv'], 'm_mla_w_o': out['m_mla_w_o'], 'm_ffn_w_in': out['m_ffn_w_in'], 'm_ffn_w_out': out['m_ffn_w_out'], 'v_c_ctx': out['v_c_ctx'], 'v_ada_w': out['v_ada_w'], 'v_ada_b': out['v_ada_b'], 'v_ln_g': out['v_ln_g'], 'v_ln_b': out['v_ln_b'], 'v_ret_w_qkv': out['v_ret_w_qkv'], 'v_ret_w_g': out['v_ret_w_g'], 'v_ret_decay_logit': out['v_ret_decay_logit'], 'v_ret_w_o': out['v_ret_w_o'], 'v_mla_w_dq': out['v_mla_w_dq'], 'v_mla_g_q': out['v_mla_g_q'], 'v_mla_w_uq': out['v_mla_w_uq'], 'v_mla_w_dkv': out['v_mla_w_dkv'], 'v_mla_g_kv': out['v_mla_g_kv'], 'v_mla_w_ukv': out['v_mla_w_ukv'], 'v_mla_w_o': out['v_mla_w_o'], 'v_ffn_w_in': out['v_ffn_w_in'], 'v_ffn_w_out': out['v_ffn_w_out']}


def _loss(weights, diff, rest, loss_target):
    with _jax.named_scope("forward"):
        args = {**rest, TWIN_DIFF_INPUT: diff, **{k: w.astype(_WEIGHT_DTYPES[k]) for k, w in weights.items()}}
        y = _forward(args)
    with _jax.named_scope("loss_head"):
        err = _jnp.square(y.astype(_jnp.float32) - loss_target)
        return 0.5 * _jnp.sum(_jnp.mean(err, axis=-1)) if err.ndim else 0.5 * err


def _adamw(w, g, m, v):
    m = ADAM_B1 * m + (1.0 - ADAM_B1) * g
    v = ADAM_B2 * v + (1.0 - ADAM_B2) * _jnp.square(g)
    m_hat = m / (1.0 - ADAM_B1 ** ADAM_STEP)
    v_hat = v / (1.0 - ADAM_B2 ** ADAM_STEP)
    delta = -ADAM_LR * (m_hat / (_jnp.sqrt(v_hat) + ADAM_EPS) + ADAM_WD * w)
    return delta, m, v


def reference(x, c, ctx, c_ctx, ada_w, ada_b, ln_g, ln_b, ret_w_qkv, ret_w_g, ret_decay_logit, ret_w_o, mla_w_dq, mla_g_q, mla_w_uq, mla_w_dkv, mla_g_kv, mla_w_ukv, mla_w_o, ffn_w_in, ffn_w_out, loss_target, m_c_ctx, m_ada_w, m_ada_b, m_ln_g, m_ln_b, m_ret_w_qkv, m_ret_w_g, m_ret_decay_logit, m_ret_w_o, m_mla_w_dq, m_mla_g_q, m_mla_w_uq, m_mla_w_dkv, m_mla_g_kv, m_mla_w_ukv, m_mla_w_o, m_ffn_w_in, m_ffn_w_out, v_c_ctx, v_ada_w, v_ada_b, v_ln_g, v_ln_b, v_ret_w_qkv, v_ret_w_g, v_ret_decay_logit, v_ret_w_o, v_mla_w_dq, v_mla_g_q, v_mla_w_uq, v_mla_w_dkv, v_mla_g_kv, v_mla_w_ukv, v_mla_w_o, v_ffn_w_in, v_ffn_w_out):
    given = dict(x=x, c=c, ctx=ctx, c_ctx=c_ctx, ada_w=ada_w, ada_b=ada_b, ln_g=ln_g, ln_b=ln_b, ret_w_qkv=ret_w_qkv, ret_w_g=ret_w_g, ret_decay_logit=ret_decay_logit, ret_w_o=ret_w_o, mla_w_dq=mla_w_dq, mla_g_q=mla_g_q, mla_w_uq=mla_w_uq, mla_w_dkv=mla_w_dkv, mla_g_kv=mla_g_kv, mla_w_ukv=mla_w_ukv, mla_w_o=mla_w_o, ffn_w_in=ffn_w_in, ffn_w_out=ffn_w_out, loss_target=loss_target, m_c_ctx=m_c_ctx, m_ada_w=m_ada_w, m_ada_b=m_ada_b, m_ln_g=m_ln_g, m_ln_b=m_ln_b, m_ret_w_qkv=m_ret_w_qkv, m_ret_w_g=m_ret_w_g, m_ret_decay_logit=m_ret_decay_logit, m_ret_w_o=m_ret_w_o, m_mla_w_dq=m_mla_w_dq, m_mla_g_q=m_mla_g_q, m_mla_w_uq=m_mla_w_uq, m_mla_w_dkv=m_mla_w_dkv, m_mla_g_kv=m_mla_g_kv, m_mla_w_ukv=m_mla_w_ukv, m_mla_w_o=m_mla_w_o, m_ffn_w_in=m_ffn_w_in, m_ffn_w_out=m_ffn_w_out, v_c_ctx=v_c_ctx, v_ada_w=v_ada_w, v_ada_b=v_ada_b, v_ln_g=v_ln_g, v_ln_b=v_ln_b, v_ret_w_qkv=v_ret_w_qkv, v_ret_w_g=v_ret_w_g, v_ret_decay_logit=v_ret_decay_logit, v_ret_w_o=v_ret_w_o, v_mla_w_dq=v_mla_w_dq, v_mla_g_q=v_mla_g_q, v_mla_w_uq=v_mla_w_uq, v_mla_w_dkv=v_mla_w_dkv, v_mla_g_kv=v_mla_g_kv, v_mla_w_ukv=v_mla_w_ukv, v_mla_w_o=v_mla_w_o, v_ffn_w_in=v_ffn_w_in, v_ffn_w_out=v_ffn_w_out)
    weights = {n: given[n] for n in TWIN_WEIGHTS}
    shared = {n: given[n] for n in SHARED_INPUTS}
    per_example = {n: given[n] for n in ['x', 'c', 'ctx']}
    grad_fn = _jax.value_and_grad(_loss, argnums=(0, 1))

    def one_microbatch(ex, loss_target):
        ex = dict(ex)
        diff = ex.pop(TWIN_DIFF_INPUT)
        return grad_fn(weights, diff, {**shared, **ex}, loss_target)

    if N_MICROBATCH == 1:
        loss, (grad_w, grad_x) = one_microbatch(per_example, given["loss_target"])
    else:
        def body(carry, xs):
            loss_sum, grad_sum = carry
            l_k, (gw_k, gx_k) = one_microbatch(xs[0], xs[1])
            with _jax.named_scope("update"):
                return (loss_sum + l_k, _jax.tree.map(_jnp.add, grad_sum, gw_k)), gx_k

        init = (_jnp.zeros((), _jnp.float32), _jax.tree.map(_jnp.zeros_like, weights))
        (loss, grad_w), grad_x = _jax.lax.scan(body, init, (per_example, given["loss_target"]))
    with _jax.named_scope("update"):
        delta_w, new_m, new_v = {}, {}, {}
        for n in TWIN_WEIGHTS:
            delta_w[n], new_m[n], new_v[n] = _adamw(weights[n], grad_w[n], given["m_" + n], given["v_" + n])
    return (loss, grad_x, *[grad_w[n] for n in TWIN_WEIGHTS], *[delta_w[n] for n in TWIN_WEIGHTS],
            *[new_m[n] for n in TWIN_WEIGHTS], *[new_v[n] for n in TWIN_WEIGHTS])
```

```python
import functools
import math

import jax
import jax.numpy as jnp
from jax import lax
from jax.experimental import pallas as pl
from jax.experimental.pallas import tpu as pltpu

F32 = jnp.float32
BF16 = jnp.bfloat16

AXES = ("x", "y", "c")
N_DEV = 8
MESH_IDS = pl.DeviceIdType.MESH

DEPTH = 4
GRID_W = 64
RET_HEADS = 8
RET_CHUNK = 128
RET_ROPE_BASE = 10000.0
GN_EPS = 1e-6
MLA_HEADS = 16
MLA_D_NOPE = 128
MLA_D_ROPE = 64
MLA_D_V = 128
AXIAL_ROPE_BASE = 10000.0
RMS_EPS = 1e-6
LN_EPS = 1e-5
DEEPNORM_ALPHA = (2 * DEPTH) ** 0.25

ADAM_LR = 0.001
ADAM_B1 = 0.9
ADAM_B2 = 0.999
ADAM_EPS = 1e-08
ADAM_WD = 0.01
ADAM_STEP = 10

LANE = 128
SUBLANE = 8
VMEM_LIMIT = 56 * 1024 * 1024

ROW_TILE = 256
WIDE_ROW_TILE = 64
MM_ROWS = 544
MLA_PAD = 128


def _div(n, cap, mult):
    best = None
    for d in range(mult, min(n, cap) + 1, mult):
        if n % d == 0:
            best = d
    return n if best is None else best


def _cparams(sem=None):
    kw = dict(vmem_limit_bytes=VMEM_LIMIT)
    if sem is not None:
        kw["dimension_semantics"] = sem
    return pltpu.CompilerParams(**kw)


_DN = {"nn": (((1,), (0,)), ((), ())), "nt": (((1,), (1,)), ((), ())), "tn": (((0,), (0,)), ((), ()))}


def _raw_dot(a, b, kind):
    return lax.dot_general(a.astype(BF16), b.astype(BF16), _DN[kind], preferred_element_type=F32)


@functools.partial(jax.custom_vjp, nondiff_argnums=(2,))
def _dot(a, b, kind):
    return _raw_dot(a, b, kind)


def _dot_fwd(a, b, kind):
    return _raw_dot(a, b, kind), (a, b)


def _dot_bwd(kind, res, g):
    a, b = res
    if kind == "nn":
        return _raw_dot(g, b, "nt"), _raw_dot(a, g, "tn")
    if kind == "nt":
        return _raw_dot(g, b, "nn"), _raw_dot(g, a, "tn")
    return _raw_dot(b, g, "nt"), _raw_dot(a, g, "nn")


_dot.defvjp(_dot_fwd, _dot_bwd)


def _sigmoid(x):
    return 1.0 / (1.0 + jnp.exp(-x))


def _silu(x):
    return x * _sigmoid(x)


def _my_index():
    return 4 * lax.axis_index("x") + 2 * lax.axis_index("y") + lax.axis_index("c")


def _all_gather_call(x, name):
    def body(x_ref, out_ref, send_sems, recv_sems, local_sem):
        x_, y_, c_ = lax.axis_index("x"), lax.axis_index("y"), lax.axis_index("c")
        me, sibling = (x_, y_, c_), (x_, y_, 1 - c_)
        chips = [(1 - x_, y_), (x_, 1 - y_), (1 - x_, 1 - y_)]

        def slot(px, py, pc):
            return out_ref.at[4 * px + 2 * py + pc]

        def copy(k, block, to, src=None):
            return pltpu.make_async_remote_copy(
                src_ref=slot(*block) if src is None else src, dst_ref=slot(*block),
                send_sem=send_sems.at[k], recv_sem=recv_sems.at[k],
                device_id=to, device_id_type=MESH_IDS)

        mine = pltpu.make_async_copy(x_ref, slot(*me), local_sem)
        mine.start()
        first = [copy(0, me, sibling, src=x_ref)]
        first += [copy(1 + j, me, (*chip, c_), src=x_ref) for j, chip in enumerate(chips)]
        for cp in first:
            cp.start()
        passed = [copy(4 + j, (*chip, c_), sibling) for j, chip in enumerate(chips)]
        for j, chip in enumerate(chips):
            copy(1 + j, (*chip, c_), me).wait_recv()
            passed[j].start()
        copy(0, sibling, me).wait_recv()
        for j, chip in enumerate(chips):
            copy(4 + j, (*chip, 1 - c_), me).wait_recv()
        for cp in first + passed:
            cp.wait_send()
        mine.wait()

    return pl.pallas_call(
        body, name=name,
        out_shape=jax.ShapeDtypeStruct((N_DEV,) + x.shape, x.dtype),
        in_specs=[pl.BlockSpec(memory_space=pl.ANY)],
        out_specs=pl.BlockSpec(memory_space=pl.ANY),
        scratch_shapes=[pltpu.SemaphoreType.DMA((7,)), pltpu.SemaphoreType.DMA((7,)), pltpu.SemaphoreType.DMA],
    )(x)


def _all_to_all_call(g, name):
    def body(g_ref, out_ref, send_sems, recv_sems, local_sem):
        x_, y_, c_ = lax.axis_index("x"), lax.axis_index("y"), lax.axis_index("c")
        my = 4 * x_ + 2 * y_ + c_
        peers = []
        for k in range(1, N_DEV):
            px = 1 - x_ if (k >> 2) & 1 else x_
            py = 1 - y_ if (k >> 1) & 1 else y_
            pc = 1 - c_ if k & 1 else c_
            peers.append((px, py, pc))

        def copy(k, peer):
            pid = 4 * peer[0] + 2 * peer[1] + peer[2]
            return pltpu.make_async_remote_copy(
                src_ref=g_ref.at[pid], dst_ref=out_ref.at[my],
                send_sem=send_sems.at[k], recv_sem=recv_sems.at[k],
                device_id=peer, device_id_type=MESH_IDS)

        def arrival(k, peer):
            pid = 4 * peer[0] + 2 * peer[1] + peer[2]
            return pltpu.make_async_remote_copy(
                src_ref=g_ref.at[pid], dst_ref=out_ref.at[pid],
                send_sem=send_sems.at[k], recv_sem=recv_sems.at[k],
                device_id=peer, device_id_type=MESH_IDS)

        mine = pltpu.make_async_copy(g_ref.at[my], out_ref.at[my], local_sem)
        mine.start()
        sends = [copy(k, peer) for k, peer in enumerate(peers)]
        for cp in sends:
            cp.start()
        for k, peer in enumerate(peers):
            arrival(k, peer).wait_recv()
        for cp in sends:
            cp.wait_send()
        mine.wait()

    return pl.pallas_call(
        body, name=name,
        out_shape=jax.ShapeDtypeStruct(g.shape, g.dtype),
        in_specs=[pl.BlockSpec(memory_space=pl.ANY)],
        out_specs=pl.BlockSpec(memory_space=pl.ANY),
        scratch_shapes=[pltpu.SemaphoreType.DMA((7,)), pltpu.SemaphoreType.DMA((7,)), pltpu.SemaphoreType.DMA],
    )(g)


def _sum_slots_call(g, name):
    _, rows, cols = g.shape
    tr = _div(rows, max(SUBLANE, (1 << 19) // cols), 16) if rows % 16 == 0 else rows

    def body(g_ref, o_ref):
        acc = g_ref[0].astype(F32)
        for s in range(1, N_DEV):
            acc = acc + g_ref[s].astype(F32)
        o_ref[...] = acc

    return pl.pallas_call(
        body, name=name, grid=(rows // tr,),
        out_shape=jax.ShapeDtypeStruct((rows, cols), F32),
        in_specs=[pl.BlockSpec((N_DEV, tr, cols), lambda i: (0, i, 0))],
        out_specs=pl.BlockSpec((tr, cols), lambda i: (i, 0)),
        compiler_params=_cparams(("parallel",)),
    )(g)


def _make_gather(tag, travel_dtype):
    @jax.custom_vjp
    def gather(w):
        return _all_gather_call(w.astype(travel_dtype), "ag_" + tag)

    def fwd(w):
        return gather(w), None

    def bwd(_, g):
        got = _all_to_all_call(g.astype(travel_dtype), "a2a_" + tag)
        return (_sum_slots_call(got, "sum_" + tag),)

    gather.defvjp(fwd, bwd)
    return gather


def _make_replicated(tag):
    @jax.custom_vjp
    def rep(p):
        return p

    def fwd(p):
        return p, None

    def bwd(_, g):
        return (_sum_slots_call(_all_gather_call(g, "ag_" + tag), "sum_" + tag),)

    rep.defvjp(fwd, bwd)
    return rep


def _mm_nn_call(a, w3, name):
    m, k = a.shape
    nb, _, ns = w3.shape
    tm = _div(m, MM_ROWS, SUBLANE)
    tn = _div(ns, max(LANE, (6 << 20) // (2 * k)), LANE)
    tps = ns // tn

    def body(a_ref, w_ref, o_ref):
        o_ref[...] = _raw_dot(a_ref[...], w_ref[...], "nn")

    return pl.pallas_call(
        body, name=name, grid=(nb * tps, m // tm),
        out_shape=jax.ShapeDtypeStruct((m, nb * ns), F32),
        in_specs=[pl.BlockSpec((tm, k), lambda j, i: (i, 0)),
                  pl.BlockSpec((None, k, tn), lambda j, i: (j // tps, 0, j % tps))],
        out_specs=pl.BlockSpec((tm, tn), lambda j, i: (i, j)),
        compiler_params=_cparams(("parallel", "parallel")),
    )(a, w3)


def _mm_nt_call(g, w3, name):
    m, n = g.shape
    nb, k, ns = w3.shape
    tm = _div(m, MM_ROWS, SUBLANE)
    tk = _div(k, 2048, LANE)
    tn = _div(ns, max(LANE, (6 << 20) // (2 * tk)), LANE)
    tps = ns // tn
    nj = nb * tps

    def body(g_ref, w_ref, o_ref, acc):
        j = pl.program_id(2)
        p = _raw_dot(g_ref[...], w_ref[...], "nt")

        @pl.when(j == 0)
        def _():
            acc[...] = p

        @pl.when(j > 0)
        def _():
            acc[...] += p

        @pl.when(j == nj - 1)
        def _():
            o_ref[...] = acc[...]

    return pl.pallas_call(
        body, name=name, grid=(m // tm, k // tk, nj),
        out_shape=jax.ShapeDtypeStruct((m, k), F32),
        in_specs=[pl.BlockSpec((tm, tn), lambda i, kk, j: (i, j)),
                  pl.BlockSpec((None, tk, tn), lambda i, kk, j: (j // tps, kk, j % tps))],
        out_specs=pl.BlockSpec((tm, tk), lambda i, kk, j: (i, kk)),
        scratch_shapes=[pltpu.VMEM((tm, tk), F32)],
        compiler_params=_cparams(("parallel", "parallel", "arbitrary")),
    )(g, w3)


def _mm_tn_call(a, g, nb, out_dtype, name):
    m, k = a.shape
    n = g.shape[1]
    ns = n // nb
    tm = _div(m, MM_ROWS, SUBLANE)
    tk = _div(k, 2048, LANE)
    tn = _div(ns, max(LANE, (12 << 20) // (4 * tk)), LANE)
    tps = ns // tn
    nm = m // tm

    def body(a_ref, g_ref, o_ref, acc):
        i = pl.program_id(2)
        p = _raw_dot(a_ref[...], g_ref[...], "tn")

        @pl.when(i == 0)
        def _():
            acc[...] = p

        @pl.when(i > 0)
        def _():
            acc[...] += p

        @pl.when(i == nm - 1)
        def _():
            o_ref[...] = acc[...].astype(o_ref.dtype)

    return pl.pallas_call(
        body, name=name, grid=(nb * tps, k // tk, nm),
        out_shape=jax.ShapeDtypeStruct((nb, k, ns), out_dtype),
        in_specs=[pl.BlockSpec((tm, tk), lambda j, kk, i: (i, kk)),
                  pl.BlockSpec((tm, tn), lambda j, kk, i: (i, j))],
        out_specs=pl.BlockSpec((None, tk, tn), lambda j, kk, i: (j // tps, kk, j % tps)),
        scratch_shapes=[pltpu.VMEM((tk, tn), F32)],
        compiler_params=_cparams(("parallel", "parallel", "arbitrary")),
    )(a, g)


def _make_mm(tag):
    @jax.custom_vjp
    def mm(a, w3):
        return _mm_nn_call(a, w3, "mm_" + tag)

    def fwd(a, w3):
        return mm(a, w3), (a, w3)

    def bwd(res, g):
        a, w3 = res
        return (_mm_nt_call(g, w3, "mm_da_" + tag),
                _mm_tn_call(a, g, w3.shape[0], w3.dtype, "mm_dw_" + tag))

    mm.defvjp(fwd, bwd)
    return mm


def _make_rowwise(f, kinds, diff, out_cols, tag, nx_rows, tile=ROW_TILE):
    n_in = len(kinds)
    nxt = nx_rows // tile

    def in_spec(kind, arr):
        if kind == "row":
            return pl.BlockSpec((tile, arr.shape[1]), lambda i: (i, 0))
        if kind == "grp":
            return pl.BlockSpec((None, 1, arr.shape[2]), lambda i: (i // nxt, 0, 0))
        return pl.BlockSpec(arr.shape, lambda i: (0,) * arr.ndim)

    def fwd_call(*args):
        t = next(a.shape[0] for a, kd in zip(args, kinds) if kd == "row")

        def body(*refs):
            outs = f(*[r[...] for r in refs[:n_in]])
            for r, o in zip(refs[n_in:], outs):
                r[...] = o

        return pl.pallas_call(
            body, name="rw_" + tag, grid=(t // tile,),
            out_shape=tuple(jax.ShapeDtypeStruct((t, c), F32) for c in out_cols),
            in_specs=[in_spec(kd, a) for kd, a in zip(kinds, args)],
            out_specs=tuple(pl.BlockSpec((tile, c), lambda i: (i, 0)) for c in out_cols),
            compiler_params=_cparams(("parallel",)),
        )(*args)

    def bwd_call(args, cts):
        t = cts[0].shape[0]
        didx = [i for i in range(n_in) if diff[i]]

        def body(*refs):
            i = pl.program_id(0)
            vals = [r[...] for r in refs[:n_in]]
            ct = tuple(r[...] for r in refs[n_in:n_in + len(out_cols)])
            outs = refs[n_in + len(out_cols):]

            def g(*dv):
                full = list(vals)
                for j, v in zip(didx, dv):
                    full[j] = v
                return tuple(f(*full))

            _, vjp = jax.vjp(g, *[vals[j] for j in didx])
            grads = vjp(ct)
            for j, o_ref, d in zip(didx, outs, grads):
                if kinds[j] == "row":
                    o_ref[...] = d
                else:
                    first = (i % nxt == 0) if kinds[j] == "grp" else (i == 0)

                    @pl.when(first)
                    def _(o_ref=o_ref, d=d):
                        o_ref[...] = d

                    @pl.when(jnp.logical_not(first))
                    def _(o_ref=o_ref, d=d):
                        o_ref[...] += d

        def out_spec(j):
            a = args[j]
            if kinds[j] == "row":
                return pl.BlockSpec((tile, a.shape[1]), lambda i: (i, 0))
            if kinds[j] == "grp":
                return pl.BlockSpec((None, 1, a.shape[2]), lambda i: (i // nxt, 0, 0))
            return pl.BlockSpec(a.shape, lambda i: (0,) * a.ndim)

        return pl.pallas_call(
            body, name="rw_bwd_" + tag, grid=(t // tile,),
            out_shape=tuple(jax.ShapeDtypeStruct(args[j].shape, F32) for j in didx),
            in_specs=[in_spec(kd, a) for kd, a in zip(kinds, args)]
            + [pl.BlockSpec((tile, c), lambda i: (i, 0)) for c in out_cols],
            out_specs=tuple(out_spec(j) for j in didx),
            compiler_params=_cparams(("arbitrary",)),
        )(*args, *cts)

    @jax.custom_vjp
    def op(*args):
        return fwd_call(*args)

    def fwd(*args):
        return fwd_call(*args), args

    def bwd(args, cts):
        grads = bwd_call(args, cts)
        full = [None] * n_in
        for j, gr in zip([i for i in range(n_in) if diff[i]], grads):
            full[j] = gr
        return tuple(jnp.zeros_like(a) if gfull is None else gfull for a, gfull in zip(args, full))

    op.defvjp(fwd, bwd)
    return op


def _layer_norm(z, g, b):
    mu = jnp.mean(z, axis=-1, keepdims=True)
    var = jnp.mean(jnp.square(z - mu), axis=-1, keepdims=True)
    return (z - mu) * lax.rsqrt(var + LN_EPS) * g + b


def _f_modulate(h, sc, sh):
    return (h * (1.0 + sc) + sh,)


def _f_ln_res(h, y, gate, g, b):
    return (_layer_norm(DEEPNORM_ALPHA * h + gate * y, g, b),)


def _f_ln_res_mod(h, y, gate, g, b, sc, sh):
    h1 = _layer_norm(DEEPNORM_ALPHA * h + gate * y, g, b)
    return h1, h1 * (1.0 + sc) + sh


def _f_swiglu(ab):
    half = ab.shape[1] // 2
    return (_silu(ab[:, :half]) * ab[:, half:],)


def _f_ret_combine(gates, o_f, o_b):
    hv = o_f.shape[1]
    dv = hv // RET_HEADS

    def gn(o):
        parts = []
        for h in range(RET_HEADS):
            oh = o[:, h * dv:(h + 1) * dv]
            mu = jnp.mean(oh, axis=-1, keepdims=True)
            var = jnp.mean(jnp.square(oh - mu), axis=-1, keepdims=True)
            parts.append((oh - mu) * lax.rsqrt(var + GN_EPS))
        return jnp.concatenate(parts, axis=1)

    return (_silu(gates[:, :hv]) * gn(o_f) + _silu(gates[:, hv:]) * gn(o_b),)


def _rms(x, g):
    return x * lax.rsqrt(jnp.mean(jnp.square(x), axis=-1, keepdims=True) + RMS_EPS) * g


def _f_rms(x, g):
    return (_rms(x, g),)


def _rot(x, cos, sin_signed, perm):
    swapped = jnp.dot(x, perm, precision=lax.Precision.HIGHEST, preferred_element_type=F32)
    return x * cos + swapped * sin_signed


def _f_kv_latent(ckv, g, cos, sin_signed, perm):
    lora = ckv.shape[1] - MLA_PAD
    return _rms(ckv[:, :lora], g), _rot(ckv[:, lora:], cos, sin_signed, perm)


def _f_loss(y, tgt):
    row = 0.5 * jnp.mean(jnp.square(y - tgt), axis=-1, keepdims=True)
    return (jnp.broadcast_to(row, (y.shape[0], LANE)),)


def _ret_step(q, k, v, state, lg, cos, sin, d):
    c, dk = q.shape
    half = dk // 2

    def rope(t):
        t1, t2 = t[:, :half], t[:, half:]
        return jnp.concatenate([t1 * cos - t2 * sin, t1 * sin + t2 * cos], axis=1)

    q = rope(q)
    k = rope(k * (dk ** -0.5))
    sgn = (1 - 2 * d).astype(F32)
    ii = lax.broadcasted_iota(jnp.int32, (c, c), 0).astype(F32)
    jj = lax.broadcasted_iota(jnp.int32, (c, c), 1).astype(F32)
    e = (ii - jj) * sgn
    intra = jnp.where(e >= 0, jnp.exp(lg * jnp.maximum(e, 0.0)), 0.0)
    idx = lax.broadcasted_iota(jnp.int32, (c, 1), 0).astype(F32)
    pos = jnp.where(d == 0, idx, c - 1.0 - idx)
    q_dec = jnp.exp(lg * (pos + 1.0))
    k_dec = jnp.exp(lg * (c - 1.0 - pos))
    c_dec = jnp.exp(lg * float(c))
    scores = _dot(q, k, "nt") * intra
    o = _dot(scores, v, "nn") + _dot(q * q_dec, state, "nn")
    new_state = state * c_dec + _dot(k * k_dec, v, "tn")
    return o, new_state


def _ret_chunk_of(d, p, nxc, nc):
    return (1 - d) * ((p + nxc) % nc) + d * (nc - 1 - p)


def _ret_specs(t, dk, dv, nxc, nc, step_of):
    hq = RET_HEADS * dk // dk
    c = RET_CHUNK

    def chunk(d, h, s):
        return _ret_chunk_of(d, step_of(s), nxc, nc)

    q_spec = pl.BlockSpec((c, dk), lambda d, h, s: (chunk(d, h, s), h))
    k_spec = pl.BlockSpec((c, dk), lambda d, h, s: (chunk(d, h, s), hq + h))
    v_spec = pl.BlockSpec((c, dv), lambda d, h, s: (chunk(d, h, s), (2 * RET_HEADS * dk) // dv + h))
    tab_spec = pl.BlockSpec((c, dk // 2), lambda d, h, s: (chunk(d, h, s), 0))
    lg_spec = pl.BlockSpec((None, SUBLANE, LANE), lambda d, h, s: (d * RET_HEADS + h, 0, 0))
    o_spec = pl.BlockSpec((None, c, dv), lambda d, h, s: (d, chunk(d, h, s), h))
    st_spec = pl.BlockSpec((None, None, None, dk, dv), lambda d, h, s: (d, h, step_of(s), 0, 0))
    return q_spec, k_spec, v_spec, tab_spec, lg_spec, o_spec, st_spec


def _make_retention(tag, nx_rows):
    def dims(qkv):
        t, w = qkv.shape
        dk = w // (4 * RET_HEADS)
        return t, dk, 2 * dk, nx_rows // RET_CHUNK, t // RET_CHUNK

    def fwd_call(qkv, cos, sin, lgb):
        t, dk, dv, nxc, nc = dims(qkv)
        q_spec, k_spec, v_spec, tab_spec, lg_spec, o_spec, st_spec = _ret_specs(t, dk, dv, nxc, nc, lambda s: s)

        def body(q_ref, k_ref, v_ref, cos_ref, sin_ref, lg_ref, o_ref, st_ref, state):
            d = pl.program_id(0)

            @pl.when(pl.program_id(2) == 0)
            def _():
                state[...] = jnp.zeros_like(state)

            st_ref[...] = state[...]
            o, new_state = _ret_step(q_ref[...], k_ref[...], v_ref[...], state[...], lg_ref[0:1, 0:1],
                                     cos_ref[...], sin_ref[...], d)
            o_ref[...] = o
            state[...] = new_state

        return pl.pallas_call(
            body, name="ret_" + tag, grid=(2, RET_HEADS, nc),
            out_shape=(jax.ShapeDtypeStruct((2, t, RET_HEADS * dv), F32),
                       jax.ShapeDtypeStruct((2, RET_HEADS, nc, dk, dv), F32)),
            in_specs=[q_spec, k_spec, v_spec, tab_spec, tab_spec, lg_spec],
            out_specs=(o_spec, st_spec),
            scratch_shapes=[pltpu.VMEM((dk, dv), F32)],
            compiler_params=_cparams(("parallel", "parallel", "arbitrary")),
        )(qkv, qkv, qkv, cos, sin, lgb)

    def bwd_call(qkv, cos, sin, lgb, states, do):
        t, dk, dv, nxc, nc = dims(qkv)
        q_spec, k_spec, v_spec, tab_spec, lg_spec, o_spec, st_spec = _ret_specs(
            t, dk, dv, nxc, nc, lambda s: nc - 1 - s)
        c = RET_CHUNK

        def chunk(d, h, s):
            return _ret_chunk_of(d, nc - 1 - s, nxc, nc)

        dq_spec = pl.BlockSpec((None, c, dk), lambda d, h, s: (d, chunk(d, h, s), h))
        dv_spec = pl.BlockSpec((None, c, dv), lambda d, h, s: (d, chunk(d, h, s), h))

        def body(q_ref, k_ref, v_ref, cos_ref, sin_ref, lg_ref, st_ref, do_ref,
                 dq_ref, dk_ref, dv_ref, dlg_ref, dstate):
            d = pl.program_id(0)
            s = pl.program_id(2)

            @pl.when(s == 0)
            def _():
                dstate[...] = jnp.zeros_like(dstate)

            cos_t, sin_t = cos_ref[...], sin_ref[...]

            def step(q, k, v, state, lg):
                return _ret_step(q, k, v, state, lg, cos_t, sin_t, d)

            _, vjp = jax.vjp(step, q_ref[...], k_ref[...], v_ref[...], st_ref[...], lg_ref[0:1, 0:1])
            dq, dkk, dvv, dst, dlg = vjp((do_ref[...], dstate[...]))
            dq_ref[...] = dq
            dk_ref[...] = dkk
            dv_ref[...] = dvv
            dstate[...] = dst
            corner = jnp.logical_and(lax.broadcasted_iota(jnp.int32, (SUBLANE, LANE), 0) == 0,
                                     lax.broadcasted_iota(jnp.int32, (SUBLANE, LANE), 1) == 0)
            dlg_full = jnp.where(corner, dlg, 0.0)

            @pl.when(s == 0)
            def _():
                dlg_ref[...] = dlg_full

            @pl.when(s > 0)
            def _():
                dlg_ref[...] += dlg_full

        return pl.pallas_call(
            body, name="ret_bwd_" + tag, grid=(2, RET_HEADS, nc),
            out_shape=(jax.ShapeDtypeStruct((2, t, RET_HEADS * dk), F32),
                       jax.ShapeDtypeStruct((2, t, RET_HEADS * dk), F32),
                       jax.ShapeDtypeStruct((2, t, RET_HEADS * dv), F32),
                       jax.ShapeDtypeStruct(lgb.shape, F32)),
            in_specs=[q_spec, k_spec, v_spec, tab_spec, tab_spec, lg_spec, st_spec, o_spec],
            out_specs=(dq_spec, dq_spec, dv_spec, lg_spec),
            scratch_shapes=[pltpu.VMEM((dk, dv), F32)],
            compiler_params=_cparams(("parallel", "parallel", "arbitrary")),
        )(qkv, qkv, qkv, cos, sin, lgb, states, do)

    @jax.custom_vjp
    def ret(qkv, cos, sin, lgb):
        return fwd_call(qkv, cos, sin, lgb)[0]

    def fwd(qkv, cos, sin, lgb):
        o, states = fwd_call(qkv, cos, sin, lgb)
        return o, (qkv, cos, sin, lgb, states)

    def bwd(res, do):
        qkv, cos, sin, lgb, states = res
        dq, dkk, dvv, dlg = bwd_call(qkv, cos, sin, lgb, states, do)
        dqkv = jnp.concatenate([dq[0] + dq[1], dkk[0] + dkk[1], dvv[0] + dvv[1]], axis=1)
        return dqkv, jnp.zeros_like(cos), jnp.zeros_like(sin), dlg

    ret.defvjp(fwd, bwd)
    return ret


MLA_SCALE = (MLA_D_NOPE + MLA_D_ROPE) ** -0.5
MLA_Q_ROWS_FWD = 256
MLA_Q_ROWS_BWD = 128
MASKED = -1e30


def _attend(q, kn, v, kr, cos, sin_signed, perm, bias):
    qn = q[:, :MLA_D_NOPE]
    qr = _rot(q[:, MLA_D_NOPE:], cos, sin_signed, perm)
    s = (_dot(qn, kn, "nt") + _dot(qr, kr, "nt")) * MLA_SCALE + bias
    s = s - jnp.max(s, axis=-1, keepdims=True)
    e = jnp.exp(s)
    p = e / jnp.sum(e, axis=-1, keepdims=True)
    return _dot(p, v, "nn")


def _make_attention(tag, nx_rows):
    hw = MLA_D_NOPE + MLA_PAD

    def bias_for(i, tq, t):
        col = lax.broadcasted_iota(jnp.int32, (1, t), 1)
        ctx_query = i * tq >= nx_rows
        return jnp.where(jnp.logical_and(ctx_query, col < nx_rows), MASKED, 0.0).astype(F32)

    def in_specs(t, tq):
        return [pl.BlockSpec((tq, hw), lambda h, i: (i, h)),
                pl.BlockSpec((t, MLA_D_NOPE), lambda h, i: (0, 2 * h)),
                pl.BlockSpec((t, MLA_D_V), lambda h, i: (0, 2 * h + 1)),
                pl.BlockSpec((t, MLA_PAD), lambda h, i: (0, 0)),
                pl.BlockSpec((tq, MLA_PAD), lambda h, i: (i, 0)),
                pl.BlockSpec((tq, MLA_PAD), lambda h, i: (i, 0)),
                pl.BlockSpec((MLA_PAD, MLA_PAD), lambda h, i: (0, 0))]

    def fwd_call(q, kv, kr, cos, sin_signed, perm):
        t = q.shape[0]
        tq = MLA_Q_ROWS_FWD

        def body(q_ref, kn_ref, v_ref, kr_ref, cos_ref, sin_ref, perm_ref, o_ref):
            bias = bias_for(pl.program_id(1), tq, t)
            o_ref[...] = _attend(q_ref[...], kn_ref[...], v_ref[...], kr_ref[...], cos_ref[...], sin_ref[...],
                                 perm_ref[...], bias)

        return pl.pallas_call(
            body, name="mla_" + tag, grid=(MLA_HEADS, t // tq),
            out_shape=jax.ShapeDtypeStruct((t, MLA_HEADS * MLA_D_V), F32),
            in_specs=in_specs(t, tq),
            out_specs=pl.BlockSpec((tq, MLA_D_V), lambda h, i: (i, h)),
            compiler_params=_cparams(("parallel", "parallel")),
        )(q, kv, kv, kr, cos, sin_signed, perm)

    def bwd_call(q, kv, kr, cos, sin_signed, perm, do):
        t = q.shape[0]
        tq = MLA_Q_ROWS_BWD

        def body(q_ref, kn_ref, v_ref, kr_ref, cos_ref, sin_ref, perm_ref, do_ref,
                 dq_ref, dkn_ref, dv_ref, dkr_ref):
            h, i = pl.program_id(0), pl.program_id(1)
            bias = bias_for(i, tq, t)
            cos_t, sin_t, perm_t = cos_ref[...], sin_ref[...], perm_ref[...]

            def f(q_t, kn_t, v_t, kr_t):
                return _attend(q_t, kn_t, v_t, kr_t, cos_t, sin_t, perm_t, bias)

            _, vjp = jax.vjp(f, q_ref[...], kn_ref[...], v_ref[...], kr_ref[...])
            dq, dkn, dvv, dkr = vjp(do_ref[...])
            dq_ref[...] = dq

            @pl.when(i == 0)
            def _():
                dkn_ref[...] = dkn
                dv_ref[...] = dvv

            @pl.when(i > 0)
            def _():
                dkn_ref[...] += dkn
                dv_ref[...] += dvv

            first = jnp.logical_and(h == 0, i == 0)

            @pl.when(first)
            def _():
                dkr_ref[...] = dkr

            @pl.when(jnp.logical_not(first))
            def _():
                dkr_ref[...] += dkr

        return pl.pallas_call(
            body, name="mla_bwd_" + tag, grid=(MLA_HEADS, t // tq),
            out_shape=(jax.ShapeDtypeStruct(q.shape, F32),
                       jax.ShapeDtypeStruct((t, MLA_HEADS * MLA_D_NOPE), F32),
                       jax.ShapeDtypeStruct((t, MLA_HEADS * MLA_D_V), F32),
                       jax.ShapeDtypeStruct(kr.shape, F32)),
            in_specs=in_specs(t, tq) + [pl.BlockSpec((tq, MLA_D_V), lambda h, i: (i, h))],
            out_specs=(pl.BlockSpec((tq, hw), lambda h, i: (i, h)),
                       pl.BlockSpec((t, MLA_D_NOPE), lambda h, i: (0, h)),
                       pl.BlockSpec((t, MLA_D_V), lambda h, i: (0, h)),
                       pl.BlockSpec((t, MLA_PAD), lambda h, i: (0, 0))),
            compiler_params=_cparams(("arbitrary", "arbitrary")),
        )(q, kv, kv, kr, cos, sin_signed, perm, do)

    @jax.custom_vjp
    def attn(q, kv, kr, cos, sin_signed, perm):
        return fwd_call(q, kv, kr, cos, sin_signed, perm)

    def fwd(q, kv, kr, cos, sin_signed, perm):
        return fwd_call(q, kv, kr, cos, sin_signed, perm), (q, kv, kr, cos, sin_signed, perm)

    def bwd(res, do):
        q, kv, kr, cos, sin_signed, perm = res
        dq, dkn, dvv, dkr = bwd_call(q, kv, kr, cos, sin_signed, perm, do)
        t = q.shape[0]
        dkv = jnp.stack([dkn.reshape(t, MLA_HEADS, MLA_D_NOPE), dvv.reshape(t, MLA_HEADS, MLA_D_V)],
                        axis=2).reshape(t, -1)
        return dq, dkv, dkr, jnp.zeros_like(cos), jnp.zeros_like(sin_signed), jnp.zeros_like(perm)

    attn.defvjp(fwd, bwd)
    return attn


def _adamw_call(w, g, m, v, name):
    rows, cols = w.shape
    tr = _div(rows, max(SUBLANE, (1 << 18) // cols), SUBLANE) if rows % SUBLANE == 0 else rows

    def body(w_ref, g_ref, m_ref, v_ref, d_ref, nm_ref, nv_ref):
        gg = g_ref[...]
        nm = ADAM_B1 * m_ref[...] + (1.0 - ADAM_B1) * gg
        nv = ADAM_B2 * v_ref[...] + (1.0 - ADAM_B2) * jnp.square(gg)
        m_hat = nm / (1.0 - ADAM_B1 ** ADAM_STEP)
        v_hat = nv / (1.0 - ADAM_B2 ** ADAM_STEP)
        d_ref[...] = -ADAM_LR * (m_hat / (jnp.sqrt(v_hat) + ADAM_EPS) + ADAM_WD * w_ref[...])
        nm_ref[...] = nm
        nv_ref[...] = nv

    spec = pl.BlockSpec((tr, cols), lambda i: (i, 0))
    return pl.pallas_call(
        body, name=name, grid=(rows // tr,),
        out_shape=tuple(jax.ShapeDtypeStruct((rows, cols), F32) for _ in range(3)),
        in_specs=[spec] * 4, out_specs=(spec,) * 3,
        compiler_params=_cparams(("parallel",)),
    )(w, g, m, v)


def _adamw(w, g, m, v, name):
    shape = w.shape
    if w.ndim >= 2 and shape[-1] % LANE == 0:
        two = (-1, shape[-1])
    elif w.size % LANE == 0:
        two = (-1, LANE)
    else:
        two = (1, w.size)
    outs = _adamw_call(w.reshape(two), g.reshape(two), m.reshape(two), v.reshape(two), name)
    return tuple(o.reshape(shape) for o in outs)


def _ret_tables(nx, nctx, dk):
    inv = RET_ROPE_BASE ** (-jnp.linspace(0.0, 1.0, dk // 2, dtype=F32))
    ang = jnp.arange(nx, dtype=F32)[:, None] * inv[None, :]
    cos = jnp.concatenate([jnp.cos(ang), jnp.ones((nctx, dk // 2), F32)], axis=0)
    sin = jnp.concatenate([jnp.sin(ang), jnp.zeros((nctx, dk // 2), F32)], axis=0)
    return cos, sin


def _mla_tables(nx, nctx):
    quarter = MLA_D_ROPE // 4
    inv = AXIAL_ROPE_BASE ** (-jnp.arange(quarter, dtype=F32) * 2.0 / (MLA_D_ROPE // 2))
    tok = jnp.arange(nx)
    r_ang = (tok // GRID_W).astype(F32)[:, None] * inv[None, :]
    c_ang = (tok % GRID_W).astype(F32)[:, None] * inv[None, :]
    pad = MLA_PAD - MLA_D_ROPE
    cos = jnp.concatenate([jnp.cos(r_ang), jnp.cos(r_ang), jnp.cos(c_ang), jnp.cos(c_ang),
                           jnp.ones((nx, pad), F32)], axis=1)
    sin = jnp.concatenate([-jnp.sin(r_ang), jnp.sin(r_ang), -jnp.sin(c_ang), jnp.sin(c_ang),
                           jnp.zeros((nx, pad), F32)], axis=1)
    cos = jnp.concatenate([cos, jnp.ones((nctx, MLA_PAD), F32)], axis=0)
    sin = jnp.concatenate([sin, jnp.zeros((nctx, MLA_PAD), F32)], axis=0)
    lane = jnp.arange(MLA_PAD)
    partner = jnp.where(lane % (2 * quarter) < quarter, lane + quarter, lane - quarter)
    perm = ((lane[:, None] == partner[None, :]) & (lane[None, :] < MLA_D_ROPE)).astype(F32)
    return cos, sin, perm


def _columns(gathered):
    return gathered


def _rows(gathered):
    return gathered.reshape(1, gathered.shape[0] * gathered.shape[1], gathered.shape[2])


def _unshard_vec(gathered):
    return jnp.moveaxis(gathered, 0, 1).reshape(gathered.shape[1], -1)


def _pad_rows(a, rows):
    return jnp.concatenate([a, jnp.zeros((rows - a.shape[0],) + a.shape[1:], a.dtype)], axis=0)


def _loss_fn(weights, x, c_all, ctx, tgt, me):
    nx, d_model = x.shape
    nctx = ctx.shape[0]
    dk = d_model // RET_HEADS
    ret_cos, ret_sin = _ret_tables(nx, nctx, dk)
    mla_cos, mla_sin, mla_perm = _mla_tables(nx, nctx)

    rw = functools.partial(_make_rowwise, nx_rows=nx)

    n_ln = weights["ln_g"].size
    small = jnp.concatenate([weights["ln_g"].reshape(-1), weights["ln_b"].reshape(-1),
                             weights["mla_g_q"].reshape(-1), weights["mla_g_kv"].reshape(-1)])
    n_small = small.size
    small = _pad_rows(small.reshape(-1, 1), -(-n_small // LANE) * LANE).reshape(-1, LANE)
    small = _make_gather("small", F32)(small).reshape(N_DEV, -1)
    ln_g = _unshard_vec(small[:, :n_ln].reshape(N_DEV, DEPTH * 2, -1)).reshape(DEPTH, 2, d_model)
    ln_b = _unshard_vec(small[:, n_ln:2 * n_ln].reshape(N_DEV, DEPTH * 2, -1)).reshape(DEPTH, 2, d_model)
    n_g = weights["mla_g_q"].size
    g_q = _unshard_vec(small[:, 2 * n_ln:2 * n_ln + n_g].reshape(N_DEV, DEPTH // 2, -1))
    g_kv = _unshard_vec(small[:, 2 * n_ln + n_g:2 * n_ln + 2 * n_g].reshape(N_DEV, DEPTH // 2, -1))

    n_b = weights["ada_b"].size
    n_dec = weights["ret_decay_logit"].size
    repl = jnp.concatenate([weights["c_ctx"].reshape(-1), weights["ada_b"].reshape(-1),
                            weights["ret_decay_logit"].reshape(-1)])
    n_repl = repl.size
    repl = _pad_rows(repl.reshape(-1, 1), -(-n_repl // LANE) * LANE).reshape(-1, LANE)
    repl = _make_replicated("repl")(repl).reshape(-1)
    c_ctx = repl[:d_model]
    ada_b = repl[d_model:d_model + n_b].reshape(DEPTH, 6 * d_model)
    decay = repl[d_model + n_b:d_model + n_b + n_dec].reshape(weights["ret_decay_logit"].shape)

    cond_rows = 2 * SUBLANE
    cond = _pad_rows(jnp.concatenate([c_all, c_ctx[None, :]], axis=0), cond_rows)
    s_cond = _make_rowwise_plain_silu(cond)
    mods = [_make_mm("ada%d" % i)(s_cond, weights["ada_w"][i][None]) for i in range(DEPTH)]
    mod = jnp.concatenate(mods, axis=0)
    mod = _make_gather("mod", F32)(mod)
    mod = jnp.moveaxis(mod.reshape(N_DEV, DEPTH, cond_rows, -1), 0, 2).reshape(DEPTH, cond_rows, -1)
    mod = mod + ada_b[:, None, :]
    mine = lax.dynamic_slice_in_dim(mod, me, 1, axis=1)
    mod = jnp.concatenate([mine, mod[:, N_DEV:N_DEV + 1]], axis=1)
    mod = mod.reshape(DEPTH, 2, 6, 1, d_model)

    def grp(i, j):
        return mod[i, :, j]

    h = jnp.concatenate([x, ctx], axis=0)
    for i in range(DEPTH):
        j = i // 2
        (u,) = rw(_f_modulate, ("row", "grp", "grp"), (True, True, True), (d_model,), "mod%d" % i)(
            h, grp(i, 1), grp(i, 0))
        if i % 2 == 0:
            w_qkv = _columns(_make_gather("ret_qkv", BF16)(weights["ret_w_qkv"][j]))
            w_g = _columns(_make_gather("ret_g", BF16)(weights["ret_w_g"][j]))
            w_o = _rows(_make_gather("ret_o", BF16)(weights["ret_w_o"][j]))
            qkv = _make_mm("ret_qkv%d" % j)(u, w_qkv)
            gates = _make_mm("ret_g%d" % j)(u, w_g)
            lg = jax.nn.log_sigmoid(decay[j]).reshape(2 * RET_HEADS, 1, 1)
            lgb = jnp.broadcast_to(lg, (2 * RET_HEADS, SUBLANE, LANE))
            o = _make_retention("l%d" % j, nx)(qkv, ret_cos, ret_sin, lgb)
            (comb,) = rw(_f_ret_combine, ("row", "row", "row"), (True, True, True), (o.shape[2],),
                         "ret_comb%d" % j, tile=WIDE_ROW_TILE)(gates, o[0], o[1])
            y = _make_mm("ret_o%d" % j)(comb, w_o)
        else:
            y = _mla_mixer(weights, j, u, g_q[j], g_kv[j], mla_cos, mla_sin, mla_perm, rw, nx)
        w_in = _columns(_make_gather("ffn_in", BF16)(weights["ffn_w_in"][i]))
        w_out = _rows(_make_gather("ffn_out", BF16)(weights["ffn_w_out"][i]))
        h1, u2 = rw(_f_ln_res_mod, ("row", "row", "grp", "par", "par", "grp", "grp"), (True,) * 7,
                    (d_model, d_model), "ln_a%d" % i)(
            h, y, grp(i, 2), ln_g[i, 0][None], ln_b[i, 0][None], grp(i, 4), grp(i, 3))
        ab = _make_mm("ffn_in%d" % i)(u2, w_in)
        (act,) = rw(_f_swiglu, ("row",), (True,), (ab.shape[1] // 2,), "swiglu%d" % i, tile=WIDE_ROW_TILE)(ab)
        f = _make_mm("ffn_out%d" % i)(act, w_out)
        (h,) = rw(_f_ln_res, ("row", "row", "grp", "par", "par"), (True,) * 5, (d_model,), "ln_f%d" % i)(
            h1, f, grp(i, 5), ln_g[i, 1][None], ln_b[i, 1][None])

    (rows,) = rw(_f_loss, ("row", "row"), (True, False), (LANE,), "loss")(h[:nx], tgt)
    return jnp.sum(rows[:, 0])


def _make_rowwise_plain_silu(cond):
    def body(c_ref, o_ref):
        o_ref[...] = _silu(c_ref[...])

    def call(c):
        return pl.pallas_call(body, name="silu_cond", out_shape=jax.ShapeDtypeStruct(c.shape, F32))(c)

    def bwd_body(c_ref, g_ref, o_ref):
        _, vjp = jax.vjp(_silu, c_ref[...])
        o_ref[...] = vjp(g_ref[...])[0]

    @jax.custom_vjp
    def op(c):
        return call(c)

    def fwd(c):
        return call(c), c

    def bwd(c, g):
        return (pl.pallas_call(bwd_body, name="silu_cond_bwd", out_shape=jax.ShapeDtypeStruct(c.shape, F32))(c, g),)

    op.defvjp(fwd, bwd)
    return op(cond)


def _mla_mixer(weights, j, u, g_q, g_kv, cos, sin_signed, perm, rw, nx):
    heads, dn, dr, dv = MLA_HEADS, MLA_D_NOPE, MLA_D_ROPE, MLA_D_V
    w_dq = _rows(_make_gather("mla_dq", BF16)(weights["mla_w_dq"][j]))
    w_uq = _make_gather("mla_uq", BF16)(weights["mla_w_uq"][j])
    w_dkv = _rows(_make_gather("mla_dkv", BF16)(weights["mla_w_dkv"][j]))
    w_ukv = _columns(_make_gather("mla_ukv", BF16)(weights["mla_w_ukv"][j]))
    w_o = _rows(_make_gather("mla_o", BF16)(weights["mla_w_o"][j]))

    q_lora = w_uq.shape[1]
    w_uq = jnp.moveaxis(w_uq, 0, 1).reshape(q_lora, heads, dn + dr)
    w_uq = jnp.concatenate([w_uq, jnp.zeros((q_lora, heads, MLA_PAD - dr), w_uq.dtype)], axis=2)
    w_uq = w_uq.reshape(1, q_lora, heads * (dn + MLA_PAD))
    w_dkv = jnp.concatenate([w_dkv, jnp.zeros(w_dkv.shape[:2] + (MLA_PAD - dr,), w_dkv.dtype)], axis=2)

    cq = _make_mm("mla_dq%d" % j)(u, w_dq)
    (cqn,) = rw(_f_rms, ("row", "par"), (True, True), (cq.shape[1],), "rms_q%d" % j)(cq, g_q[None])
    q = _make_mm("mla_uq%d" % j)(cqn, w_uq)
    ckv = _make_mm("mla_dkv%d" % j)(u, w_dkv)
    lora = ckv.shape[1] - MLA_PAD
    c_kv, kr = rw(_f_kv_latent, ("row", "par", "row", "row", "par"), (True, True, False, False, False),
                  (lora, MLA_PAD), "kv_lat%d" % j)(ckv, g_kv[None], cos, sin_signed, perm)
    kv = _make_mm("mla_ukv%d" % j)(c_kv, w_ukv)
    o = _make_attention("l%d" % j, nx)(q, kv, kr, cos, sin_signed, perm)
    return _make_mm("mla_o%d" % j)(o, w_o)


WEIGHT_NAMES = ("c_ctx", "ada_w", "ada_b", "ln_g", "ln_b", "ret_w_qkv", "ret_w_g", "ret_decay_logit", "ret_w_o",
                "mla_w_dq", "mla_g_q", "mla_w_uq", "mla_w_dkv", "mla_g_kv", "mla_w_ukv", "mla_w_o",
                "ffn_w_in", "ffn_w_out")


def kernel(x, c, ctx, c_ctx, ada_w, ada_b, ln_g, ln_b, ret_w_qkv, ret_w_g, ret_decay_logit, ret_w_o, mla_w_dq, mla_g_q, mla_w_uq, mla_w_dkv, mla_g_kv, mla_w_ukv, mla_w_o, ffn_w_in, ffn_w_out, loss_target, m_c_ctx, m_ada_w, m_ada_b, m_ln_g, m_ln_b, m_ret_w_qkv, m_ret_w_g, m_ret_decay_logit, m_ret_w_o, m_mla_w_dq, m_mla_g_q, m_mla_w_uq, m_mla_w_dkv, m_mla_g_kv, m_mla_w_ukv, m_mla_w_o, m_ffn_w_in, m_ffn_w_out, v_c_ctx, v_ada_w, v_ada_b, v_ln_g, v_ln_b, v_ret_w_qkv, v_ret_w_g, v_ret_decay_logit, v_ret_w_o, v_mla_w_dq, v_mla_g_q, v_mla_w_uq, v_mla_w_dkv, v_mla_g_kv, v_mla_w_ukv, v_mla_w_o, v_ffn_w_in, v_ffn_w_out):
    given = dict(locals())
    weights = {n: given[n] for n in WEIGHT_NAMES}
    me = _my_index()
    c_all = _all_gather_call(c, "ag_cond").reshape(N_DEV, -1)

    def loss_of(wts, x2):
        return _loss_fn(wts, x2, c_all, ctx[0], loss_target[0], me)

    loss, (grad_w, grad_x) = jax.value_and_grad(loss_of, argnums=(0, 1))(weights, x[0])
    loss = lax.psum(loss, AXES)
    delta, new_m, new_v = {}, {}, {}
    for n in WEIGHT_NAMES:
        delta[n], new_m[n], new_v[n] = _adamw(weights[n], grad_w[n], given["m_" + n], given["v_" + n], "adamw_" + n)
    return (loss, grad_x[None], *[grad_w[n] for n in WEIGHT_NAMES], *[delta[n] for n in WEIGHT_NAMES],
            *[new_m[n] for n in WEIGHT_NAMES], *[new_v[n] for n in WEIGHT_NAMES])
```

```python
import functools
import math

import jax
import jax.numpy as jnp
from jax import lax
from jax.experimental import pallas as pl
from jax.experimental.pallas import tpu as pltpu

F32 = jnp.float32
BF16 = jnp.bfloat16

AXES = ("x", "y", "c")
N_DEV = 8
MESH_IDS = pl.DeviceIdType.MESH

DEPTH = 4
GRID_W = 64
RET_HEADS = 8
RET_CHUNK = 128
RET_ROPE_BASE = 10000.0
GN_EPS = 1e-6
MLA_HEADS = 16
MLA_D_NOPE = 128
MLA_D_ROPE = 64
MLA_D_V = 128
AXIAL_ROPE_BASE = 10000.0
RMS_EPS = 1e-6
LN_EPS = 1e-5
DEEPNORM_ALPHA = (2 * DEPTH) ** 0.25

ADAM_LR = 0.001
ADAM_B1 = 0.9
ADAM_B2 = 0.999
ADAM_EPS = 1e-08
ADAM_WD = 0.01
ADAM_STEP = 10

LANE = 128
SUBLANE = 8
VMEM_LIMIT = 56 * 1024 * 1024

ROW_TILE = 256
WIDE_ROW_TILE = 64
MM_ROWS = 544
MLA_PAD = 128


def _div(n, cap, mult):
    best = None
    for d in range(mult, min(n, cap) + 1, mult):
        if n % d == 0:
            best = d
    return n if best is None else best


def _cparams(sem=None):
    kw = dict(vmem_limit_bytes=VMEM_LIMIT)
    if sem is not None:
        kw["dimension_semantics"] = sem
    return pltpu.CompilerParams(**kw)


_DN = {"nn": (((1,), (0,)), ((), ())), "nt": (((1,), (1,)), ((), ())), "tn": (((0,), (0,)), ((), ()))}


def _raw_dot(a, b, kind):
    return lax.dot_general(a.astype(BF16), b.astype(BF16), _DN[kind], preferred_element_type=F32)


@functools.partial(jax.custom_vjp, nondiff_argnums=(2,))
def _dot(a, b, kind):
    return _raw_dot(a, b, kind)


def _dot_fwd(a, b, kind):
    return _raw_dot(a, b, kind), (a, b)


def _dot_bwd(kind, res, g):
    a, b = res
    if kind == "nn":
        return _raw_dot(g, b, "nt"), _raw_dot(a, g, "tn")
    if kind == "nt":
        return _raw_dot(g, b, "nn"), _raw_dot(g, a, "tn")
    return _raw_dot(b, g, "nt"), _raw_dot(a, g, "nn")


_dot.defvjp(_dot_fwd, _dot_bwd)


def _sigmoid(x):
    return 1.0 / (1.0 + jnp.exp(-x))


def _silu(x):
    return x * _sigmoid(x)


def _my_index():
    return 4 * lax.axis_index("x") + 2 * lax.axis_index("y") + lax.axis_index("c")


def _all_gather_call(x, name):
    def body(x_ref, out_ref, send_sems, recv_sems, local_sem):
        x_, y_, c_ = lax.axis_index("x"), lax.axis_index("y"), lax.axis_index("c")
        me, sibling = (x_, y_, c_), (x_, y_, 1 - c_)
        chips = [(1 - x_, y_), (x_, 1 - y_), (1 - x_, 1 - y_)]

        def slot(px, py, pc):
            return out_ref.at[4 * px + 2 * py + pc]

        def copy(k, block, to, src=None):
            return pltpu.make_async_remote_copy(
                src_ref=slot(*block) if src is None else src, dst_ref=slot(*block),
                send_sem=send_sems.at[k], recv_sem=recv_sems.at[k],
                device_id=to, device_id_type=MESH_IDS)

        mine = pltpu.make_async_copy(x_ref, slot(*me), local_sem)
        mine.start()
        first = [copy(0, me, sibling, src=x_ref)]
        first += [copy(1 + j, me, (*chip, c_), src=x_ref) for j, chip in enumerate(chips)]
        for cp in first:
            cp.start()
        passed = [copy(4 + j, (*chip, c_), sibling) for j, chip in enumerate(chips)]
        for j, chip in enumerate(chips):
            copy(1 + j, (*chip, c_), me).wait_recv()
            passed[j].start()
        copy(0, sibling, me).wait_recv()
        for j, chip in enumerate(chips):
            copy(4 + j, (*chip, 1 - c_), me).wait_recv()
        for cp in first + passed:
            cp.wait_send()
        mine.wait()

    return pl.pallas_call(
        body, name=name,
        out_shape=jax.ShapeDtypeStruct((N_DEV,) + x.shape, x.dtype),
        in_specs=[pl.BlockSpec(memory_space=pl.ANY)],
        out_specs=pl.BlockSpec(memory_space=pl.ANY),
        scratch_shapes=[pltpu.SemaphoreType.DMA((7,)), pltpu.SemaphoreType.DMA((7,)), pltpu.SemaphoreType.DMA],
    )(x)


def _all_to_all_call(g, name):
    def body(g_ref, out_ref, send_sems, recv_sems, local_sem):
        x_, y_, c_ = lax.axis_index("x"), lax.axis_index("y"), lax.axis_index("c")
        my = 4 * x_ + 2 * y_ + c_
        peers = []
        for k in range(1, N_DEV):
            px = 1 - x_ if (k >> 2) & 1 else x_
            py = 1 - y_ if (k >> 1) & 1 else y_
            pc = 1 - c_ if k & 1 else c_
            peers.append((px, py, pc))

        def copy(k, peer):
            pid = 4 * peer[0] + 2 * peer[1] + peer[2]
            return pltpu.make_async_remote_copy(
                src_ref=g_ref.at[pid], dst_ref=out_ref.at[my],
                send_sem=send_sems.at[k], recv_sem=recv_sems.at[k],
                device_id=peer, device_id_type=MESH_IDS)

        def arrival(k, peer):
            pid = 4 * peer[0] + 2 * peer[1] + peer[2]
            return pltpu.make_async_remote_copy(
                src_ref=g_ref.at[pid], dst_ref=out_ref.at[pid],
                send_sem=send_sems.at[k], recv_sem=recv_sems.at[k],
                device_id=peer, device_id_type=MESH_IDS)

        mine = pltpu.make_async_copy(g_ref.at[my], out_ref.at[my], local_sem)
        mine.start()
        sends = [copy(k, peer) for k, peer in enumerate(peers)]
        for cp in sends:
            cp.start()
        for k, peer in enumerate(peers):
            arrival(k, peer).wait_recv()
        for cp in sends:
            cp.wait_send()
        mine.wait()

    return pl.pallas_call(
        body, name=name,
        out_shape=jax.ShapeDtypeStruct(g.shape, g.dtype),
        in_specs=[pl.BlockSpec(memory_space=pl.ANY)],
        out_specs=pl.BlockSpec(memory_space=pl.ANY),
        scratch_shapes=[pltpu.SemaphoreType.DMA((7,)), pltpu.SemaphoreType.DMA((7,)), pltpu.SemaphoreType.DMA],
    )(g)


def _sum_slots_call(g, name):
    _, rows, cols = g.shape
    tr = _div(rows, max(SUBLANE, (1 << 19) // cols), 16) if rows % 16 == 0 else rows

    def body(g_ref, o_ref):
        acc = g_ref[0].astype(F32)
        for s in range(1, N_DEV):
            acc = acc + g_ref[s].astype(F32)
        o_ref[...] = acc

    return pl.pallas_call(
        body, name=name, grid=(rows // tr,),
        out_shape=jax.ShapeDtypeStruct((rows, cols), F32),
        in_specs=[pl.BlockSpec((N_DEV, tr, cols), lambda i: (0, i, 0))],
        out_specs=pl.BlockSpec((tr, cols), lambda i: (i, 0)),
        compiler_params=_cparams(("parallel",)),
    )(g)


def _make_gather(tag, travel_dtype):
    @jax.custom_vjp
    def gather(w):
        return _all_gather_call(w.astype(travel_dtype), "ag_" + tag)

    def fwd(w):
        return gather(w), None

    def bwd(_, g):
        got = _all_to_all_call(g.astype(travel_dtype), "a2a_" + tag)
        return (_sum_slots_call(got, "sum_" + tag),)

    gather.defvjp(fwd, bwd)
    return gather


def _make_replicated(tag):
    @jax.custom_vjp
    def rep(p):
        return p

    def fwd(p):
        return p, None

    def bwd(_, g):
        return (_sum_slots_call(_all_gather_call(g, "ag_" + tag), "sum_" + tag),)

    rep.defvjp(fwd, bwd)
    return rep


class _Riders:
    def __init__(self, gathers=(), exchanges=()):
        self.gathers = list(gathers)
        self.exchanges = list(exchanges)
        self.n = len(self.gathers) + len(self.exchanges)

    def operands(self):
        return self.gathers + self.exchanges

    def out_shape(self):
        return ([jax.ShapeDtypeStruct((N_DEV,) + g.shape, g.dtype) for g in self.gathers]
                + [jax.ShapeDtypeStruct(e.shape, e.dtype) for e in self.exchanges])

    def scratch(self):
        return [pltpu.SemaphoreType.DMA((7 * self.n,)), pltpu.SemaphoreType.DMA((7 * self.n,)),
                pltpu.SemaphoreType.DMA((self.n,))]

    def _gather_phase(self, phase, r, x_ref, out_ref, send_sems, recv_sems, local_sem):
        x_, y_, c_ = lax.axis_index("x"), lax.axis_index("y"), lax.axis_index("c")
        me, sibling = (x_, y_, c_), (x_, y_, 1 - c_)
        chips = [(1 - x_, y_), (x_, 1 - y_), (1 - x_, 1 - y_)]

        def slot(px, py, pc):
            return out_ref.at[4 * px + 2 * py + pc]

        def copy(k, block, to, src=None):
            return pltpu.make_async_remote_copy(
                src_ref=slot(*block) if src is None else src, dst_ref=slot(*block),
                send_sem=send_sems.at[7 * r + k], recv_sem=recv_sems.at[7 * r + k],
                device_id=to, device_id_type=MESH_IDS)

        def mine():
            return pltpu.make_async_copy(x_ref, slot(*me), local_sem.at[r])

        def first():
            return [copy(0, me, sibling, src=x_ref)] + [copy(1 + j, me, (*chip, c_), src=x_ref)
                                                        for j, chip in enumerate(chips)]

        def passed(j):
            return copy(4 + j, (*chips[j], c_), sibling)

        if phase == "start":
            mine().start()
            for cp in first():
                cp.start()
        elif phase == "pass":
            for j, chip in enumerate(chips):
                copy(1 + j, (*chip, c_), me).wait_recv()
                passed(j).start()
        else:
            copy(0, sibling, me).wait_recv()
            for j, chip in enumerate(chips):
                copy(4 + j, (*chip, 1 - c_), me).wait_recv()
            for cp in first() + [passed(j) for j in range(3)]:
                cp.wait_send()
            mine().wait()

    def _exchange_phase(self, phase, r, g_ref, out_ref, send_sems, recv_sems, local_sem):
        x_, y_, c_ = lax.axis_index("x"), lax.axis_index("y"), lax.axis_index("c")
        my = 4 * x_ + 2 * y_ + c_
        peers = []
        for k in range(1, N_DEV):
            peers.append((1 - x_ if (k >> 2) & 1 else x_, 1 - y_ if (k >> 1) & 1 else y_, 1 - c_ if k & 1 else c_))

        def copy(k, peer, arriving):
            pid = 4 * peer[0] + 2 * peer[1] + peer[2]
            return pltpu.make_async_remote_copy(
                src_ref=g_ref.at[pid], dst_ref=out_ref.at[pid if arriving else my],
                send_sem=send_sems.at[7 * r + k], recv_sem=recv_sems.at[7 * r + k],
                device_id=peer, device_id_type=MESH_IDS)

        def mine():
            return pltpu.make_async_copy(g_ref.at[my], out_ref.at[my], local_sem.at[r])

        if phase == "start":
            mine().start()
            for k, peer in enumerate(peers):
                copy(k, peer, False).start()
        elif phase == "finish":
            for k, peer in enumerate(peers):
                copy(k, peer, True).wait_recv()
            for k, peer in enumerate(peers):
                copy(k, peer, False).wait_send()
            mine().wait()

    def phase(self, phase, in_refs, out_refs, send_sems, recv_sems, local_sem):
        ng = len(self.gathers)
        for r in range(self.n):
            if r < ng:
                self._gather_phase(phase, r, in_refs[r], out_refs[r], send_sems, recv_sems, local_sem)
            else:
                self._exchange_phase(phase, r, in_refs[r], out_refs[r], send_sems, recv_sems, local_sem)


def _call(body, *, name, grid, in_specs, out_specs, out_shape, operands, scratch_shapes=(), semantics, riders=None):
    if riders is None or riders.n == 0:
        return tuple(pl.pallas_call(
            body, name=name, grid=grid, out_shape=tuple(out_shape), in_specs=list(in_specs),
            out_specs=tuple(out_specs), scratch_shapes=list(scratch_shapes), compiler_params=_cparams(semantics),
        )(*operands))

    n_in, n_out, n_scr, rn = len(in_specs), len(out_specs), len(scratch_shapes), riders.n
    steps = math.prod(grid)

    def full_body(*refs):
        ins, r_in = refs[:n_in], refs[n_in:n_in + rn]
        outs, r_out = refs[n_in + rn:n_in + rn + n_out], refs[n_in + rn + n_out:n_in + 2 * rn + n_out]
        scr = refs[n_in + 2 * rn + n_out:n_in + 2 * rn + n_out + n_scr]
        sems = refs[n_in + 2 * rn + n_out + n_scr:]
        step = 0
        for axis, size in enumerate(grid):
            step = step * size + pl.program_id(axis)

        @pl.when(step == 0)
        def _():
            riders.phase("start", r_in, r_out, *sems)

        @pl.when(step == steps // 2)
        def _():
            riders.phase("pass", r_in, r_out, *sems)

        body(*ins, *outs, *scr)

        @pl.when(step == steps - 1)
        def _():
            riders.phase("finish", r_in, r_out, *sems)

    hbm = pl.BlockSpec(memory_space=pl.ANY)
    return tuple(pl.pallas_call(
        full_body, name=name, grid=grid,
        out_shape=tuple(out_shape) + tuple(riders.out_shape()),
        in_specs=list(in_specs) + [hbm] * rn,
        out_specs=tuple(out_specs) + (hbm,) * rn,
        scratch_shapes=list(scratch_shapes) + riders.scratch(),
        compiler_params=_cparams(("arbitrary",) * len(grid)),
    )(*operands, *riders.operands()))


def _mm_nn_call(a, w3, name, riders=None):
    m, k = a.shape
    nb, _, ns = w3.shape
    tm = _div(m, MM_ROWS, SUBLANE)
    tn = _div(ns, max(LANE, (6 << 20) // (2 * k)), LANE)
    tps = ns // tn

    def body(a_ref, w_ref, o_ref):
        o_ref[...] = _raw_dot(a_ref[...], w_ref[...], "nn")

    return _call(
        body, name=name, grid=(nb * tps, m // tm),
        out_shape=[jax.ShapeDtypeStruct((m, nb * ns), F32)],
        in_specs=[pl.BlockSpec((tm, k), lambda j, i: (i, 0)),
                  pl.BlockSpec((None, k, tn), lambda j, i: (j // tps, 0, j % tps))],
        out_specs=[pl.BlockSpec((tm, tn), lambda j, i: (i, j))],
        operands=(a, w3), semantics=("parallel", "parallel"), riders=riders)


def _mm_nt_call(g, w3, name, riders=None):
    m, n = g.shape
    nb, k, ns = w3.shape
    tm = _div(m, MM_ROWS, SUBLANE)
    tk = _div(k, 2048, LANE)
    tn = _div(ns, max(LANE, (6 << 20) // (2 * tk)), LANE)
    tps = ns // tn
    nj = nb * tps

    def body(g_ref, w_ref, o_ref, acc):
        j = pl.program_id(2)
        p = _raw_dot(g_ref[...], w_ref[...], "nt")

        @pl.when(j == 0)
        def _():
            acc[...] = p

        @pl.when(j > 0)
        def _():
            acc[...] += p

        @pl.when(j == nj - 1)
        def _():
            o_ref[...] = acc[...]

    return _call(
        body, name=name, grid=(m // tm, k // tk, nj),
        out_shape=[jax.ShapeDtypeStruct((m, k), F32)],
        in_specs=[pl.BlockSpec((tm, tn), lambda i, kk, j: (i, j)),
                  pl.BlockSpec((None, tk, tn), lambda i, kk, j: (j // tps, kk, j % tps))],
        out_specs=[pl.BlockSpec((tm, tk), lambda i, kk, j: (i, kk))],
        scratch_shapes=[pltpu.VMEM((tm, tk), F32)],
        operands=(g, w3), semantics=("parallel", "parallel", "arbitrary"), riders=riders)


def _mm_tn_call(a, g, nb, out_dtype, name, riders=None):
    m, k = a.shape
    n = g.shape[1]
    ns = n // nb
    tm = _div(m, MM_ROWS, SUBLANE)
    tk = _div(k, 2048, LANE)
    tn = _div(ns, max(LANE, (12 << 20) // (4 * tk)), LANE)
    tps = ns // tn
    nm = m // tm

    def body(a_ref, g_ref, o_ref, acc):
        i = pl.program_id(2)
        p = _raw_dot(a_ref[...], g_ref[...], "tn")

        @pl.when(i == 0)
        def _():
            acc[...] = p

        @pl.when(i > 0)
        def _():
            acc[...] += p

        @pl.when(i == nm - 1)
        def _():
            o_ref[...] = acc[...].astype(o_ref.dtype)

    return _call(
        body, name=name, grid=(nb * tps, k // tk, nm),
        out_shape=[jax.ShapeDtypeStruct((nb, k, ns), out_dtype)],
        in_specs=[pl.BlockSpec((tm, tk), lambda j, kk, i: (i, kk)),
                  pl.BlockSpec((tm, tn), lambda j, kk, i: (i, j))],
        out_specs=[pl.BlockSpec((None, tk, tn), lambda j, kk, i: (j // tps, kk, j % tps))],
        scratch_shapes=[pltpu.VMEM((tk, tn), F32)],
        operands=(a, g), semantics=("parallel", "parallel", "arbitrary"), riders=riders)


def _sum_exchanged(exchanged, tag):
    return [_sum_slots_call(e, "sum_%s_%d" % (tag, k)) for k, e in enumerate(exchanged)]


def _make_mm(tag, n_ride=0):
    def impl(a, w3, *shards):
        return _mm_nn_call(a, w3, "mm_" + tag, _Riders(gathers=[s.astype(BF16) for s in shards]))

    @jax.custom_vjp
    def mm(a, w3, *shards):
        return impl(a, w3, *shards)

    def fwd(a, w3, *shards):
        return impl(a, w3, *shards), (a, w3)

    def bwd(res, cts):
        a, w3 = res
        g, d_gathered = cts[0], list(cts[1:])
        order = sorted(range(n_ride), key=lambda r: -d_gathered[r].size)
        with_dw, with_da = order[0::2], order[1::2]
        da, *ex_da = _mm_nt_call(g, w3, "mm_da_" + tag, _Riders(exchanges=[d_gathered[r] for r in with_da]))
        dw, *ex_dw = _mm_tn_call(a, g, w3.shape[0], w3.dtype, "mm_dw_" + tag,
                                 _Riders(exchanges=[d_gathered[r] for r in with_dw]))
        d_shards = [None] * n_ride
        for r, s in zip(with_da, _sum_exchanged(ex_da, tag + "_a")):
            d_shards[r] = s
        for r, s in zip(with_dw, _sum_exchanged(ex_dw, tag + "_w")):
            d_shards[r] = s
        return (da, dw, *d_shards)

    mm.defvjp(fwd, bwd)
    return mm


def _make_rowwise(f, kinds, diff, out_cols, tag, nx_rows, tile=ROW_TILE):
    n_in = len(kinds)
    nxt = nx_rows // tile

    def in_spec(kind, arr):
        if kind == "row":
            return pl.BlockSpec((tile, arr.shape[1]), lambda i: (i, 0))
        if kind == "grp":
            return pl.BlockSpec((None, 1, arr.shape[2]), lambda i: (i // nxt, 0, 0))
        return pl.BlockSpec(arr.shape, lambda i: (0,) * arr.ndim)

    def fwd_call(*args):
        t = next(a.shape[0] for a, kd in zip(args, kinds) if kd == "row")

        def body(*refs):
            outs = f(*[r[...] for r in refs[:n_in]])
            for r, o in zip(refs[n_in:], outs):
                r[...] = o

        return pl.pallas_call(
            body, name="rw_" + tag, grid=(t // tile,),
            out_shape=tuple(jax.ShapeDtypeStruct((t, c), F32) for c in out_cols),
            in_specs=[in_spec(kd, a) for kd, a in zip(kinds, args)],
            out_specs=tuple(pl.BlockSpec((tile, c), lambda i: (i, 0)) for c in out_cols),
            compiler_params=_cparams(("parallel",)),
        )(*args)

    def bwd_call(args, cts):
        t = cts[0].shape[0]
        didx = [i for i in range(n_in) if diff[i]]

        def body(*refs):
            i = pl.program_id(0)
            vals = [r[...] for r in refs[:n_in]]
            ct = tuple(r[...] for r in refs[n_in:n_in + len(out_cols)])
            outs = refs[n_in + len(out_cols):]

            def g(*dv):
                full = list(vals)
                for j, v in zip(didx, dv):
                    full[j] = v
                return tuple(f(*full))

            _, vjp = jax.vjp(g, *[vals[j] for j in didx])
            grads = vjp(ct)
            for j, o_ref, d in zip(didx, outs, grads):
                if kinds[j] == "row":
                    o_ref[...] = d
                else:
                    first = (i % nxt == 0) if kinds[j] == "grp" else (i == 0)

                    @pl.when(first)
                    def _(o_ref=o_ref, d=d):
                        o_ref[...] = d

                    @pl.when(jnp.logical_not(first))
                    def _(o_ref=o_ref, d=d):
                        o_ref[...] += d

        def out_spec(j):
            a = args[j]
            if kinds[j] == "row":
                return pl.BlockSpec((tile, a.shape[1]), lambda i: (i, 0))
            if kinds[j] == "grp":
                return pl.BlockSpec((None, 1, a.shape[2]), lambda i: (i // nxt, 0, 0))
            return pl.BlockSpec(a.shape, lambda i: (0,) * a.ndim)

        return pl.pallas_call(
            body, name="rw_bwd_" + tag, grid=(t // tile,),
            out_shape=tuple(jax.ShapeDtypeStruct(args[j].shape, F32) for j in didx),
            in_specs=[in_spec(kd, a) for kd, a in zip(kinds, args)]
            + [pl.BlockSpec((tile, c), lambda i: (i, 0)) for c in out_cols],
            out_specs=tuple(out_spec(j) for j in didx),
            compiler_params=_cparams(("arbitrary",)),
        )(*args, *cts)

    @jax.custom_vjp
    def op(*args):
        return fwd_call(*args)

    def fwd(*args):
        return fwd_call(*args), args

    def bwd(args, cts):
        grads = bwd_call(args, cts)
        full = [None] * n_in
        for j, gr in zip([i for i in range(n_in) if diff[i]], grads):
            full[j] = gr
        return tuple(jnp.zeros_like(a) if gfull is None else gfull for a, gfull in zip(args, full))

    op.defvjp(fwd, bwd)
    return op


def _layer_norm(z, g, b):
    mu = jnp.mean(z, axis=-1, keepdims=True)
    var = jnp.mean(jnp.square(z - mu), axis=-1, keepdims=True)
    return (z - mu) * lax.rsqrt(var + LN_EPS) * g + b


def _f_modulate(h, sc, sh):
    return (h * (1.0 + sc) + sh,)


def _f_ln_res(h, y, gate, g, b):
    return (_layer_norm(DEEPNORM_ALPHA * h + gate * y, g, b),)


def _f_ln_res_mod(h, y, gate, g, b, sc, sh):
    h1 = _layer_norm(DEEPNORM_ALPHA * h + gate * y, g, b)
    return h1, h1 * (1.0 + sc) + sh


def _f_swiglu(ab):
    half = ab.shape[1] // 2
    return (_silu(ab[:, :half]) * ab[:, half:],)


def _f_ret_combine(gates, o_f, o_b):
    hv = o_f.shape[1]
    dv = hv // RET_HEADS

    def gn(o):
        parts = []
        for h in range(RET_HEADS):
            oh = o[:, h * dv:(h + 1) * dv]
            mu = jnp.mean(oh, axis=-1, keepdims=True)
            var = jnp.mean(jnp.square(oh - mu), axis=-1, keepdims=True)
            parts.append((oh - mu) * lax.rsqrt(var + GN_EPS))
        return jnp.concatenate(parts, axis=1)

    return (_silu(gates[:, :hv]) * gn(o_f) + _silu(gates[:, hv:]) * gn(o_b),)


def _rms(x, g):
    return x * lax.rsqrt(jnp.mean(jnp.square(x), axis=-1, keepdims=True) + RMS_EPS) * g


def _f_rms(x, g):
    return (_rms(x, g),)


def _rot(x, cos, sin_signed, perm):
    swapped = jnp.dot(x, perm, precision=lax.Precision.HIGHEST, preferred_element_type=F32)
    return x * cos + swapped * sin_signed


def _f_kv_latent(ckv, g, cos, sin_signed, perm):
    lora = ckv.shape[1] - MLA_PAD
    return _rms(ckv[:, :lora], g), _rot(ckv[:, lora:], cos, sin_signed, perm)


def _f_loss(y, tgt):
    row = 0.5 * jnp.mean(jnp.square(y - tgt), axis=-1, keepdims=True)
    return (jnp.broadcast_to(row, (y.shape[0], LANE)),)


def _ret_step(q, k, v, state, lg, cos, sin, d):
    c, dk = q.shape
    half = dk // 2

    def rope(t):
        t1, t2 = t[:, :half], t[:, half:]
        return jnp.concatenate([t1 * cos - t2 * sin, t1 * sin + t2 * cos], axis=1)

    q = rope(q)
    k = rope(k * (dk ** -0.5))
    sgn = (1 - 2 * d).astype(F32)
    ii = lax.broadcasted_iota(jnp.int32, (c, c), 0).astype(F32)
    jj = lax.broadcasted_iota(jnp.int32, (c, c), 1).astype(F32)
    e = (ii - jj) * sgn
    intra = jnp.where(e >= 0, jnp.exp(lg * jnp.maximum(e, 0.0)), 0.0)
    idx = lax.broadcasted_iota(jnp.int32, (c, 1), 0).astype(F32)
    pos = jnp.where(d == 0, idx, c - 1.0 - idx)
    q_dec = jnp.exp(lg * (pos + 1.0))
    k_dec = jnp.exp(lg * (c - 1.0 - pos))
    c_dec = jnp.exp(lg * float(c))
    scores = _dot(q, k, "nt") * intra
    o = _dot(scores, v, "nn") + _dot(q * q_dec, state, "nn")
    new_state = state * c_dec + _dot(k * k_dec, v, "tn")
    return o, new_state


def _ret_chunk_of(d, p, nxc, nc):
    return (1 - d) * ((p + nxc) % nc) + d * (nc - 1 - p)


def _ret_specs(t, dk, dv, nxc, nc, step_of):
    hq = RET_HEADS * dk // dk
    c = RET_CHUNK

    def chunk(d, h, s):
        return _ret_chunk_of(d, step_of(s), nxc, nc)

    q_spec = pl.BlockSpec((c, dk), lambda d, h, s: (chunk(d, h, s), h))
    k_spec = pl.BlockSpec((c, dk), lambda d, h, s: (chunk(d, h, s), hq + h))
    v_spec = pl.BlockSpec((c, dv), lambda d, h, s: (chunk(d, h, s), (2 * RET_HEADS * dk) // dv + h))
    tab_spec = pl.BlockSpec((c, dk // 2), lambda d, h, s: (chunk(d, h, s), 0))
    lg_spec = pl.BlockSpec((None, SUBLANE, LANE), lambda d, h, s: (d * RET_HEADS + h, 0, 0))
    o_spec = pl.BlockSpec((None, c, dv), lambda d, h, s: (d, chunk(d, h, s), h))
    st_spec = pl.BlockSpec((None, None, None, dk, dv), lambda d, h, s: (d, h, step_of(s), 0, 0))
    return q_spec, k_spec, v_spec, tab_spec, lg_spec, o_spec, st_spec


def _make_retention(tag, nx_rows):
    def dims(qkv):
        t, w = qkv.shape
        dk = w // (4 * RET_HEADS)
        return t, dk, 2 * dk, nx_rows // RET_CHUNK, t // RET_CHUNK

    def fwd_call(qkv, cos, sin, lgb, riders):
        t, dk, dv, nxc, nc = dims(qkv)
        q_spec, k_spec, v_spec, tab_spec, lg_spec, o_spec, st_spec = _ret_specs(t, dk, dv, nxc, nc, lambda s: s)

        def body(q_ref, k_ref, v_ref, cos_ref, sin_ref, lg_ref, o_ref, st_ref, state):
            d = pl.program_id(0)

            @pl.when(pl.program_id(2) == 0)
            def _():
                state[...] = jnp.zeros_like(state)

            st_ref[...] = state[...]
            o, new_state = _ret_step(q_ref[...], k_ref[...], v_ref[...], state[...], lg_ref[0:1, 0:1],
                                     cos_ref[...], sin_ref[...], d)
            o_ref[...] = o
            state[...] = new_state

        return _call(
            body, name="ret_" + tag, grid=(2, RET_HEADS, nc),
            out_shape=(jax.ShapeDtypeStruct((2, t, RET_HEADS * dv), F32),
                       jax.ShapeDtypeStruct((2, RET_HEADS, nc, dk, dv), F32)),
            in_specs=[q_spec, k_spec, v_spec, tab_spec, tab_spec, lg_spec],
            out_specs=(o_spec, st_spec),
            scratch_shapes=[pltpu.VMEM((dk, dv), F32)],
            operands=(qkv, qkv, qkv, cos, sin, lgb),
            semantics=("parallel", "parallel", "arbitrary"), riders=riders)

    def bwd_call(qkv, cos, sin, lgb, states, do, riders):
        t, dk, dv, nxc, nc = dims(qkv)
        q_spec, k_spec, v_spec, tab_spec, lg_spec, o_spec, st_spec = _ret_specs(
            t, dk, dv, nxc, nc, lambda s: nc - 1 - s)
        c = RET_CHUNK

        def chunk(d, h, s):
            return _ret_chunk_of(d, nc - 1 - s, nxc, nc)

        dq_spec = pl.BlockSpec((None, c, dk), lambda d, h, s: (d, chunk(d, h, s), h))
        dv_spec = pl.BlockSpec((None, c, dv), lambda d, h, s: (d, chunk(d, h, s), h))

        def body(q_ref, k_ref, v_ref, cos_ref, sin_ref, lg_ref, st_ref, do_ref,
                 dq_ref, dk_ref, dv_ref, dlg_ref, dstate):
            d = pl.program_id(0)
            s = pl.program_id(2)

            @pl.when(s == 0)
            def _():
                dstate[...] = jnp.zeros_like(dstate)

            cos_t, sin_t = cos_ref[...], sin_ref[...]

            def step(q, k, v, state, lg):
                return _ret_step(q, k, v, state, lg, cos_t, sin_t, d)

            _, vjp = jax.vjp(step, q_ref[...], k_ref[...], v_ref[...], st_ref[...], lg_ref[0:1, 0:1])
            dq, dkk, dvv, dst, dlg = vjp((do_ref[...], dstate[...]))
            dq_ref[...] = dq
            dk_ref[...] = dkk
            dv_ref[...] = dvv
            dstate[...] = dst
            corner = jnp.logical_and(lax.broadcasted_iota(jnp.int32, (SUBLANE, LANE), 0) == 0,
                                     lax.broadcasted_iota(jnp.int32, (SUBLANE, LANE), 1) == 0)
            dlg_full = jnp.where(corner, dlg, 0.0)

            @pl.when(s == 0)
            def _():
                dlg_ref[...] = dlg_full

            @pl.when(s > 0)
            def _():
                dlg_ref[...] += dlg_full

        return _call(
            body, name="ret_bwd_" + tag, grid=(2, RET_HEADS, nc),
            out_shape=(jax.ShapeDtypeStruct((2, t, RET_HEADS * dk), F32),
                       jax.ShapeDtypeStruct((2, t, RET_HEADS * dk), F32),
                       jax.ShapeDtypeStruct((2, t, RET_HEADS * dv), F32),
                       jax.ShapeDtypeStruct(lgb.shape, F32)),
            in_specs=[q_spec, k_spec, v_spec, tab_spec, tab_spec, lg_spec, st_spec, o_spec],
            out_specs=(dq_spec, dq_spec, dv_spec, lg_spec),
            scratch_shapes=[pltpu.VMEM((dk, dv), F32)],
            operands=(qkv, qkv, qkv, cos, sin, lgb, states, do),
            semantics=("parallel", "parallel", "arbitrary"), riders=riders)

    def impl(qkv, cos, sin, lgb, *shards):
        return fwd_call(qkv, cos, sin, lgb, _Riders(gathers=[s.astype(BF16) for s in shards]))

    @jax.custom_vjp
    def ret(qkv, cos, sin, lgb, *shards):
        o, _, *gathered = impl(qkv, cos, sin, lgb, *shards)
        return (o, *gathered)

    def fwd(qkv, cos, sin, lgb, *shards):
        o, states, *gathered = impl(qkv, cos, sin, lgb, *shards)
        return (o, *gathered), (qkv, cos, sin, lgb, states)

    def bwd(res, cts):
        qkv, cos, sin, lgb, states = res
        dq, dkk, dvv, dlg, *exchanged = bwd_call(qkv, cos, sin, lgb, states, cts[0],
                                                 _Riders(exchanges=list(cts[1:])))
        dqkv = jnp.concatenate([dq[0] + dq[1], dkk[0] + dkk[1], dvv[0] + dvv[1]], axis=1)
        return (dqkv, jnp.zeros_like(cos), jnp.zeros_like(sin), dlg, *_sum_exchanged(exchanged, "ret_" + tag))

    ret.defvjp(fwd, bwd)
    return ret


MLA_SCALE = (MLA_D_NOPE + MLA_D_ROPE) ** -0.5
MLA_Q_ROWS = 256


def _attn_query(q, cos, sin_signed, perm):
    qr = _rot(q[:, MLA_D_NOPE:], cos, sin_signed, perm)
    return jnp.concatenate([q[:, :MLA_D_NOPE], qr], axis=1).astype(BF16)


def _make_attention(tag, nx_rows):
    hw = MLA_D_NOPE + MLA_PAD
    tq = MLA_Q_ROWS

    def operands(kv, kr):
        t = kv.shape[0]
        kv3 = kv.reshape(t, MLA_HEADS, MLA_D_NOPE + MLA_D_V)
        k = jnp.concatenate([kv3[..., :MLA_D_NOPE], jnp.broadcast_to(kr[:, None, :], (t, MLA_HEADS, MLA_PAD))],
                            axis=2).reshape(t, MLA_HEADS * hw).astype(BF16)
        v = kv3[..., MLA_D_NOPE:].reshape(t, MLA_HEADS * MLA_D_V).astype(BF16)
        return k, k.T, v, v.T

    def softmax_t(s_t):
        m = jnp.max(s_t, axis=0, keepdims=True)
        e = jnp.exp((s_t - m) * MLA_SCALE)
        return e, m, 1.0 / jnp.sum(e, axis=0, keepdims=True)

    def fwd_call(q, k, v_t, cos, sin_signed, perm):
        t = q.shape[0]
        nctx = t - nx_rows

        def body(q_ref, cos_ref, sin_ref, perm_ref, k_ref, vt_ref, o_ref, m_ref, linv_ref):
            i = pl.program_id(1)
            qb = _attn_query(q_ref[...], cos_ref[...], sin_ref[...], perm_ref[...])

            def attend(k_part, vt_part):
                e, m, linv = softmax_t(_raw_dot(k_part, qb, "nt"))
                o_t = _raw_dot(vt_part, e * linv, "nn")
                o_ref[...] = o_t.T
                m_ref[...] = m
                linv_ref[...] = linv

            @pl.when(i * tq < nx_rows)
            def _():
                attend(k_ref[...], vt_ref[...])

            @pl.when(i * tq >= nx_rows)
            def _():
                attend(k_ref[pl.ds(nx_rows, nctx), :], vt_ref[:, pl.ds(nx_rows, nctx)])

        stat = jax.ShapeDtypeStruct((MLA_HEADS, 1, t), F32)
        stat_spec = pl.BlockSpec((None, 1, tq), lambda h, i: (h, 0, i))
        return pl.pallas_call(
            body, name="mla_" + tag, grid=(MLA_HEADS, t // tq),
            out_shape=(jax.ShapeDtypeStruct((t, MLA_HEADS * MLA_D_V), F32), stat, stat),
            in_specs=[pl.BlockSpec((tq, hw), lambda h, i: (i, h)),
                      pl.BlockSpec((tq, MLA_PAD), lambda h, i: (i, 0)),
                      pl.BlockSpec((tq, MLA_PAD), lambda h, i: (i, 0)),
                      pl.BlockSpec((MLA_PAD, MLA_PAD), lambda h, i: (0, 0)),
                      pl.BlockSpec((t, hw), lambda h, i: (0, h)),
                      pl.BlockSpec((MLA_D_V, t), lambda h, i: (h, 0))],
            out_specs=(pl.BlockSpec((tq, MLA_D_V), lambda h, i: (i, h)), stat_spec, stat_spec),
            compiler_params=_cparams(("parallel", "arbitrary")),
        )(q, cos, sin_signed, perm, k, v_t)

    def bwd_call(q, k, k_t, v, cos, sin_signed, perm, o, m, linv, do):
        t = q.shape[0]
        nctx = t - nx_rows

        def body(q_ref, cos_ref, sin_ref, perm_ref, k_ref, kt_ref, v_ref, o_ref, m_ref, linv_ref, do_ref,
                 dq_ref, dkn_ref, dv_ref, dkr_ref):
            i = pl.program_id(1)
            cos_t, sin_t, perm_t = cos_ref[...], sin_ref[...], perm_ref[...]
            qb = _attn_query(q_ref[...], cos_t, sin_t, perm_t)
            do_t = do_ref[...]
            dob = do_t.astype(BF16)
            delta = jnp.sum((do_t * o_ref[...]).T, axis=0, keepdims=True)
            m_t, linv_t = m_ref[...], linv_ref[...]

            def grads(k_part, kt_part, v_part):
                e = jnp.exp((_raw_dot(k_part, qb, "nt") - m_t) * MLA_SCALE)
                p_t = e * linv_t
                dp_t = _raw_dot(v_part, dob, "nt")
                ds_t = (p_t * (dp_t - delta) * MLA_SCALE).astype(BF16)
                dq_t = _raw_dot(kt_part, ds_t, "nn")
                dqb = dq_t.T
                g_r = dqb[:, MLA_D_NOPE:]
                dq_r = g_r * cos_t + jnp.dot(g_r * sin_t, perm_t, precision=lax.Precision.HIGHEST,
                                             preferred_element_type=F32)
                dq_ref[...] = jnp.concatenate([dqb[:, :MLA_D_NOPE], dq_r], axis=1)
                return _raw_dot(ds_t, qb, "nn"), _raw_dot(p_t, dob, "nn")

            @pl.when(i == 0)
            def _():
                dkn_ref[...] = jnp.zeros_like(dkn_ref)
                dv_ref[...] = jnp.zeros_like(dv_ref)

            @pl.when(jnp.logical_and(i == 0, pl.program_id(0) == 0))
            def _():
                dkr_ref[...] = jnp.zeros_like(dkr_ref)

            @pl.when(i * tq < nx_rows)
            def _():
                dk, dv = grads(k_ref[...], kt_ref[...], v_ref[...])
                dkn_ref[...] += dk[:, :MLA_D_NOPE]
                dkr_ref[...] += dk[:, MLA_D_NOPE:]
                dv_ref[...] += dv

            @pl.when(i * tq >= nx_rows)
            def _():
                rows = pl.ds(nx_rows, nctx)
                dk, dv = grads(k_ref[rows, :], kt_ref[:, rows], v_ref[rows, :])
                dkn_ref[rows, :] += dk[:, :MLA_D_NOPE]
                dkr_ref[rows, :] += dk[:, MLA_D_NOPE:]
                dv_ref[rows, :] += dv

        stat_spec = pl.BlockSpec((None, 1, tq), lambda h, i: (h, 0, i))
        return pl.pallas_call(
            body, name="mla_bwd_" + tag, grid=(MLA_HEADS, t // tq),
            out_shape=(jax.ShapeDtypeStruct(q.shape, F32),
                       jax.ShapeDtypeStruct((t, MLA_HEADS * MLA_D_NOPE), F32),
                       jax.ShapeDtypeStruct((t, MLA_HEADS * MLA_D_V), F32),
                       jax.ShapeDtypeStruct((t, MLA_PAD), F32)),
            in_specs=[pl.BlockSpec((tq, hw), lambda h, i: (i, h)),
                      pl.BlockSpec((tq, MLA_PAD), lambda h, i: (i, 0)),
                      pl.BlockSpec((tq, MLA_PAD), lambda h, i: (i, 0)),
                      pl.BlockSpec((MLA_PAD, MLA_PAD), lambda h, i: (0, 0)),
                      pl.BlockSpec((t, hw), lambda h, i: (0, h)),
                      pl.BlockSpec((hw, t), lambda h, i: (h, 0)),
                      pl.BlockSpec((t, MLA_D_V), lambda h, i: (0, h)),
                      pl.BlockSpec((tq, MLA_D_V), lambda h, i: (i, h)),
                      stat_spec, stat_spec,
                      pl.BlockSpec((tq, MLA_D_V), lambda h, i: (i, h))],
            out_specs=(pl.BlockSpec((tq, hw), lambda h, i: (i, h)),
                       pl.BlockSpec((t, MLA_D_NOPE), lambda h, i: (0, h)),
                       pl.BlockSpec((t, MLA_D_V), lambda h, i: (0, h)),
                       pl.BlockSpec((t, MLA_PAD), lambda h, i: (0, 0))),
            compiler_params=_cparams(("arbitrary", "arbitrary")),
        )(q, cos, sin_signed, perm, k, k_t, v, o, m, linv, do)

    @jax.custom_vjp
    def attn(q, kv, kr, cos, sin_signed, perm):
        k, _, _, v_t = operands(kv, kr)
        return fwd_call(q, k, v_t, cos, sin_signed, perm)[0]

    def fwd(q, kv, kr, cos, sin_signed, perm):
        k, k_t, v, v_t = operands(kv, kr)
        o, m, linv = fwd_call(q, k, v_t, cos, sin_signed, perm)
        return o, (q, k, k_t, v, cos, sin_signed, perm, o, m, linv)

    def bwd(res, do):
        q, k, k_t, v, cos, sin_signed, perm, o, m, linv = res
        dq, dkn, dv, dkr = bwd_call(q, k, k_t, v, cos, sin_signed, perm, o, m, linv, do)
        t = q.shape[0]
        dkv = jnp.concatenate([dkn.reshape(t, MLA_HEADS, MLA_D_NOPE), dv.reshape(t, MLA_HEADS, MLA_D_V)],
                              axis=2).reshape(t, -1)
        return dq, dkv, dkr, jnp.zeros_like(cos), jnp.zeros_like(sin_signed), jnp.zeros_like(perm)

    attn.defvjp(fwd, bwd)
    return attn


def _adamw_call(w, g, m, v, name):
    rows, cols = w.shape
    tr = _div(rows, max(SUBLANE, (1 << 18) // cols), SUBLANE) if rows % SUBLANE == 0 else rows

    def body(w_ref, g_ref, m_ref, v_ref, d_ref, nm_ref, nv_ref):
        gg = g_ref[...]
        nm = ADAM_B1 * m_ref[...] + (1.0 - ADAM_B1) * gg
        nv = ADAM_B2 * v_ref[...] + (1.0 - ADAM_B2) * jnp.square(gg)
        m_hat = nm / (1.0 - ADAM_B1 ** ADAM_STEP)
        v_hat = nv / (1.0 - ADAM_B2 ** ADAM_STEP)
        d_ref[...] = -ADAM_LR * (m_hat / (jnp.sqrt(v_hat) + ADAM_EPS) + ADAM_WD * w_ref[...])
        nm_ref[...] = nm
        nv_ref[...] = nv

    spec = pl.BlockSpec((tr, cols), lambda i: (i, 0))
    return pl.pallas_call(
        body, name=name, grid=(rows // tr,),
        out_shape=tuple(jax.ShapeDtypeStruct((rows, cols), F32) for _ in range(3)),
        in_specs=[spec] * 4, out_specs=(spec,) * 3,
        compiler_params=_cparams(("parallel",)),
    )(w, g, m, v)


def _adamw(w, g, m, v, name):
    shape = w.shape
    if w.ndim >= 2 and shape[-1] % LANE == 0:
        two = (-1, shape[-1])
    elif w.size % LANE == 0:
        two = (-1, LANE)
    else:
        two = (1, w.size)
    outs = _adamw_call(w.reshape(two), g.reshape(two), m.reshape(two), v.reshape(two), name)
    return tuple(o.reshape(shape) for o in outs)


def _ret_tables(nx, nctx, dk):
    inv = RET_ROPE_BASE ** (-jnp.linspace(0.0, 1.0, dk // 2, dtype=F32))
    ang = jnp.arange(nx, dtype=F32)[:, None] * inv[None, :]
    cos = jnp.concatenate([jnp.cos(ang), jnp.ones((nctx, dk // 2), F32)], axis=0)
    sin = jnp.concatenate([jnp.sin(ang), jnp.zeros((nctx, dk // 2), F32)], axis=0)
    return cos, sin


def _mla_tables(nx, nctx):
    quarter = MLA_D_ROPE // 4
    inv = AXIAL_ROPE_BASE ** (-jnp.arange(quarter, dtype=F32) * 2.0 / (MLA_D_ROPE // 2))
    tok = jnp.arange(nx)
    r_ang = (tok // GRID_W).astype(F32)[:, None] * inv[None, :]
    c_ang = (tok % GRID_W).astype(F32)[:, None] * inv[None, :]
    pad = MLA_PAD - MLA_D_ROPE
    cos = jnp.concatenate([jnp.cos(r_ang), jnp.cos(r_ang), jnp.cos(c_ang), jnp.cos(c_ang),
                           jnp.ones((nx, pad), F32)], axis=1)
    sin = jnp.concatenate([-jnp.sin(r_ang), jnp.sin(r_ang), -jnp.sin(c_ang), jnp.sin(c_ang),
                           jnp.zeros((nx, pad), F32)], axis=1)
    cos = jnp.concatenate([cos, jnp.ones((nctx, MLA_PAD), F32)], axis=0)
    sin = jnp.concatenate([sin, jnp.zeros((nctx, MLA_PAD), F32)], axis=0)
    lane = jnp.arange(MLA_PAD)
    partner = jnp.where(lane % (2 * quarter) < quarter, lane + quarter, lane - quarter)
    perm = ((lane[:, None] == partner[None, :]) & (lane[None, :] < MLA_D_ROPE)).astype(F32)
    return cos, sin, perm


def _columns(gathered):
    return gathered


def _rows(gathered):
    return gathered.reshape(1, gathered.shape[0] * gathered.shape[1], gathered.shape[2])


def _unshard_vec(gathered):
    return jnp.moveaxis(gathered, 0, 1).reshape(gathered.shape[1], -1)


def _pad_rows(a, rows):
    return jnp.concatenate([a, jnp.zeros((rows - a.shape[0],) + a.shape[1:], a.dtype)], axis=0)


def _loss_fn(weights, x, c_all, ctx, tgt, me):
    nx, d_model = x.shape
    nctx = ctx.shape[0]
    dk = d_model // RET_HEADS
    ret_cos, ret_sin = _ret_tables(nx, nctx, dk)
    mla_cos, mla_sin, mla_perm = _mla_tables(nx, nctx)

    rw = functools.partial(_make_rowwise, nx_rows=nx)

    n_ln = weights["ln_g"].size
    small = jnp.concatenate([weights["ln_g"].reshape(-1), weights["ln_b"].reshape(-1),
                             weights["mla_g_q"].reshape(-1), weights["mla_g_kv"].reshape(-1)])
    n_small = small.size
    small = _pad_rows(small.reshape(-1, 1), -(-n_small // LANE) * LANE).reshape(-1, LANE)
    small = _make_gather("small", F32)(small).reshape(N_DEV, -1)
    ln_g = _unshard_vec(small[:, :n_ln].reshape(N_DEV, DEPTH * 2, -1)).reshape(DEPTH, 2, d_model)
    ln_b = _unshard_vec(small[:, n_ln:2 * n_ln].reshape(N_DEV, DEPTH * 2, -1)).reshape(DEPTH, 2, d_model)
    n_g = weights["mla_g_q"].size
    g_q = _unshard_vec(small[:, 2 * n_ln:2 * n_ln + n_g].reshape(N_DEV, DEPTH // 2, -1))
    g_kv = _unshard_vec(small[:, 2 * n_ln + n_g:2 * n_ln + 2 * n_g].reshape(N_DEV, DEPTH // 2, -1))

    n_b = weights["ada_b"].size
    n_dec = weights["ret_decay_logit"].size
    repl = jnp.concatenate([weights["c_ctx"].reshape(-1), weights["ada_b"].reshape(-1),
                            weights["ret_decay_logit"].reshape(-1)])
    n_repl = repl.size
    repl = _pad_rows(repl.reshape(-1, 1), -(-n_repl // LANE) * LANE).reshape(-1, LANE)
    repl = _make_replicated("repl")(repl).reshape(-1)
    c_ctx = repl[:d_model]
    ada_b = repl[d_model:d_model + n_b].reshape(DEPTH, 6 * d_model)
    decay = repl[d_model + n_b:d_model + n_b + n_dec].reshape(weights["ret_decay_logit"].shape)

    cond_rows = 2 * SUBLANE
    cond = _pad_rows(jnp.concatenate([c_all, c_ctx[None, :]], axis=0), cond_rows)
    s_cond = _make_rowwise_plain_silu(cond)
    mods = [_make_mm("ada%d" % i)(s_cond, weights["ada_w"][i][None])[0] for i in range(DEPTH)]
    mod = jnp.concatenate(mods, axis=0)
    mod = _make_gather("mod", F32)(mod)
    mod = jnp.moveaxis(mod.reshape(N_DEV, DEPTH, cond_rows, -1), 0, 2).reshape(DEPTH, cond_rows, -1)
    mod = mod + ada_b[:, None, :]
    mine = lax.dynamic_slice_in_dim(mod, me, 1, axis=1)
    mod = jnp.concatenate([mine, mod[:, N_DEV:N_DEV + 1]], axis=1)
    mod = mod.reshape(DEPTH, 2, 6, 1, d_model)

    def grp(i, j):
        return mod[i, :, j]

    h = jnp.concatenate([x, ctx], axis=0)
    ahead = {"ret_qkv": _make_gather("ret_qkv", BF16)(weights["ret_w_qkv"][0])}
    for i in range(DEPTH):
        j = i // 2
        retention_layer, last = i % 2 == 0, i + 1 == DEPTH
        (u,) = rw(_f_modulate, ("row", "grp", "grp"), (True, True, True), (d_model,), "mod%d" % i)(
            h, grp(i, 1), grp(i, 0))
        if retention_layer:
            qkv, w_g = _make_mm("ret_qkv%d" % j, 1)(u, _columns(ahead.pop("ret_qkv")), weights["ret_w_g"][j])
            gates, w_o = _make_mm("ret_g%d" % j, 1)(u, _columns(w_g), weights["ret_w_o"][j])
            lg = jax.nn.log_sigmoid(decay[j]).reshape(2 * RET_HEADS, 1, 1)
            lgb = jnp.broadcast_to(lg, (2 * RET_HEADS, SUBLANE, LANE))
            o, w_in, w_out = _make_retention("l%d" % j, nx)(
                qkv, ret_cos, ret_sin, lgb, weights["ffn_w_in"][i], weights["ffn_w_out"][i])
            (comb,) = rw(_f_ret_combine, ("row", "row", "row"), (True, True, True), (o.shape[2],),
                         "ret_comb%d" % j, tile=WIDE_ROW_TILE)(gates, o[0], o[1])
            nxt = [] if last else [weights[n][j] for n in MLA_MATRICES]
            y, *gathered = _make_mm("ret_o%d" % j, len(nxt))(comb, _rows(w_o), *nxt)
            if nxt:
                ahead["mla"] = gathered
        else:
            y = _mla_mixer(ahead.pop("mla"), j, u, g_q[j], g_kv[j], mla_cos, mla_sin, mla_perm, rw, nx)
            w_in, w_out = ahead.pop("ffn_in"), ahead.pop("ffn_out")
        h1, u2 = rw(_f_ln_res_mod, ("row", "row", "grp", "par", "par", "grp", "grp"), (True,) * 7,
                    (d_model, d_model), "ln_a%d" % i)(
            h, y, grp(i, 2), ln_g[i, 0][None], ln_b[i, 0][None], grp(i, 4), grp(i, 3))
        if last:
            (ab,) = _make_mm("ffn_in%d" % i)(u2, _columns(w_in))
        elif retention_layer:
            ab, ahead["ffn_in"] = _make_mm("ffn_in%d" % i, 1)(u2, _columns(w_in), weights["ffn_w_in"][i + 1])
        else:
            ab, ahead["ret_qkv"] = _make_mm("ffn_in%d" % i, 1)(u2, _columns(w_in), weights["ret_w_qkv"][j + 1])
        (act,) = rw(_f_swiglu, ("row",), (True,), (ab.shape[1] // 2,), "swiglu%d" % i, tile=WIDE_ROW_TILE)(ab)
        if retention_layer and not last:
            f, ahead["ffn_out"] = _make_mm("ffn_out%d" % i, 1)(act, _rows(w_out), weights["ffn_w_out"][i + 1])
        else:
            (f,) = _make_mm("ffn_out%d" % i)(act, _rows(w_out))
        (h,) = rw(_f_ln_res, ("row", "row", "grp", "par", "par"), (True,) * 5, (d_model,), "ln_f%d" % i)(
            h1, f, grp(i, 5), ln_g[i, 1][None], ln_b[i, 1][None])

    (rows,) = rw(_f_loss, ("row", "row"), (True, False), (LANE,), "loss")(h[:nx], tgt)
    return jnp.sum(rows[:, 0])


def _make_rowwise_plain_silu(cond):
    def body(c_ref, o_ref):
        o_ref[...] = _silu(c_ref[...])

    def call(c):
        return pl.pallas_call(body, name="silu_cond", out_shape=jax.ShapeDtypeStruct(c.shape, F32))(c)

    def bwd_body(c_ref, g_ref, o_ref):
        _, vjp = jax.vjp(_silu, c_ref[...])
        o_ref[...] = vjp(g_ref[...])[0]

    @jax.custom_vjp
    def op(c):
        return call(c)

    def fwd(c):
        return call(c), c

    def bwd(c, g):
        return (pl.pallas_call(bwd_body, name="silu_cond_bwd", out_shape=jax.ShapeDtypeStruct(c.shape, F32))(c, g),)

    op.defvjp(fwd, bwd)
    return op(cond)


MLA_MATRICES = ("mla_w_dq", "mla_w_uq", "mla_w_dkv", "mla_w_ukv", "mla_w_o")


def _mla_mixer(gathered, j, u, g_q, g_kv, cos, sin_signed, perm, rw, nx):
    heads, dn, dr = MLA_HEADS, MLA_D_NOPE, MLA_D_ROPE
    w_dq, w_uq, w_dkv, w_ukv, w_o = gathered
    w_dq, w_dkv, w_ukv, w_o = _rows(w_dq), _rows(w_dkv), _columns(w_ukv), _rows(w_o)

    q_lora = w_uq.shape[1]
    w_uq = jnp.moveaxis(w_uq, 0, 1).reshape(q_lora, heads, dn + dr)
    w_uq = jnp.concatenate([w_uq, jnp.zeros((q_lora, heads, MLA_PAD - dr), w_uq.dtype)], axis=2)
    w_uq = w_uq.reshape(1, q_lora, heads * (dn + MLA_PAD))
    w_dkv = jnp.concatenate([w_dkv, jnp.zeros(w_dkv.shape[:2] + (MLA_PAD - dr,), w_dkv.dtype)], axis=2)

    (cq,) = _make_mm("mla_dq%d" % j)(u, w_dq)
    (cqn,) = rw(_f_rms, ("row", "par"), (True, True), (cq.shape[1],), "rms_q%d" % j)(cq, g_q[None])
    (q,) = _make_mm("mla_uq%d" % j)(cqn, w_uq)
    (ckv,) = _make_mm("mla_dkv%d" % j)(u, w_dkv)
    lora = ckv.shape[1] - MLA_PAD
    c_kv, kr = rw(_f_kv_latent, ("row", "par", "row", "row", "par"), (True, True, False, False, False),
                  (lora, MLA_PAD), "kv_lat%d" % j)(ckv, g_kv[None], cos, sin_signed, perm)
    (kv,) = _make_mm("mla_ukv%d" % j)(c_kv, w_ukv)
    o = _make_attention("l%d" % j, nx)(q, kv, kr, cos, sin_signed, perm)
    return _make_mm("mla_o%d" % j)(o, w_o)[0]


WEIGHT_NAMES = ("c_ctx", "ada_w", "ada_b", "ln_g", "ln_b", "ret_w_qkv", "ret_w_g", "ret_decay_logit", "ret_w_o",
                "mla_w_dq", "mla_g_q", "mla_w_uq", "mla_w_dkv", "mla_g_kv", "mla_w_ukv", "mla_w_o",
                "ffn_w_in", "ffn_w_out")


def kernel(x, c, ctx, c_ctx, ada_w, ada_b, ln_g, ln_b, ret_w_qkv, ret_w_g, ret_decay_logit, ret_w_o, mla_w_dq, mla_g_q, mla_w_uq, mla_w_dkv, mla_g_kv, mla_w_ukv, mla_w_o, ffn_w_in, ffn_w_out, loss_target, m_c_ctx, m_ada_w, m_ada_b, m_ln_g, m_ln_b, m_ret_w_qkv, m_ret_w_g, m_ret_decay_logit, m_ret_w_o, m_mla_w_dq, m_mla_g_q, m_mla_w_uq, m_mla_w_dkv, m_mla_g_kv, m_mla_w_ukv, m_mla_w_o, m_ffn_w_in, m_ffn_w_out, v_c_ctx, v_ada_w, v_ada_b, v_ln_g, v_ln_b, v_ret_w_qkv, v_ret_w_g, v_ret_decay_logit, v_ret_w_o, v_mla_w_dq, v_mla_g_q, v_mla_w_uq, v_mla_w_dkv, v_mla_g_kv, v_mla_w_ukv, v_mla_w_o, v_ffn_w_in, v_ffn_w_out):
    given = dict(locals())
    weights = {n: given[n] for n in WEIGHT_NAMES}
    me = _my_index()
    c_all = _all_gather_call(c, "ag_cond").reshape(N_DEV, -1)

    def loss_of(wts, x2):
        return _loss_fn(wts, x2, c_all, ctx[0], loss_target[0], me)

    loss, (grad_w, grad_x) = jax.value_and_grad(loss_of, argnums=(0, 1))(weights, x[0])
    loss = lax.psum(loss, AXES)
    delta, new_m, new_v = {}, {}, {}
    for n in WEIGHT_NAMES:
        delta[n], new_m[n], new_v[n] = _adamw(weights[n], grad_w[n], given["m_" + n], given["v_" + n], "adamw_" + n)
    return (loss, grad_x[None], *[grad_w[n] for n in WEIGHT_NAMES], *[delta[n] for n in WEIGHT_NAMES],
            *[new_m[n] for n in WEIGHT_NAMES], *[new_v[n] for n in WEIGHT_NAMES])
```

```python
import functools
import math

import jax
import jax.numpy as jnp
from jax import lax
from jax.experimental import pallas as pl
from jax.experimental.pallas import tpu as pltpu

F32 = jnp.float32
BF16 = jnp.bfloat16

AXES = ("x", "y", "c")
N_DEV = 8
MESH_IDS = pl.DeviceIdType.MESH

DEPTH = 4
GRID_W = 64
RET_HEADS = 8
RET_CHUNK = 128
RET_ROPE_BASE = 10000.0
GN_EPS = 1e-6
MLA_HEADS = 16
MLA_D_NOPE = 128
MLA_D_ROPE = 64
MLA_D_V = 128
AXIAL_ROPE_BASE = 10000.0
RMS_EPS = 1e-6
LN_EPS = 1e-5
DEEPNORM_ALPHA = (2 * DEPTH) ** 0.25

ADAM_LR = 0.001
ADAM_B1 = 0.9
ADAM_B2 = 0.999
ADAM_EPS = 1e-08
ADAM_WD = 0.01
ADAM_STEP = 10

LANE = 128
SUBLANE = 8
VMEM_LIMIT = 56 * 1024 * 1024

ROW_TILE = 256
WIDE_ROW_TILE = 64
MM_ROWS = 544
MLA_PAD = 128


def _div(n, cap, mult):
    best = None
    for d in range(mult, min(n, cap) + 1, mult):
        if n % d == 0:
            best = d
    return n if best is None else best


def _cparams(sem=None):
    kw = dict(vmem_limit_bytes=VMEM_LIMIT)
    if sem is not None:
        kw["dimension_semantics"] = sem
    return pltpu.CompilerParams(**kw)


_DN = {"nn": (((1,), (0,)), ((), ())), "nt": (((1,), (1,)), ((), ())), "tn": (((0,), (0,)), ((), ()))}


def _raw_dot(a, b, kind):
    return lax.dot_general(a.astype(BF16), b.astype(BF16), _DN[kind], preferred_element_type=F32)


@functools.partial(jax.custom_vjp, nondiff_argnums=(2,))
def _dot(a, b, kind):
    return _raw_dot(a, b, kind)


def _dot_fwd(a, b, kind):
    return _raw_dot(a, b, kind), (a, b)


def _dot_bwd(kind, res, g):
    a, b = res
    if kind == "nn":
        return _raw_dot(g, b, "nt"), _raw_dot(a, g, "tn")
    if kind == "nt":
        return _raw_dot(g, b, "nn"), _raw_dot(g, a, "tn")
    return _raw_dot(b, g, "nt"), _raw_dot(a, g, "nn")


_dot.defvjp(_dot_fwd, _dot_bwd)


def _sigmoid(x):
    return 1.0 / (1.0 + jnp.exp(-x))


def _silu(x):
    return x * _sigmoid(x)


def _my_index():
    return 4 * lax.axis_index("x") + 2 * lax.axis_index("y") + lax.axis_index("c")


def _all_gather_call(x, name):
    def body(x_ref, out_ref, send_sems, recv_sems, local_sem):
        x_, y_, c_ = lax.axis_index("x"), lax.axis_index("y"), lax.axis_index("c")
        me, sibling = (x_, y_, c_), (x_, y_, 1 - c_)
        chips = [(1 - x_, y_), (x_, 1 - y_), (1 - x_, 1 - y_)]

        def slot(px, py, pc):
            return out_ref.at[4 * px + 2 * py + pc]

        def copy(k, block, to, src=None):
            return pltpu.make_async_remote_copy(
                src_ref=slot(*block) if src is None else src, dst_ref=slot(*block),
                send_sem=send_sems.at[k], recv_sem=recv_sems.at[k],
                device_id=to, device_id_type=MESH_IDS)

        mine = pltpu.make_async_copy(x_ref, slot(*me), local_sem)
        mine.start()
        first = [copy(0, me, sibling, src=x_ref)]
        first += [copy(1 + j, me, (*chip, c_), src=x_ref) for j, chip in enumerate(chips)]
        for cp in first:
            cp.start()
        passed = [copy(4 + j, (*chip, c_), sibling) for j, chip in enumerate(chips)]
        for j, chip in enumerate(chips):
            copy(1 + j, (*chip, c_), me).wait_recv()
            passed[j].start()
        copy(0, sibling, me).wait_recv()
        for j, chip in enumerate(chips):
            copy(4 + j, (*chip, 1 - c_), me).wait_recv()
        for cp in first + passed:
            cp.wait_send()
        mine.wait()

    return pl.pallas_call(
        body, name=name,
        out_shape=jax.ShapeDtypeStruct((N_DEV,) + x.shape, x.dtype),
        in_specs=[pl.BlockSpec(memory_space=pl.ANY)],
        out_specs=pl.BlockSpec(memory_space=pl.ANY),
        scratch_shapes=[pltpu.SemaphoreType.DMA((7,)), pltpu.SemaphoreType.DMA((7,)), pltpu.SemaphoreType.DMA],
    )(x)


def _all_to_all_call(g, name):
    def body(g_ref, out_ref, send_sems, recv_sems, local_sem):
        x_, y_, c_ = lax.axis_index("x"), lax.axis_index("y"), lax.axis_index("c")
        my = 4 * x_ + 2 * y_ + c_
        peers = []
        for k in range(1, N_DEV):
            px = 1 - x_ if (k >> 2) & 1 else x_
            py = 1 - y_ if (k >> 1) & 1 else y_
            pc = 1 - c_ if k & 1 else c_
            peers.append((px, py, pc))

        def copy(k, peer):
            pid = 4 * peer[0] + 2 * peer[1] + peer[2]
            return pltpu.make_async_remote_copy(
                src_ref=g_ref.at[pid], dst_ref=out_ref.at[my],
                send_sem=send_sems.at[k], recv_sem=recv_sems.at[k],
                device_id=peer, device_id_type=MESH_IDS)

        def arrival(k, peer):
            pid = 4 * peer[0] + 2 * peer[1] + peer[2]
            return pltpu.make_async_remote_copy(
                src_ref=g_ref.at[pid], dst_ref=out_ref.at[pid],
                send_sem=send_sems.at[k], recv_sem=recv_sems.at[k],
                device_id=peer, device_id_type=MESH_IDS)

        mine = pltpu.make_async_copy(g_ref.at[my], out_ref.at[my], local_sem)
        mine.start()
        sends = [copy(k, peer) for k, peer in enumerate(peers)]
        for cp in sends:
            cp.start()
        for k, peer in enumerate(peers):
            arrival(k, peer).wait_recv()
        for cp in sends:
            cp.wait_send()
        mine.wait()

    return pl.pallas_call(
        body, name=name,
        out_shape=jax.ShapeDtypeStruct(g.shape, g.dtype),
        in_specs=[pl.BlockSpec(memory_space=pl.ANY)],
        out_specs=pl.BlockSpec(memory_space=pl.ANY),
        scratch_shapes=[pltpu.SemaphoreType.DMA((7,)), pltpu.SemaphoreType.DMA((7,)), pltpu.SemaphoreType.DMA],
    )(g)


def _sum_slots_call(g, name):
    _, rows, cols = g.shape
    tr = _div(rows, max(SUBLANE, (1 << 19) // cols), 16) if rows % 16 == 0 else rows

    def body(g_ref, o_ref):
        acc = g_ref[0].astype(F32)
        for s in range(1, N_DEV):
            acc = acc + g_ref[s].astype(F32)
        o_ref[...] = acc

    return pl.pallas_call(
        body, name=name, grid=(rows // tr,),
        out_shape=jax.ShapeDtypeStruct((rows, cols), F32),
        in_specs=[pl.BlockSpec((N_DEV, tr, cols), lambda i: (0, i, 0))],
        out_specs=pl.BlockSpec((tr, cols), lambda i: (i, 0)),
        compiler_params=_cparams(("parallel",)),
    )(g)


def _make_gather(tag, travel_dtype):
    @jax.custom_vjp
    def gather(w):
        return _all_gather_call(w.astype(travel_dtype), "ag_" + tag)

    def fwd(w):
        return gather(w), None

    def bwd(_, g):
        got = _all_to_all_call(g.astype(travel_dtype), "a2a_" + tag)
        return (_sum_slots_call(got, "sum_" + tag),)

    gather.defvjp(fwd, bwd)
    return gather


def _make_replicated(tag):
    @jax.custom_vjp
    def rep(p):
        return p

    def fwd(p):
        return p, None

    def bwd(_, g):
        return (_sum_slots_call(_all_gather_call(g, "ag_" + tag), "sum_" + tag),)

    rep.defvjp(fwd, bwd)
    return rep


class _Riders:
    def __init__(self, gathers=(), exchanges=()):
        self.gathers = list(gathers)
        self.exchanges = list(exchanges)
        self.n = len(self.gathers) + len(self.exchanges)

    def operands(self):
        return self.gathers + self.exchanges

    def out_shape(self):
        return ([jax.ShapeDtypeStruct((N_DEV,) + g.shape, g.dtype) for g in self.gathers]
                + [jax.ShapeDtypeStruct(e.shape, e.dtype) for e in self.exchanges])

    def scratch(self):
        return [pltpu.SemaphoreType.DMA((7 * self.n,)), pltpu.SemaphoreType.DMA((7 * self.n,)),
                pltpu.SemaphoreType.DMA((self.n,))]

    def _gather_phase(self, phase, r, x_ref, out_ref, send_sems, recv_sems, local_sem):
        x_, y_, c_ = lax.axis_index("x"), lax.axis_index("y"), lax.axis_index("c")
        me, sibling = (x_, y_, c_), (x_, y_, 1 - c_)
        chips = [(1 - x_, y_), (x_, 1 - y_), (1 - x_, 1 - y_)]

        def slot(px, py, pc):
            return out_ref.at[4 * px + 2 * py + pc]

        def copy(k, block, to, src=None):
            return pltpu.make_async_remote_copy(
                src_ref=slot(*block) if src is None else src, dst_ref=slot(*block),
                send_sem=send_sems.at[7 * r + k], recv_sem=recv_sems.at[7 * r + k],
                device_id=to, device_id_type=MESH_IDS)

        def mine():
            return pltpu.make_async_copy(x_ref, slot(*me), local_sem.at[r])

        def first():
            return [copy(0, me, sibling, src=x_ref)] + [copy(1 + j, me, (*chip, c_), src=x_ref)
                                                        for j, chip in enumerate(chips)]

        def passed(j):
            return copy(4 + j, (*chips[j], c_), sibling)

        if phase == "start":
            mine().start()
            for cp in first():
                cp.start()
        elif phase == "pass":
            for j, chip in enumerate(chips):
                copy(1 + j, (*chip, c_), me).wait_recv()
                passed(j).start()
        else:
            copy(0, sibling, me).wait_recv()
            for j, chip in enumerate(chips):
                copy(4 + j, (*chip, 1 - c_), me).wait_recv()
            for cp in first() + [passed(j) for j in range(3)]:
                cp.wait_send()
            mine().wait()

    def _exchange_phase(self, phase, r, g_ref, out_ref, send_sems, recv_sems, local_sem):
        x_, y_, c_ = lax.axis_index("x"), lax.axis_index("y"), lax.axis_index("c")
        my = 4 * x_ + 2 * y_ + c_
        peers = []
        for k in range(1, N_DEV):
            peers.append((1 - x_ if (k >> 2) & 1 else x_, 1 - y_ if (k >> 1) & 1 else y_, 1 - c_ if k & 1 else c_))

        def copy(k, peer, arriving):
            pid = 4 * peer[0] + 2 * peer[1] + peer[2]
            return pltpu.make_async_remote_copy(
                src_ref=g_ref.at[pid], dst_ref=out_ref.at[pid if arriving else my],
                send_sem=send_sems.at[7 * r + k], recv_sem=recv_sems.at[7 * r + k],
                device_id=peer, device_id_type=MESH_IDS)

        def mine():
            return pltpu.make_async_copy(g_ref.at[my], out_ref.at[my], local_sem.at[r])

        if phase == "start":
            mine().start()
            for k, peer in enumerate(peers):
                copy(k, peer, False).start()
        elif phase == "finish":
            for k, peer in enumerate(peers):
                copy(k, peer, True).wait_recv()
            for k, peer in enumerate(peers):
                copy(k, peer, False).wait_send()
            mine().wait()

    def phase(self, phase, in_refs, out_refs, send_sems, recv_sems, local_sem):
        ng = len(self.gathers)
        for r in range(self.n):
            if r < ng:
                self._gather_phase(phase, r, in_refs[r], out_refs[r], send_sems, recv_sems, local_sem)
            else:
                self._exchange_phase(phase, r, in_refs[r], out_refs[r], send_sems, recv_sems, local_sem)


def _call(body, *, name, grid, in_specs, out_specs, out_shape, operands, scratch_shapes=(), semantics, riders=None):
    if riders is None or riders.n == 0:
        return tuple(pl.pallas_call(
            body, name=name, grid=grid, out_shape=tuple(out_shape), in_specs=list(in_specs),
            out_specs=tuple(out_specs), scratch_shapes=list(scratch_shapes), compiler_params=_cparams(semantics),
        )(*operands))

    n_in, n_out, n_scr, rn = len(in_specs), len(out_specs), len(scratch_shapes), riders.n
    steps = math.prod(grid)

    def full_body(*refs):
        ins, r_in = refs[:n_in], refs[n_in:n_in + rn]
        outs, r_out = refs[n_in + rn:n_in + rn + n_out], refs[n_in + rn + n_out:n_in + 2 * rn + n_out]
        scr = refs[n_in + 2 * rn + n_out:n_in + 2 * rn + n_out + n_scr]
        sems = refs[n_in + 2 * rn + n_out + n_scr:]
        step = 0
        for axis, size in enumerate(grid):
            step = step * size + pl.program_id(axis)

        @pl.when(step == 0)
        def _():
            riders.phase("start", r_in, r_out, *sems)

        @pl.when(step == steps // 2)
        def _():
            riders.phase("pass", r_in, r_out, *sems)

        body(*ins, *outs, *scr)

        @pl.when(step == steps - 1)
        def _():
            riders.phase("finish", r_in, r_out, *sems)

    hbm = pl.BlockSpec(memory_space=pl.ANY)
    return tuple(pl.pallas_call(
        full_body, name=name, grid=grid,
        out_shape=tuple(out_shape) + tuple(riders.out_shape()),
        in_specs=list(in_specs) + [hbm] * rn,
        out_specs=tuple(out_specs) + (hbm,) * rn,
        scratch_shapes=list(scratch_shapes) + riders.scratch(),
        compiler_params=_cparams(("arbitrary",) * len(grid)),
    )(*operands, *riders.operands()))


def _mm_nn_call(a, w3, name, riders=None):
    m, k = a.shape
    nb, _, ns = w3.shape
    tm = _div(m, MM_ROWS, SUBLANE)
    tn = _div(ns, max(LANE, (6 << 20) // (2 * k)), LANE)
    tps = ns // tn

    def body(a_ref, w_ref, o_ref):
        o_ref[...] = _raw_dot(a_ref[...], w_ref[...], "nn")

    return _call(
        body, name=name, grid=(nb * tps, m // tm),
        out_shape=[jax.ShapeDtypeStruct((m, nb * ns), F32)],
        in_specs=[pl.BlockSpec((tm, k), lambda j, i: (i, 0)),
                  pl.BlockSpec((None, k, tn), lambda j, i: (j // tps, 0, j % tps))],
        out_specs=[pl.BlockSpec((tm, tn), lambda j, i: (i, j))],
        operands=(a, w3), semantics=("parallel", "parallel"), riders=riders)


def _mm_nt_call(g, w3, name, riders=None):
    m, n = g.shape
    nb, k, ns = w3.shape
    tm = _div(m, MM_ROWS, SUBLANE)
    tk = _div(k, 2048, LANE)
    tn = _div(ns, max(LANE, (6 << 20) // (2 * tk)), LANE)
    tps = ns // tn
    nj = nb * tps

    def body(g_ref, w_ref, o_ref, acc):
        j = pl.program_id(2)
        p = _raw_dot(g_ref[...], w_ref[...], "nt")

        @pl.when(j == 0)
        def _():
            acc[...] = p

        @pl.when(j > 0)
        def _():
            acc[...] += p

        @pl.when(j == nj - 1)
        def _():
            o_ref[...] = acc[...]

    return _call(
        body, name=name, grid=(m // tm, k // tk, nj),
        out_shape=[jax.ShapeDtypeStruct((m, k), F32)],
        in_specs=[pl.BlockSpec((tm, tn), lambda i, kk, j: (i, j)),
                  pl.BlockSpec((None, tk, tn), lambda i, kk, j: (j // tps, kk, j % tps))],
        out_specs=[pl.BlockSpec((tm, tk), lambda i, kk, j: (i, kk))],
        scratch_shapes=[pltpu.VMEM((tm, tk), F32)],
        operands=(g, w3), semantics=("parallel", "parallel", "arbitrary"), riders=riders)


def _mm_dw_call(a_t, g, nb, out_dtype, name, riders=None):
    k, m = a_t.shape
    n = g.shape[1]
    ns = n // nb
    tn = _div(ns, max(LANE, (13 << 20) // (2 * m)), LANE)
    tk = _div(k, max(SUBLANE, (5 << 20) // (2 * m)), 16) if k % 16 == 0 else k
    tps = ns // tn

    def body(a_ref, g_ref, o_ref):
        o_ref[...] = _raw_dot(a_ref[...], g_ref[...], "nn").astype(o_ref.dtype)

    return _call(
        body, name=name, grid=(nb * tps, k // tk),
        out_shape=[jax.ShapeDtypeStruct((nb, k, ns), out_dtype)],
        in_specs=[pl.BlockSpec((tk, m), lambda j, kk: (kk, 0)),
                  pl.BlockSpec((m, tn), lambda j, kk: (0, j))],
        out_specs=[pl.BlockSpec((None, tk, tn), lambda j, kk: (j // tps, kk, j % tps))],
        operands=(a_t, g), semantics=("parallel", "parallel"), riders=riders)


def _sum_exchanged(exchanged, tag):
    return [_sum_slots_call(e, "sum_%s_%d" % (tag, k)) for k, e in enumerate(exchanged)]


def _make_mm(tag, n_ride=0):
    def impl(a, w3, *shards):
        return _mm_nn_call(a, w3, "mm_" + tag, _Riders(gathers=[s.astype(BF16) for s in shards]))

    @jax.custom_vjp
    def mm(a, w3, *shards):
        return impl(a, w3, *shards)

    def fwd(a, w3, *shards):
        return impl(a, w3, *shards), (a, w3)

    def bwd(res, cts):
        a, w3 = res
        g, d_gathered = cts[0].astype(BF16), list(cts[1:])
        order = sorted(range(n_ride), key=lambda r: -d_gathered[r].size)
        with_dw, with_da = order[0::2], order[1::2]
        da, *ex_da = _mm_nt_call(g, w3, "mm_da_" + tag, _Riders(exchanges=[d_gathered[r] for r in with_da]))
        dw, *ex_dw = _mm_dw_call(a.T.astype(BF16), g, w3.shape[0], w3.dtype, "mm_dw_" + tag,
                                 _Riders(exchanges=[d_gathered[r] for r in with_dw]))
        d_shards = [None] * n_ride
        for r, s in zip(with_da, _sum_exchanged(ex_da, tag + "_a")):
            d_shards[r] = s
        for r, s in zip(with_dw, _sum_exchanged(ex_dw, tag + "_w")):
            d_shards[r] = s
        return (da, dw, *d_shards)

    mm.defvjp(fwd, bwd)
    return mm


def _make_rowwise(f, kinds, diff, out_cols, tag, nx_rows, tile=ROW_TILE):
    n_in = len(kinds)
    nxt = nx_rows // tile

    def in_spec(kind, arr):
        if kind == "row":
            return pl.BlockSpec((tile, arr.shape[1]), lambda i: (i, 0))
        if kind == "pair":
            return pl.BlockSpec((arr.shape[0], tile, arr.shape[2]), lambda i: (0, i, 0))
        if kind == "grp":
            return pl.BlockSpec((None, 1, arr.shape[2]), lambda i: (i // nxt, 0, 0))
        return pl.BlockSpec(arr.shape, lambda i: (0,) * arr.ndim)

    def fwd_call(*args):
        t = next(a.shape[0] for a, kd in zip(args, kinds) if kd == "row")

        def body(*refs):
            outs = f(*[r[...] for r in refs[:n_in]])
            for r, o in zip(refs[n_in:], outs):
                r[...] = o

        return pl.pallas_call(
            body, name="rw_" + tag, grid=(t // tile,),
            out_shape=tuple(jax.ShapeDtypeStruct((t, c), F32) for c in out_cols),
            in_specs=[in_spec(kd, a) for kd, a in zip(kinds, args)],
            out_specs=tuple(pl.BlockSpec((tile, c), lambda i: (i, 0)) for c in out_cols),
            compiler_params=_cparams(("parallel",)),
        )(*args)

    def bwd_call(args, cts):
        t = cts[0].shape[0]
        didx = [i for i in range(n_in) if diff[i]]

        def body(*refs):
            i = pl.program_id(0)
            vals = [r[...] for r in refs[:n_in]]
            ct = tuple(r[...] for r in refs[n_in:n_in + len(out_cols)])
            outs = refs[n_in + len(out_cols):]

            def g(*dv):
                full = list(vals)
                for j, v in zip(didx, dv):
                    full[j] = v
                return tuple(f(*full))

            _, vjp = jax.vjp(g, *[vals[j] for j in didx])
            grads = vjp(ct)
            for j, o_ref, d in zip(didx, outs, grads):
                if kinds[j] in ("row", "pair"):
                    o_ref[...] = d
                else:
                    first = (i % nxt == 0) if kinds[j] == "grp" else (i == 0)

                    @pl.when(first)
                    def _(o_ref=o_ref, d=d):
                        o_ref[...] = d

                    @pl.when(jnp.logical_not(first))
                    def _(o_ref=o_ref, d=d):
                        o_ref[...] += d

        def out_spec(j):
            return in_spec(kinds[j], args[j])

        return pl.pallas_call(
            body, name="rw_bwd_" + tag, grid=(t // tile,),
            out_shape=tuple(jax.ShapeDtypeStruct(args[j].shape, F32) for j in didx),
            in_specs=[in_spec(kd, a) for kd, a in zip(kinds, args)]
            + [pl.BlockSpec((tile, c), lambda i: (i, 0)) for c in out_cols],
            out_specs=tuple(out_spec(j) for j in didx),
            compiler_params=_cparams(("arbitrary",)),
        )(*args, *cts)

    @jax.custom_vjp
    def op(*args):
        return fwd_call(*args)

    def fwd(*args):
        return fwd_call(*args), args

    def bwd(args, cts):
        grads = bwd_call(args, cts)
        full = [None] * n_in
        for j, gr in zip([i for i in range(n_in) if diff[i]], grads):
            full[j] = gr
        return tuple(jnp.zeros_like(a) if gfull is None else gfull for a, gfull in zip(args, full))

    op.defvjp(fwd, bwd)
    return op


def _layer_norm(z, g, b):
    mu = jnp.mean(z, axis=-1, keepdims=True)
    var = jnp.mean(jnp.square(z - mu), axis=-1, keepdims=True)
    return (z - mu) * lax.rsqrt(var + LN_EPS) * g + b


def _f_modulate(h, sc, sh):
    return (h * (1.0 + sc) + sh,)


def _f_ln_res(h, y, gate, g, b):
    return (_layer_norm(DEEPNORM_ALPHA * h + gate * y, g, b),)


def _f_ln_res_mod(h, y, gate, g, b, sc, sh):
    h1 = _layer_norm(DEEPNORM_ALPHA * h + gate * y, g, b)
    return h1, h1 * (1.0 + sc) + sh


def _f_swiglu(ab):
    half = ab.shape[1] // 2
    return (_silu(ab[:, :half]) * ab[:, half:],)


def _f_ret_combine(gates, o):
    o_f, o_b = o[0], o[1]
    hv = o_f.shape[1]
    dv = hv // RET_HEADS

    def gn(o):
        parts = []
        for h in range(RET_HEADS):
            oh = o[:, h * dv:(h + 1) * dv]
            mu = jnp.mean(oh, axis=-1, keepdims=True)
            var = jnp.mean(jnp.square(oh - mu), axis=-1, keepdims=True)
            parts.append((oh - mu) * lax.rsqrt(var + GN_EPS))
        return jnp.concatenate(parts, axis=1)

    return (_silu(gates[:, :hv]) * gn(o_f) + _silu(gates[:, hv:]) * gn(o_b),)


def _rms(x, g):
    return x * lax.rsqrt(jnp.mean(jnp.square(x), axis=-1, keepdims=True) + RMS_EPS) * g


def _f_rms(x, g):
    return (_rms(x, g),)


def _rot(x, cos, sin_signed, perm):
    swapped = jnp.dot(x, perm, precision=lax.Precision.HIGHEST, preferred_element_type=F32)
    return x * cos + swapped * sin_signed


def _f_kv_latent(ckv, g, cos, sin_signed, perm):
    lora = ckv.shape[1] - MLA_PAD
    return _rms(ckv[:, :lora], g), _rot(ckv[:, lora:], cos, sin_signed, perm)


def _f_loss(y, tgt):
    row = 0.5 * jnp.mean(jnp.square(y - tgt), axis=-1, keepdims=True)
    return (jnp.broadcast_to(row, (y.shape[0], LANE)),)


def _ret_step(q, k, v, state, lg, cos, sin, d):
    c, dk = q.shape
    half = dk // 2

    def rope(t):
        t1, t2 = t[:, :half], t[:, half:]
        return jnp.concatenate([t1 * cos - t2 * sin, t1 * sin + t2 * cos], axis=1)

    q = rope(q)
    k = rope(k * (dk ** -0.5))
    sgn = (1 - 2 * d).astype(F32)
    ii = lax.broadcasted_iota(jnp.int32, (c, c), 0).astype(F32)
    jj = lax.broadcasted_iota(jnp.int32, (c, c), 1).astype(F32)
    e = (ii - jj) * sgn
    intra = jnp.where(e >= 0, jnp.exp(lg * jnp.maximum(e, 0.0)), 0.0)
    idx = lax.broadcasted_iota(jnp.int32, (c, 1), 0).astype(F32)
    pos = jnp.where(d == 0, idx, c - 1.0 - idx)
    q_dec = jnp.exp(lg * (pos + 1.0))
    k_dec = jnp.exp(lg * (c - 1.0 - pos))
    c_dec = jnp.exp(lg * float(c))
    scores = _dot(q, k, "nt") * intra
    o = _dot(scores, v, "nn") + _dot(q * q_dec, state, "nn")
    new_state = state * c_dec + _dot(k * k_dec, v, "tn")
    return o, new_state


def _ret_chunk_of(d, p, nxc, nc):
    return (1 - d) * ((p + nxc) % nc) + d * (nc - 1 - p)


def _ret_specs(t, dk, dv, nxc, nc, step_of):
    hq = RET_HEADS * dk // dk
    c = RET_CHUNK

    def chunk(d, h, s):
        return _ret_chunk_of(d, step_of(s), nxc, nc)

    q_spec = pl.BlockSpec((c, dk), lambda d, h, s: (chunk(d, h, s), h))
    k_spec = pl.BlockSpec((c, dk), lambda d, h, s: (chunk(d, h, s), hq + h))
    v_spec = pl.BlockSpec((c, dv), lambda d, h, s: (chunk(d, h, s), (2 * RET_HEADS * dk) // dv + h))
    tab_spec = pl.BlockSpec((c, dk // 2), lambda d, h, s: (chunk(d, h, s), 0))
    lg_spec = pl.BlockSpec((None, SUBLANE, LANE), lambda d, h, s: (d * RET_HEADS + h, 0, 0))
    o_spec = pl.BlockSpec((None, c, dv), lambda d, h, s: (d, chunk(d, h, s), h))
    st_spec = pl.BlockSpec((None, None, None, dk, dv), lambda d, h, s: (d, h, step_of(s), 0, 0))
    return q_spec, k_spec, v_spec, tab_spec, lg_spec, o_spec, st_spec


def _make_retention(tag, nx_rows):
    def dims(qkv):
        t, w = qkv.shape
        dk = w // (4 * RET_HEADS)
        return t, dk, 2 * dk, nx_rows // RET_CHUNK, t // RET_CHUNK

    def fwd_call(qkv, cos, sin, lgb, riders):
        t, dk, dv, nxc, nc = dims(qkv)
        q_spec, k_spec, v_spec, tab_spec, lg_spec, o_spec, st_spec = _ret_specs(t, dk, dv, nxc, nc, lambda s: s)

        def body(q_ref, k_ref, v_ref, cos_ref, sin_ref, lg_ref, o_ref, st_ref, state):
            d = pl.program_id(0)

            @pl.when(pl.program_id(2) == 0)
            def _():
                state[...] = jnp.zeros_like(state)

            st_ref[...] = state[...]
            o, new_state = _ret_step(q_ref[...], k_ref[...], v_ref[...], state[...], lg_ref[0:1, 0:1],
                                     cos_ref[...], sin_ref[...], d)
            o_ref[...] = o
            state[...] = new_state

        return _call(
            body, name="ret_" + tag, grid=(2, RET_HEADS, nc),
            out_shape=(jax.ShapeDtypeStruct((2, t, RET_HEADS * dv), F32),
                       jax.ShapeDtypeStruct((2, RET_HEADS, nc, dk, dv), F32)),
            in_specs=[q_spec, k_spec, v_spec, tab_spec, tab_spec, lg_spec],
            out_specs=(o_spec, st_spec),
            scratch_shapes=[pltpu.VMEM((dk, dv), F32)],
            operands=(qkv, qkv, qkv, cos, sin, lgb),
            semantics=("parallel", "parallel", "arbitrary"), riders=riders)

    def bwd_call(qkv, cos, sin, lgb, states, do, riders):
        t, dk, dv, nxc, nc = dims(qkv)
        q_spec, k_spec, v_spec, tab_spec, lg_spec, o_spec, st_spec = _ret_specs(
            t, dk, dv, nxc, nc, lambda s: nc - 1 - s)
        c = RET_CHUNK

        def chunk(d, h, s):
            return _ret_chunk_of(d, nc - 1 - s, nxc, nc)

        dq_spec = pl.BlockSpec((None, c, dk), lambda d, h, s: (d, chunk(d, h, s), h))
        dv_spec = pl.BlockSpec((None, c, dv), lambda d, h, s: (d, chunk(d, h, s), h))

        def body(q_ref, k_ref, v_ref, cos_ref, sin_ref, lg_ref, st_ref, do_ref,
                 dq_ref, dk_ref, dv_ref, dlg_ref, dstate):
            d = pl.program_id(0)
            s = pl.program_id(2)

            @pl.when(s == 0)
            def _():
                dstate[...] = jnp.zeros_like(dstate)

            cos_t, sin_t = cos_ref[...], sin_ref[...]

            def step(q, k, v, state, lg):
                return _ret_step(q, k, v, state, lg, cos_t, sin_t, d)

            _, vjp = jax.vjp(step, q_ref[...], k_ref[...], v_ref[...], st_ref[...], lg_ref[0:1, 0:1])
            dq, dkk, dvv, dst, dlg = vjp((do_ref[...], dstate[...]))
            dq_ref[...] = dq
            dk_ref[...] = dkk
            dv_ref[...] = dvv
            dstate[...] = dst
            corner = jnp.logical_and(lax.broadcasted_iota(jnp.int32, (SUBLANE, LANE), 0) == 0,
                                     lax.broadcasted_iota(jnp.int32, (SUBLANE, LANE), 1) == 0)
            dlg_full = jnp.where(corner, dlg, 0.0)

            @pl.when(s == 0)
            def _():
                dlg_ref[...] = dlg_full

            @pl.when(s > 0)
            def _():
                dlg_ref[...] += dlg_full

        return _call(
            body, name="ret_bwd_" + tag, grid=(2, RET_HEADS, nc),
            out_shape=(jax.ShapeDtypeStruct((2, t, RET_HEADS * dk), F32),
                       jax.ShapeDtypeStruct((2, t, RET_HEADS * dk), F32),
                       jax.ShapeDtypeStruct((2, t, RET_HEADS * dv), F32),
                       jax.ShapeDtypeStruct(lgb.shape, F32)),
            in_specs=[q_spec, k_spec, v_spec, tab_spec, tab_spec, lg_spec, st_spec, o_spec],
            out_specs=(dq_spec, dq_spec, dv_spec, lg_spec),
            scratch_shapes=[pltpu.VMEM((dk, dv), F32)],
            operands=(qkv, qkv, qkv, cos, sin, lgb, states, do),
            semantics=("parallel", "parallel", "arbitrary"), riders=riders)

    def impl(qkv, cos, sin, lgb, *shards):
        return fwd_call(qkv, cos, sin, lgb, _Riders(gathers=[s.astype(BF16) for s in shards]))

    @jax.custom_vjp
    def ret(qkv, cos, sin, lgb, *shards):
        o, _, *gathered = impl(qkv, cos, sin, lgb, *shards)
        return (o, *gathered)

    def fwd(qkv, cos, sin, lgb, *shards):
        o, states, *gathered = impl(qkv, cos, sin, lgb, *shards)
        return (o, *gathered), (qkv, cos, sin, lgb, states)

    def bwd(res, cts):
        qkv, cos, sin, lgb, states = res
        dq, dkk, dvv, dlg, *exchanged = bwd_call(qkv, cos, sin, lgb, states, cts[0],
                                                 _Riders(exchanges=list(cts[1:])))
        dqkv = jnp.concatenate([dq[0] + dq[1], dkk[0] + dkk[1], dvv[0] + dvv[1]], axis=1)
        return (dqkv, jnp.zeros_like(cos), jnp.zeros_like(sin), dlg, *_sum_exchanged(exchanged, "ret_" + tag))

    ret.defvjp(fwd, bwd)
    return ret


MLA_SCALE = (MLA_D_NOPE + MLA_D_ROPE) ** -0.5
MLA_Q_ROWS = 256


def _attn_query(q, cos, sin_signed, perm):
    qr = _rot(q[:, MLA_D_NOPE:], cos, sin_signed, perm)
    return jnp.concatenate([q[:, :MLA_D_NOPE], qr], axis=1).astype(BF16)


def _make_attention(tag, nx_rows):
    hw = MLA_D_NOPE + MLA_PAD
    tq = MLA_Q_ROWS

    def operands(kv, kr):
        t = kv.shape[0]
        kv3 = kv.reshape(t, MLA_HEADS, MLA_D_NOPE + MLA_D_V)
        k = jnp.concatenate([kv3[..., :MLA_D_NOPE], jnp.broadcast_to(kr[:, None, :], (t, MLA_HEADS, MLA_PAD))],
                            axis=2).reshape(t, MLA_HEADS * hw).astype(BF16)
        v = kv3[..., MLA_D_NOPE:].reshape(t, MLA_HEADS * MLA_D_V).astype(BF16)
        return k, k.T, v

    def softmax_t(s_t):
        m = jnp.max(s_t, axis=0, keepdims=True)
        e = jnp.exp((s_t - m) * MLA_SCALE)
        return e, m, 1.0 / jnp.sum(e, axis=0, keepdims=True)

    def fwd_call(q, k_t, v, cos, sin_signed, perm):
        t = q.shape[0]
        nctx = t - nx_rows

        def body(q_ref, cos_ref, sin_ref, perm_ref, kt_ref, v_ref, o_ref, m_ref, linv_ref):
            i = pl.program_id(1)
            qb = _attn_query(q_ref[...], cos_ref[...], sin_ref[...], perm_ref[...])

            def attend(kt_part, v_part):
                s = _raw_dot(qb, kt_part, "nn")
                m = jnp.max(s, axis=1, keepdims=True)
                e = jnp.exp((s - m) * MLA_SCALE)
                linv = 1.0 / jnp.sum(e, axis=1, keepdims=True)
                o_ref[...] = _raw_dot(e * linv, v_part, "nn")
                m_ref[...] = m
                linv_ref[...] = linv

            @pl.when(i * tq < nx_rows)
            def _():
                attend(kt_ref[...], v_ref[...])

            @pl.when(i * tq >= nx_rows)
            def _():
                attend(kt_ref[:, pl.ds(nx_rows, nctx)], v_ref[pl.ds(nx_rows, nctx), :])

        stat = jax.ShapeDtypeStruct((MLA_HEADS, t, 1), F32)
        stat_spec = pl.BlockSpec((None, tq, 1), lambda h, i: (h, i, 0))
        o, m, linv = pl.pallas_call(
            body, name="mla_" + tag, grid=(MLA_HEADS, t // tq),
            out_shape=(jax.ShapeDtypeStruct((t, MLA_HEADS * MLA_D_V), F32), stat, stat),
            in_specs=[pl.BlockSpec((tq, hw), lambda h, i: (i, h)),
                      pl.BlockSpec((tq, MLA_PAD), lambda h, i: (i, 0)),
                      pl.BlockSpec((tq, MLA_PAD), lambda h, i: (i, 0)),
                      pl.BlockSpec((MLA_PAD, MLA_PAD), lambda h, i: (0, 0)),
                      pl.BlockSpec((hw, t), lambda h, i: (h, 0)),
                      pl.BlockSpec((t, MLA_D_V), lambda h, i: (0, h))],
            out_specs=(pl.BlockSpec((tq, MLA_D_V), lambda h, i: (i, h)), stat_spec, stat_spec),
            compiler_params=_cparams(("parallel", "arbitrary")),
        )(q, cos, sin_signed, perm, k_t, v)
        return o, m.reshape(MLA_HEADS, 1, t), linv.reshape(MLA_HEADS, 1, t)

    def bwd_call(q, k, k_t, v, cos, sin_signed, perm, o, m, linv, do):
        t = q.shape[0]
        nctx = t - nx_rows

        def body(q_ref, cos_ref, sin_ref, perm_ref, k_ref, kt_ref, v_ref, o_ref, m_ref, linv_ref, do_ref,
                 dq_ref, dkn_ref, dv_ref, dkr_ref):
            i = pl.program_id(1)
            cos_t, sin_t, perm_t = cos_ref[...], sin_ref[...], perm_ref[...]
            qb = _attn_query(q_ref[...], cos_t, sin_t, perm_t)
            do_t = do_ref[...]
            dob = do_t.astype(BF16)
            delta = jnp.sum((do_t * o_ref[...]).T, axis=0, keepdims=True)
            m_t, linv_t = m_ref[...], linv_ref[...]

            def grads(k_part, kt_part, v_part):
                e = jnp.exp((_raw_dot(k_part, qb, "nt") - m_t) * MLA_SCALE)
                p_t = e * linv_t
                dp_t = _raw_dot(v_part, dob, "nt")
                ds_t = (p_t * (dp_t - delta) * MLA_SCALE).astype(BF16)
                dq_t = _raw_dot(kt_part, ds_t, "nn")
                dqb = dq_t.T
                g_r = dqb[:, MLA_D_NOPE:]
                dq_r = g_r * cos_t + jnp.dot(g_r * sin_t, perm_t, precision=lax.Precision.HIGHEST,
                                             preferred_element_type=F32)
                dq_ref[...] = jnp.concatenate([dqb[:, :MLA_D_NOPE], dq_r], axis=1)
                return _raw_dot(ds_t, qb, "nn"), _raw_dot(p_t, dob, "nn")

            @pl.when(i == 0)
            def _():
                dkn_ref[...] = jnp.zeros_like(dkn_ref)
                dv_ref[...] = jnp.zeros_like(dv_ref)

            @pl.when(jnp.logical_and(i == 0, pl.program_id(0) == 0))
            def _():
                dkr_ref[...] = jnp.zeros_like(dkr_ref)

            @pl.when(i * tq < nx_rows)
            def _():
                dk, dv = grads(k_ref[...], kt_ref[...], v_ref[...])
                dkn_ref[...] += dk[:, :MLA_D_NOPE]
                dkr_ref[...] += dk[:, MLA_D_NOPE:]
                dv_ref[...] += dv

            @pl.when(i * tq >= nx_rows)
            def _():
                rows = pl.ds(nx_rows, nctx)
                dk, dv = grads(k_ref[rows, :], kt_ref[:, rows], v_ref[rows, :])
                dkn_ref[rows, :] += dk[:, :MLA_D_NOPE]
                dkr_ref[rows, :] += dk[:, MLA_D_NOPE:]
                dv_ref[rows, :] += dv

        stat_spec = pl.BlockSpec((None, 1, tq), lambda h, i: (h, 0, i))
        return pl.pallas_call(
            body, name="mla_bwd_" + tag, grid=(MLA_HEADS, t // tq),
            out_shape=(jax.ShapeDtypeStruct(q.shape, F32),
                       jax.ShapeDtypeStruct((t, MLA_HEADS * MLA_D_NOPE), F32),
                       jax.ShapeDtypeStruct((t, MLA_HEADS * MLA_D_V), F32),
                       jax.ShapeDtypeStruct((t, MLA_PAD), F32)),
            in_specs=[pl.BlockSpec((tq, hw), lambda h, i: (i, h)),
                      pl.BlockSpec((tq, MLA_PAD), lambda h, i: (i, 0)),
                      pl.BlockSpec((tq, MLA_PAD), lambda h, i: (i, 0)),
                      pl.BlockSpec((MLA_PAD, MLA_PAD), lambda h, i: (0, 0)),
                      pl.BlockSpec((t, hw), lambda h, i: (0, h)),
                      pl.BlockSpec((hw, t), lambda h, i: (h, 0)),
                      pl.BlockSpec((t, MLA_D_V), lambda h, i: (0, h)),
                      pl.BlockSpec((tq, MLA_D_V), lambda h, i: (i, h)),
                      stat_spec, stat_spec,
                      pl.BlockSpec((tq, MLA_D_V), lambda h, i: (i, h))],
            out_specs=(pl.BlockSpec((tq, hw), lambda h, i: (i, h)),
                       pl.BlockSpec((t, MLA_D_NOPE), lambda h, i: (0, h)),
                       pl.BlockSpec((t, MLA_D_V), lambda h, i: (0, h)),
                       pl.BlockSpec((t, MLA_PAD), lambda h, i: (0, 0))),
            compiler_params=_cparams(("arbitrary", "arbitrary")),
        )(q, cos, sin_signed, perm, k, k_t, v, o, m, linv, do)

    @jax.custom_vjp
    def attn(q, kv, kr, cos, sin_signed, perm):
        _, k_t, v = operands(kv, kr)
        return fwd_call(q, k_t, v, cos, sin_signed, perm)[0]

    def fwd(q, kv, kr, cos, sin_signed, perm):
        k, k_t, v = operands(kv, kr)
        o, m, linv = fwd_call(q, k_t, v, cos, sin_signed, perm)
        return o, (q, k, k_t, v, cos, sin_signed, perm, o, m, linv)

    def bwd(res, do):
        q, k, k_t, v, cos, sin_signed, perm, o, m, linv = res
        dq, dkn, dv, dkr = bwd_call(q, k, k_t, v, cos, sin_signed, perm, o, m, linv, do)
        t = q.shape[0]
        dkv = jnp.concatenate([dkn.reshape(t, MLA_HEADS, MLA_D_NOPE), dv.reshape(t, MLA_HEADS, MLA_D_V)],
                              axis=2).reshape(t, -1)
        return dq, dkv, dkr, jnp.zeros_like(cos), jnp.zeros_like(sin_signed), jnp.zeros_like(perm)

    attn.defvjp(fwd, bwd)
    return attn


def _adamw_call(w, g, m, v, name):
    rows, cols = w.shape
    tr = _div(rows, max(SUBLANE, (1 << 18) // cols), SUBLANE) if rows % SUBLANE == 0 else rows

    def body(w_ref, g_ref, m_ref, v_ref, d_ref, nm_ref, nv_ref):
        gg = g_ref[...]
        nm = ADAM_B1 * m_ref[...] + (1.0 - ADAM_B1) * gg
        nv = ADAM_B2 * v_ref[...] + (1.0 - ADAM_B2) * jnp.square(gg)
        m_hat = nm / (1.0 - ADAM_B1 ** ADAM_STEP)
        v_hat = nv / (1.0 - ADAM_B2 ** ADAM_STEP)
        d_ref[...] = -ADAM_LR * (m_hat / (jnp.sqrt(v_hat) + ADAM_EPS) + ADAM_WD * w_ref[...])
        nm_ref[...] = nm
        nv_ref[...] = nv

    spec = pl.BlockSpec((tr, cols), lambda i: (i, 0))
    return pl.pallas_call(
        body, name=name, grid=(rows // tr,),
        out_shape=tuple(jax.ShapeDtypeStruct((rows, cols), F32) for _ in range(3)),
        in_specs=[spec] * 4, out_specs=(spec,) * 3,
        compiler_params=_cparams(("parallel",)),
    )(w, g, m, v)


def _adamw(w, g, m, v, name):
    shape = w.shape
    if w.ndim >= 2 and shape[-1] % LANE == 0:
        two = (-1, shape[-1])
    elif w.size % LANE == 0:
        two = (-1, LANE)
    else:
        two = (1, w.size)
    outs = _adamw_call(w.reshape(two), g.reshape(two), m.reshape(two), v.reshape(two), name)
    return tuple(o.reshape(shape) for o in outs)


def _ret_tables(nx, nctx, dk):
    inv = RET_ROPE_BASE ** (-jnp.linspace(0.0, 1.0, dk // 2, dtype=F32))
    ang = jnp.arange(nx, dtype=F32)[:, None] * inv[None, :]
    cos = jnp.concatenate([jnp.cos(ang), jnp.ones((nctx, dk // 2), F32)], axis=0)
    sin = jnp.concatenate([jnp.sin(ang), jnp.zeros((nctx, dk // 2), F32)], axis=0)
    return cos, sin


def _mla_tables(nx, nctx):
    quarter = MLA_D_ROPE // 4
    inv = AXIAL_ROPE_BASE ** (-jnp.arange(quarter, dtype=F32) * 2.0 / (MLA_D_ROPE // 2))
    tok = jnp.arange(nx)
    r_ang = (tok // GRID_W).astype(F32)[:, None] * inv[None, :]
    c_ang = (tok % GRID_W).astype(F32)[:, None] * inv[None, :]
    pad = MLA_PAD - MLA_D_ROPE
    cos = jnp.concatenate([jnp.cos(r_ang), jnp.cos(r_ang), jnp.cos(c_ang), jnp.cos(c_ang),
                           jnp.ones((nx, pad), F32)], axis=1)
    sin = jnp.concatenate([-jnp.sin(r_ang), jnp.sin(r_ang), -jnp.sin(c_ang), jnp.sin(c_ang),
                           jnp.zeros((nx, pad), F32)], axis=1)
    cos = jnp.concatenate([cos, jnp.ones((nctx, MLA_PAD), F32)], axis=0)
    sin = jnp.concatenate([sin, jnp.zeros((nctx, MLA_PAD), F32)], axis=0)
    lane = jnp.arange(MLA_PAD)
    partner = jnp.where(lane % (2 * quarter) < quarter, lane + quarter, lane - quarter)
    perm = ((lane[:, None] == partner[None, :]) & (lane[None, :] < MLA_D_ROPE)).astype(F32)
    return cos, sin, perm


def _columns(gathered):
    return gathered


def _rows(gathered):
    return gathered.reshape(1, gathered.shape[0] * gathered.shape[1], gathered.shape[2])


def _unshard_vec(gathered):
    return jnp.moveaxis(gathered, 0, 1).reshape(gathered.shape[1], -1)


def _pad_rows(a, rows):
    return jnp.concatenate([a, jnp.zeros((rows - a.shape[0],) + a.shape[1:], a.dtype)], axis=0)


def _loss_fn(weights, x, c_all, ctx, tgt, me):
    nx, d_model = x.shape
    nctx = ctx.shape[0]
    dk = d_model // RET_HEADS
    ret_cos, ret_sin = _ret_tables(nx, nctx, dk)
    mla_cos, mla_sin, mla_perm = _mla_tables(nx, nctx)

    rw = functools.partial(_make_rowwise, nx_rows=nx)

    n_ln = weights["ln_g"].size
    small = jnp.concatenate([weights["ln_g"].reshape(-1), weights["ln_b"].reshape(-1),
                             weights["mla_g_q"].reshape(-1), weights["mla_g_kv"].reshape(-1)])
    n_small = small.size
    small = _pad_rows(small.reshape(-1, 1), -(-n_small // LANE) * LANE).reshape(-1, LANE)
    small = _make_gather("small", F32)(small).reshape(N_DEV, -1)
    ln_g = _unshard_vec(small[:, :n_ln].reshape(N_DEV, DEPTH * 2, -1)).reshape(DEPTH, 2, d_model)
    ln_b = _unshard_vec(small[:, n_ln:2 * n_ln].reshape(N_DEV, DEPTH * 2, -1)).reshape(DEPTH, 2, d_model)
    n_g = weights["mla_g_q"].size
    g_q = _unshard_vec(small[:, 2 * n_ln:2 * n_ln + n_g].reshape(N_DEV, DEPTH // 2, -1))
    g_kv = _unshard_vec(small[:, 2 * n_ln + n_g:2 * n_ln + 2 * n_g].reshape(N_DEV, DEPTH // 2, -1))

    n_b = weights["ada_b"].size
    n_dec = weights["ret_decay_logit"].size
    repl = jnp.concatenate([weights["c_ctx"].reshape(-1), weights["ada_b"].reshape(-1),
                            weights["ret_decay_logit"].reshape(-1)])
    n_repl = repl.size
    repl = _pad_rows(repl.reshape(-1, 1), -(-n_repl // LANE) * LANE).reshape(-1, LANE)
    repl = _make_replicated("repl")(repl).reshape(-1)
    c_ctx = repl[:d_model]
    ada_b = repl[d_model:d_model + n_b].reshape(DEPTH, 6 * d_model)
    decay = repl[d_model + n_b:d_model + n_b + n_dec].reshape(weights["ret_decay_logit"].shape)

    cond_rows = 2 * SUBLANE
    cond = _pad_rows(jnp.concatenate([c_all, c_ctx[None, :]], axis=0), cond_rows)
    s_cond = _make_rowwise_plain_silu(cond)
    mods = [_make_mm("ada%d" % i)(s_cond, weights["ada_w"][i][None])[0] for i in range(DEPTH)]
    mod = jnp.concatenate(mods, axis=0)
    mod = _make_gather("mod", F32)(mod)
    mod = jnp.moveaxis(mod.reshape(N_DEV, DEPTH, cond_rows, -1), 0, 2).reshape(DEPTH, cond_rows, -1)
    mod = mod + ada_b[:, None, :]
    mine = lax.dynamic_slice_in_dim(mod, me, 1, axis=1)
    mod = jnp.concatenate([mine, mod[:, N_DEV:N_DEV + 1]], axis=1)
    mod = mod.reshape(DEPTH, 2, 6, 1, d_model)

    def grp(i, j):
        return mod[i, :, j]

    h = jnp.concatenate([x, ctx], axis=0)
    ahead = {"ret_qkv": _make_gather("ret_qkv", BF16)(weights["ret_w_qkv"][0])}
    for i in range(DEPTH):
        j = i // 2
        retention_layer, last = i % 2 == 0, i + 1 == DEPTH
        (u,) = rw(_f_modulate, ("row", "grp", "grp"), (True, True, True), (d_model,), "mod%d" % i)(
            h, grp(i, 1), grp(i, 0))
        if retention_layer:
            qkv, w_g = _make_mm("ret_qkv%d" % j, 1)(u, _columns(ahead.pop("ret_qkv")), weights["ret_w_g"][j])
            gates, w_o = _make_mm("ret_g%d" % j, 1)(u, _columns(w_g), weights["ret_w_o"][j])
            lg = jax.nn.log_sigmoid(decay[j]).reshape(2 * RET_HEADS, 1, 1)
            lgb = jnp.broadcast_to(lg, (2 * RET_HEADS, SUBLANE, LANE))
            o, w_in, w_out = _make_retention("l%d" % j, nx)(
                qkv, ret_cos, ret_sin, lgb, weights["ffn_w_in"][i], weights["ffn_w_out"][i])
            (comb,) = rw(_f_ret_combine, ("row", "pair"), (True, True), (o.shape[2],),
                         "ret_comb%d" % j, tile=WIDE_ROW_TILE)(gates, o)
            nxt = [] if last else [weights[n][j] for n in MLA_MATRICES]
            y, *gathered = _make_mm("ret_o%d" % j, len(nxt))(comb, _rows(w_o), *nxt)
            if nxt:
                ahead["mla"] = gathered
        else:
            y = _mla_mixer(ahead.pop("mla"), j, u, g_q[j], g_kv[j], mla_cos, mla_sin, mla_perm, rw, nx)
            w_in, w_out = ahead.pop("ffn_in"), ahead.pop("ffn_out")
        h1, u2 = rw(_f_ln_res_mod, ("row", "row", "grp", "par", "par", "grp", "grp"), (True,) * 7,
                    (d_model, d_model), "ln_a%d" % i)(
            h, y, grp(i, 2), ln_g[i, 0][None], ln_b[i, 0][None], grp(i, 4), grp(i, 3))
        if last:
            (ab,) = _make_mm("ffn_in%d" % i)(u2, _columns(w_in))
        elif retention_layer:
            ab, ahead["ffn_in"] = _make_mm("ffn_in%d" % i, 1)(u2, _columns(w_in), weights["ffn_w_in"][i + 1])
        else:
            ab, ahead["ret_qkv"] = _make_mm("ffn_in%d" % i, 1)(u2, _columns(w_in), weights["ret_w_qkv"][j + 1])
        (act,) = rw(_f_swiglu, ("row",), (True,), (ab.shape[1] // 2,), "swiglu%d" % i, tile=WIDE_ROW_TILE)(ab)
        if retention_layer and not last:
            f, ahead["ffn_out"] = _make_mm("ffn_out%d" % i, 1)(act, _rows(w_out), weights["ffn_w_out"][i + 1])
        else:
            (f,) = _make_mm("ffn_out%d" % i)(act, _rows(w_out))
        (h,) = rw(_f_ln_res, ("row", "row", "grp", "par", "par"), (True,) * 5, (d_model,), "ln_f%d" % i)(
            h1, f, grp(i, 5), ln_g[i, 1][None], ln_b[i, 1][None])

    (rows,) = rw(_f_loss, ("row", "row"), (True, False), (LANE,), "loss")(h[:nx], tgt)
    return jnp.sum(rows[:, 0])


def _make_rowwise_plain_silu(cond):
    def body(c_ref, o_ref):
        o_ref[...] = _silu(c_ref[...])

    def call(c):
        return pl.pallas_call(body, name="silu_cond", out_shape=jax.ShapeDtypeStruct(c.shape, F32))(c)

    def bwd_body(c_ref, g_ref, o_ref):
        _, vjp = jax.vjp(_silu, c_ref[...])
        o_ref[...] = vjp(g_ref[...])[0]

    @jax.custom_vjp
    def op(c):
        return call(c)

    def fwd(c):
        return call(c), c

    def bwd(c, g):
        return (pl.pallas_call(bwd_body, name="silu_cond_bwd", out_shape=jax.ShapeDtypeStruct(c.shape, F32))(c, g),)

    op.defvjp(fwd, bwd)
    return op(cond)


MLA_MATRICES = ("mla_w_dq", "mla_w_uq", "mla_w_dkv", "mla_w_ukv", "mla_w_o")


def _mla_mixer(gathered, j, u, g_q, g_kv, cos, sin_signed, perm, rw, nx):
    heads, dn, dr = MLA_HEADS, MLA_D_NOPE, MLA_D_ROPE
    w_dq, w_uq, w_dkv, w_ukv, w_o = gathered
    w_dq, w_dkv, w_ukv, w_o = _rows(w_dq), _rows(w_dkv), _columns(w_ukv), _rows(w_o)

    q_lora = w_uq.shape[1]
    w_uq = jnp.moveaxis(w_uq, 0, 1).reshape(q_lora, heads, dn + dr)
    w_uq = jnp.concatenate([w_uq, jnp.zeros((q_lora, heads, MLA_PAD - dr), w_uq.dtype)], axis=2)
    w_uq = w_uq.reshape(1, q_lora, heads * (dn + MLA_PAD))
    w_dkv = jnp.concatenate([w_dkv, jnp.zeros(w_dkv.shape[:2] + (MLA_PAD - dr,), w_dkv.dtype)], axis=2)

    (cq,) = _make_mm("mla_dq%d" % j)(u, w_dq)
    (cqn,) = rw(_f_rms, ("row", "par"), (True, True), (cq.shape[1],), "rms_q%d" % j)(cq, g_q[None])
    (q,) = _make_mm("mla_uq%d" % j)(cqn, w_uq)
    (ckv,) = _make_mm("mla_dkv%d" % j)(u, w_dkv)
    lora = ckv.shape[1] - MLA_PAD
    c_kv, kr = rw(_f_kv_latent, ("row", "par", "row", "row", "par"), (True, True, False, False, False),
                  (lora, MLA_PAD), "kv_lat%d" % j)(ckv, g_kv[None], cos, sin_signed, perm)
    (kv,) = _make_mm("mla_ukv%d" % j)(c_kv, w_ukv)
    o = _make_attention("l%d" % j, nx)(q, kv, kr, cos, sin_signed, perm)
    return _make_mm("mla_o%d" % j)(o, w_o)[0]


WEIGHT_NAMES = ("c_ctx", "ada_w", "ada_b", "ln_g", "ln_b", "ret_w_qkv", "ret_w_g", "ret_decay_logit", "ret_w_o",
                "mla_w_dq", "mla_g_q", "mla_w_uq", "mla_w_dkv", "mla_g_kv", "mla_w_ukv", "mla_w_o",
                "ffn_w_in", "ffn_w_out")


def kernel(x, c, ctx, c_ctx, ada_w, ada_b, ln_g, ln_b, ret_w_qkv, ret_w_g, ret_decay_logit, ret_w_o, mla_w_dq, mla_g_q, mla_w_uq, mla_w_dkv, mla_g_kv, mla_w_ukv, mla_w_o, ffn_w_in, ffn_w_out, loss_target, m_c_ctx, m_ada_w, m_ada_b, m_ln_g, m_ln_b, m_ret_w_qkv, m_ret_w_g, m_ret_decay_logit, m_ret_w_o, m_mla_w_dq, m_mla_g_q, m_mla_w_uq, m_mla_w_dkv, m_mla_g_kv, m_mla_w_ukv, m_mla_w_o, m_ffn_w_in, m_ffn_w_out, v_c_ctx, v_ada_w, v_ada_b, v_ln_g, v_ln_b, v_ret_w_qkv, v_ret_w_g, v_ret_decay_logit, v_ret_w_o, v_mla_w_dq, v_mla_g_q, v_mla_w_uq, v_mla_w_dkv, v_mla_g_kv, v_mla_w_ukv, v_mla_w_o, v_ffn_w_in, v_ffn_w_out):
    given = dict(locals())
    weights = {n: given[n] for n in WEIGHT_NAMES}
    me = _my_index()
    c_all = _all_gather_call(c, "ag_cond").reshape(N_DEV, -1)

    def loss_of(wts, x2):
        return _loss_fn(wts, x2, c_all, ctx[0], loss_target[0], me)

    loss, (grad_w, grad_x) = jax.value_and_grad(loss_of, argnums=(0, 1))(weights, x[0])
    loss = lax.psum(loss, AXES)
    delta, new_m, new_v = {}, {}, {}
    for n in WEIGHT_NAMES:
        delta[n], new_m[n], new_v[n] = _adamw(weights[n], grad_w[n], given["m_" + n], given["v_" + n], "adamw_" + n)
    return (loss, grad_x[None], *[grad_w[n] for n in WEIGHT_NAMES], *[delta[n] for n in WEIGHT_NAMES],
            *[new_m[n] for n in WEIGHT_NAMES], *[new_v[n] for n in WEIGHT_NAMES])
```

```python
import functools
import math

import jax
import jax.numpy as jnp
from jax import lax
from jax.experimental import pallas as pl
from jax.experimental.pallas import tpu as pltpu

F32 = jnp.float32
BF16 = jnp.bfloat16

AXES = ("x", "y", "c")
N_DEV = 8
MESH_IDS = pl.DeviceIdType.MESH

DEPTH = 4
GRID_W = 64
RET_HEADS = 8
RET_CHUNK = 128
RET_ROPE_BASE = 10000.0
GN_EPS = 1e-6
MLA_HEADS = 16
MLA_D_NOPE = 128
MLA_D_ROPE = 64
MLA_D_V = 128
AXIAL_ROPE_BASE = 10000.0
RMS_EPS = 1e-6
LN_EPS = 1e-5
DEEPNORM_ALPHA = (2 * DEPTH) ** 0.25

ADAM_LR = 0.001
ADAM_B1 = 0.9
ADAM_B2 = 0.999
ADAM_EPS = 1e-08
ADAM_WD = 0.01
ADAM_STEP = 10

LANE = 128
SUBLANE = 8
VMEM_LIMIT = 56 * 1024 * 1024

ROW_TILE = 256
WIDE_ROW_TILE = 64
MM_ROWS = 544
MLA_PAD = 128


def _div(n, cap, mult):
    best = None
    for d in range(mult, min(n, cap) + 1, mult):
        if n % d == 0:
            best = d
    return n if best is None else best


def _cparams(sem=None):
    kw = dict(vmem_limit_bytes=VMEM_LIMIT)
    if sem is not None:
        kw["dimension_semantics"] = sem
    return pltpu.CompilerParams(**kw)


_DN = {"nn": (((1,), (0,)), ((), ())), "nt": (((1,), (1,)), ((), ())), "tn": (((0,), (0,)), ((), ()))}


def _raw_dot(a, b, kind):
    return lax.dot_general(a.astype(BF16), b.astype(BF16), _DN[kind], preferred_element_type=F32)


@functools.partial(jax.custom_vjp, nondiff_argnums=(2,))
def _dot(a, b, kind):
    return _raw_dot(a, b, kind)


def _dot_fwd(a, b, kind):
    return _raw_dot(a, b, kind), (a, b)


def _dot_bwd(kind, res, g):
    a, b = res
    if kind == "nn":
        return _raw_dot(g, b, "nt"), _raw_dot(a, g, "tn")
    if kind == "nt":
        return _raw_dot(g, b, "nn"), _raw_dot(g, a, "tn")
    return _raw_dot(b, g, "nt"), _raw_dot(a, g, "nn")


_dot.defvjp(_dot_fwd, _dot_bwd)


def _sigmoid(x):
    return 1.0 / (1.0 + jnp.exp(-x))


def _silu(x):
    return x * _sigmoid(x)


def _my_index():
    return 4 * lax.axis_index("x") + 2 * lax.axis_index("y") + lax.axis_index("c")


def _all_gather_call(x, name):
    def body(x_ref, out_ref, send_sems, recv_sems, local_sem):
        x_, y_, c_ = lax.axis_index("x"), lax.axis_index("y"), lax.axis_index("c")
        me, sibling = (x_, y_, c_), (x_, y_, 1 - c_)
        chips = [(1 - x_, y_), (x_, 1 - y_), (1 - x_, 1 - y_)]

        def slot(px, py, pc):
            return out_ref.at[4 * px + 2 * py + pc]

        def copy(k, block, to, src=None):
            return pltpu.make_async_remote_copy(
                src_ref=slot(*block) if src is None else src, dst_ref=slot(*block),
                send_sem=send_sems.at[k], recv_sem=recv_sems.at[k],
                device_id=to, device_id_type=MESH_IDS)

        mine = pltpu.make_async_copy(x_ref, slot(*me), local_sem)
        mine.start()
        first = [copy(0, me, sibling, src=x_ref)]
        first += [copy(1 + j, me, (*chip, c_), src=x_ref) for j, chip in enumerate(chips)]
        for cp in first:
            cp.start()
        passed = [copy(4 + j, (*chip, c_), sibling) for j, chip in enumerate(chips)]
        for j, chip in enumerate(chips):
            copy(1 + j, (*chip, c_), me).wait_recv()
            passed[j].start()
        copy(0, sibling, me).wait_recv()
        for j, chip in enumerate(chips):
            copy(4 + j, (*chip, 1 - c_), me).wait_recv()
        for cp in first + passed:
            cp.wait_send()
        mine.wait()

    return pl.pallas_call(
        body, name=name,
        out_shape=jax.ShapeDtypeStruct((N_DEV,) + x.shape, x.dtype),
        in_specs=[pl.BlockSpec(memory_space=pl.ANY)],
        out_specs=pl.BlockSpec(memory_space=pl.ANY),
        scratch_shapes=[pltpu.SemaphoreType.DMA((7,)), pltpu.SemaphoreType.DMA((7,)), pltpu.SemaphoreType.DMA],
    )(x)


def _all_to_all_call(g, name):
    def body(g_ref, out_ref, send_sems, recv_sems, local_sem):
        x_, y_, c_ = lax.axis_index("x"), lax.axis_index("y"), lax.axis_index("c")
        my = 4 * x_ + 2 * y_ + c_
        peers = []
        for k in range(1, N_DEV):
            px = 1 - x_ if (k >> 2) & 1 else x_
            py = 1 - y_ if (k >> 1) & 1 else y_
            pc = 1 - c_ if k & 1 else c_
            peers.append((px, py, pc))

        def copy(k, peer):
            pid = 4 * peer[0] + 2 * peer[1] + peer[2]
            return pltpu.make_async_remote_copy(
                src_ref=g_ref.at[pid], dst_ref=out_ref.at[my],
                send_sem=send_sems.at[k], recv_sem=recv_sems.at[k],
                device_id=peer, device_id_type=MESH_IDS)

        def arrival(k, peer):
            pid = 4 * peer[0] + 2 * peer[1] + peer[2]
            return pltpu.make_async_remote_copy(
                src_ref=g_ref.at[pid], dst_ref=out_ref.at[pid],
                send_sem=send_sems.at[k], recv_sem=recv_sems.at[k],
                device_id=peer, device_id_type=MESH_IDS)

        mine = pltpu.make_async_copy(g_ref.at[my], out_ref.at[my], local_sem)
        mine.start()
        sends = [copy(k, peer) for k, peer in enumerate(peers)]
        for cp in sends:
            cp.start()
        for k, peer in enumerate(peers):
            arrival(k, peer).wait_recv()
        for cp in sends:
            cp.wait_send()
        mine.wait()

    return pl.pallas_call(
        body, name=name,
        out_shape=jax.ShapeDtypeStruct(g.shape, g.dtype),
        in_specs=[pl.BlockSpec(memory_space=pl.ANY)],
        out_specs=pl.BlockSpec(memory_space=pl.ANY),
        scratch_shapes=[pltpu.SemaphoreType.DMA((7,)), pltpu.SemaphoreType.DMA((7,)), pltpu.SemaphoreType.DMA],
    )(g)


def _sum_slots_call(g, name):
    slots, rows, cols = g.shape
    tr = _div(rows, max(SUBLANE, (1 << 19) // cols), 16) if rows % 16 == 0 else rows

    def body(g_ref, o_ref):
        acc = g_ref[0].astype(F32)
        for s in range(1, slots):
            acc = acc + g_ref[s].astype(F32)
        o_ref[...] = acc

    return pl.pallas_call(
        body, name=name, grid=(rows // tr,),
        out_shape=jax.ShapeDtypeStruct((rows, cols), F32),
        in_specs=[pl.BlockSpec((slots, tr, cols), lambda i: (0, i, 0))],
        out_specs=pl.BlockSpec((tr, cols), lambda i: (i, 0)),
        compiler_params=_cparams(("parallel",)),
    )(g)


def _pair_sum_call(g, half, name):
    _, rows, cols = g.shape
    tr = _div(rows, max(SUBLANE, (1 << 20) // cols), 16) if rows % 16 == 0 else rows
    core = lax.axis_index("c").astype(jnp.int32).reshape(1)

    def body(core_ref, g_ref, h_ref, o_ref):
        del core_ref
        o_ref[...] = (g_ref[...].astype(F32) + h_ref[...].astype(F32)).astype(o_ref.dtype)

    return pl.pallas_call(
        body, name=name,
        grid_spec=pltpu.PrefetchScalarGridSpec(
            num_scalar_prefetch=1, grid=(N_DEV // 2, rows // tr),
            in_specs=[pl.BlockSpec((None, tr, cols), lambda k, i, core_ref: (2 * k + core_ref[0], i, 0)),
                      pl.BlockSpec((None, tr, cols), lambda k, i, core_ref: (k, i, 0))],
            out_specs=pl.BlockSpec((None, tr, cols), lambda k, i, core_ref: (k, i, 0))),
        out_shape=jax.ShapeDtypeStruct(half.shape, half.dtype),
        compiler_params=_cparams(("parallel", "parallel")),
    )(core, g, half)


def _make_gather(tag, travel_dtype):
    @jax.custom_vjp
    def gather(w):
        return _all_gather_call(w.astype(travel_dtype), "ag_" + tag)

    def fwd(w):
        return gather(w), None

    def bwd(_, g):
        got = _all_to_all_call(g.astype(travel_dtype), "a2a_" + tag)
        return (_sum_slots_call(got, "sum_" + tag),)

    gather.defvjp(fwd, bwd)
    return gather


def _make_replicated(tag):
    @jax.custom_vjp
    def rep(p):
        return p

    def fwd(p):
        return p, None

    def bwd(_, g):
        return (_sum_slots_call(_all_gather_call(g, "ag_" + tag), "sum_" + tag),)

    rep.defvjp(fwd, bwd)
    return rep


class _Riders:
    def __init__(self, gathers=(), exchanges=(), sibling_swaps=(), chip_exchanges=()):
        self.gathers = list(gathers)
        self.exchanges = list(exchanges)
        self.sibling_swaps = list(sibling_swaps)
        self.chip_exchanges = list(chip_exchanges)
        self.n = len(self.operands())

    def operands(self):
        return self.gathers + self.exchanges + self.sibling_swaps + self.chip_exchanges

    def out_shape(self):
        return ([jax.ShapeDtypeStruct((N_DEV,) + g.shape, g.dtype) for g in self.gathers]
                + [jax.ShapeDtypeStruct(e.shape, e.dtype) for e in self.exchanges]
                + [jax.ShapeDtypeStruct((N_DEV // 2,) + s.shape[1:], s.dtype) for s in self.sibling_swaps]
                + [jax.ShapeDtypeStruct(p.shape, p.dtype) for p in self.chip_exchanges])

    def scratch(self):
        return [pltpu.SemaphoreType.DMA((7 * self.n,)), pltpu.SemaphoreType.DMA((7 * self.n,)),
                pltpu.SemaphoreType.DMA((self.n,))]

    def _gather_phase(self, phase, r, x_ref, out_ref, send_sems, recv_sems, local_sem):
        x_, y_, c_ = lax.axis_index("x"), lax.axis_index("y"), lax.axis_index("c")
        me, sibling = (x_, y_, c_), (x_, y_, 1 - c_)
        chips = [(1 - x_, y_), (x_, 1 - y_), (1 - x_, 1 - y_)]

        def slot(px, py, pc):
            return out_ref.at[4 * px + 2 * py + pc]

        def copy(k, block, to, src=None):
            return pltpu.make_async_remote_copy(
                src_ref=slot(*block) if src is None else src, dst_ref=slot(*block),
                send_sem=send_sems.at[7 * r + k], recv_sem=recv_sems.at[7 * r + k],
                device_id=to, device_id_type=MESH_IDS)

        def mine():
            return pltpu.make_async_copy(x_ref, slot(*me), local_sem.at[r])

        def first():
            return [copy(0, me, sibling, src=x_ref)] + [copy(1 + j, me, (*chip, c_), src=x_ref)
                                                        for j, chip in enumerate(chips)]

        def passed(j):
            return copy(4 + j, (*chips[j], c_), sibling)

        if phase == "start":
            mine().start()
            for cp in first():
                cp.start()
        elif phase == "pass":
            for j, chip in enumerate(chips):
                copy(1 + j, (*chip, c_), me).wait_recv()
                passed(j).start()
        else:
            copy(0, sibling, me).wait_recv()
            for j, chip in enumerate(chips):
                copy(4 + j, (*chip, 1 - c_), me).wait_recv()
            for cp in first() + [passed(j) for j in range(3)]:
                cp.wait_send()
            mine().wait()

    def _exchange_phase(self, phase, r, g_ref, out_ref, send_sems, recv_sems, local_sem):
        x_, y_, c_ = lax.axis_index("x"), lax.axis_index("y"), lax.axis_index("c")
        my = 4 * x_ + 2 * y_ + c_
        peers = []
        for k in range(1, N_DEV):
            peers.append((1 - x_ if (k >> 2) & 1 else x_, 1 - y_ if (k >> 1) & 1 else y_, 1 - c_ if k & 1 else c_))

        def copy(k, peer, arriving):
            pid = 4 * peer[0] + 2 * peer[1] + peer[2]
            return pltpu.make_async_remote_copy(
                src_ref=g_ref.at[pid], dst_ref=out_ref.at[pid if arriving else my],
                send_sem=send_sems.at[7 * r + k], recv_sem=recv_sems.at[7 * r + k],
                device_id=peer, device_id_type=MESH_IDS)

        def mine():
            return pltpu.make_async_copy(g_ref.at[my], out_ref.at[my], local_sem.at[r])

        if phase == "start":
            mine().start()
            for k, peer in enumerate(peers):
                copy(k, peer, False).start()
        elif phase == "finish":
            for k, peer in enumerate(peers):
                copy(k, peer, True).wait_recv()
            for k, peer in enumerate(peers):
                copy(k, peer, False).wait_send()
            mine().wait()

    def _sibling_phase(self, phase, r, g_ref, out_ref, send_sems, recv_sems):
        x_, y_, c_ = lax.axis_index("x"), lax.axis_index("y"), lax.axis_index("c")

        def copy(chip):
            return pltpu.make_async_remote_copy(
                src_ref=g_ref.at[2 * chip + 1 - c_], dst_ref=out_ref.at[chip],
                send_sem=send_sems.at[7 * r + chip], recv_sem=recv_sems.at[7 * r + chip],
                device_id=(x_, y_, 1 - c_), device_id_type=MESH_IDS)

        if phase == "start":
            for chip in range(N_DEV // 2):
                copy(chip).start()
        elif phase == "finish":
            for chip in range(N_DEV // 2):
                copy(chip).wait_recv()
            for chip in range(N_DEV // 2):
                copy(chip).wait_send()

    def _chip_phase(self, phase, r, p_ref, out_ref, send_sems, recv_sems, local_sem):
        x_, y_, c_ = lax.axis_index("x"), lax.axis_index("y"), lax.axis_index("c")
        my = 2 * x_ + y_
        peers = [(1 - x_ if (k >> 1) & 1 else x_, 1 - y_ if k & 1 else y_) for k in range(1, N_DEV // 2)]

        def copy(k, peer, arriving):
            pid = 2 * peer[0] + peer[1]
            return pltpu.make_async_remote_copy(
                src_ref=p_ref.at[pid], dst_ref=out_ref.at[pid if arriving else my],
                send_sem=send_sems.at[7 * r + k], recv_sem=recv_sems.at[7 * r + k],
                device_id=(*peer, c_), device_id_type=MESH_IDS)

        def mine():
            return pltpu.make_async_copy(p_ref.at[my], out_ref.at[my], local_sem.at[r])

        if phase == "start":
            mine().start()
            for k, peer in enumerate(peers):
                copy(k, peer, False).start()
        elif phase == "finish":
            for k, peer in enumerate(peers):
                copy(k, peer, True).wait_recv()
            for k, peer in enumerate(peers):
                copy(k, peer, False).wait_send()
            mine().wait()

    def phase(self, phase, in_refs, out_refs, send_sems, recv_sems, local_sem):
        ng, ne, ns = len(self.gathers), len(self.exchanges), len(self.sibling_swaps)
        for r in range(self.n):
            if r < ng:
                self._gather_phase(phase, r, in_refs[r], out_refs[r], send_sems, recv_sems, local_sem)
            elif r < ng + ne:
                self._exchange_phase(phase, r, in_refs[r], out_refs[r], send_sems, recv_sems, local_sem)
            elif r < ng + ne + ns:
                self._sibling_phase(phase, r, in_refs[r], out_refs[r], send_sems, recv_sems)
            else:
                self._chip_phase(phase, r, in_refs[r], out_refs[r], send_sems, recv_sems, local_sem)


def _call(body, *, name, grid, in_specs, out_specs, out_shape, operands, scratch_shapes=(), semantics, riders=None):
    if riders is None or riders.n == 0:
        return tuple(pl.pallas_call(
            body, name=name, grid=grid, out_shape=tuple(out_shape), in_specs=list(in_specs),
            out_specs=tuple(out_specs), scratch_shapes=list(scratch_shapes), compiler_params=_cparams(semantics),
        )(*operands))

    n_in, n_out, n_scr, rn = len(in_specs), len(out_specs), len(scratch_shapes), riders.n
    steps = math.prod(grid)

    def full_body(*refs):
        ins, r_in = refs[:n_in], refs[n_in:n_in + rn]
        outs, r_out = refs[n_in + rn:n_in + rn + n_out], refs[n_in + rn + n_out:n_in + 2 * rn + n_out]
        scr = refs[n_in + 2 * rn + n_out:n_in + 2 * rn + n_out + n_scr]
        sems = refs[n_in + 2 * rn + n_out + n_scr:]
        step = 0
        for axis, size in enumerate(grid):
            step = step * size + pl.program_id(axis)

        @pl.when(step == 0)
        def _():
            riders.phase("start", r_in, r_out, *sems)

        @pl.when(step == steps // 2)
        def _():
            riders.phase("pass", r_in, r_out, *sems)

        body(*ins, *outs, *scr)

        @pl.when(step == steps - 1)
        def _():
            riders.phase("finish", r_in, r_out, *sems)

    hbm = pl.BlockSpec(memory_space=pl.ANY)
    return tuple(pl.pallas_call(
        full_body, name=name, grid=grid,
        out_shape=tuple(out_shape) + tuple(riders.out_shape()),
        in_specs=list(in_specs) + [hbm] * rn,
        out_specs=tuple(out_specs) + (hbm,) * rn,
        scratch_shapes=list(scratch_shapes) + riders.scratch(),
        compiler_params=_cparams(("arbitrary",) * len(grid)),
    )(*operands, *riders.operands()))


def _mm_nn_call(a, w3, name, riders=None):
    m, k = a.shape
    nb, _, ns = w3.shape
    tm = _div(m, MM_ROWS, SUBLANE)
    tn = _div(ns, max(LANE, (6 << 20) // (2 * k)), LANE)
    tps = ns // tn

    def body(a_ref, w_ref, o_ref):
        o_ref[...] = _raw_dot(a_ref[...], w_ref[...], "nn")

    return _call(
        body, name=name, grid=(nb * tps, m // tm),
        out_shape=[jax.ShapeDtypeStruct((m, nb * ns), F32)],
        in_specs=[pl.BlockSpec((tm, k), lambda j, i: (i, 0)),
                  pl.BlockSpec((None, k, tn), lambda j, i: (j // tps, 0, j % tps))],
        out_specs=[pl.BlockSpec((tm, tn), lambda j, i: (i, j))],
        operands=(a, w3), semantics=("parallel", "parallel"), riders=riders)


def _mm_nt_call(g, w3, name, riders=None):
    m, n = g.shape
    nb, k, ns = w3.shape
    tm = _div(m, min(MM_ROWS, (13 << 19) // (4 * k)), 16) if m % 16 == 0 else m
    tn = _div(ns, max(LANE, (6 << 20) // (2 * k)), LANE)
    tps = ns // tn
    nj = nb * tps

    def body(g_ref, w_ref, o_ref, gb_ref, acc):
        j = pl.program_id(1)
        gb = g_ref[...].astype(BF16)
        gb_ref[...] = gb
        p = _raw_dot(gb, w_ref[...], "nt")

        @pl.when(j == 0)
        def _():
            acc[...] = p

        @pl.when(j > 0)
        def _():
            acc[...] += p

        @pl.when(j == nj - 1)
        def _():
            o_ref[...] = acc[...]

    return _call(
        body, name=name, grid=(m // tm, nj),
        out_shape=[jax.ShapeDtypeStruct((m, k), F32), jax.ShapeDtypeStruct((m, n), BF16)],
        in_specs=[pl.BlockSpec((tm, tn), lambda i, j: (i, j)),
                  pl.BlockSpec((None, k, tn), lambda i, j: (j // tps, 0, j % tps))],
        out_specs=[pl.BlockSpec((tm, k), lambda i, j: (i, 0)),
                   pl.BlockSpec((tm, tn), lambda i, j: (i, j))],
        scratch_shapes=[pltpu.VMEM((tm, k), F32)],
        operands=(g, w3), semantics=("parallel", "arbitrary"), riders=riders)


def _mm_dw_call(a_t, g, nb, out_dtype, name, riders=None):
    k, m = a_t.shape
    n = g.shape[1]
    ns = n // nb
    tn = _div(ns, max(LANE, (13 << 20) // (2 * m)), LANE)
    tk = _div(k, max(SUBLANE, (5 << 20) // (2 * m)), 16) if k % 16 == 0 else k
    tps = ns // tn

    def body(a_ref, g_ref, o_ref):
        o_ref[...] = _raw_dot(a_ref[...], g_ref[...], "nn").astype(o_ref.dtype)

    return _call(
        body, name=name, grid=(nb * tps, k // tk),
        out_shape=[jax.ShapeDtypeStruct((nb, k, ns), out_dtype)],
        in_specs=[pl.BlockSpec((tk, m), lambda j, kk: (kk, 0)),
                  pl.BlockSpec((m, tn), lambda j, kk: (0, j))],
        out_specs=[pl.BlockSpec((None, tk, tn), lambda j, kk: (j // tps, kk, j % tps))],
        operands=(a_t, g), semantics=("parallel", "parallel"), riders=riders)


def _sum_exchanged(exchanged, tag):
    return [_sum_slots_call(e, "sum_%s_%d" % (tag, k)) for k, e in enumerate(exchanged)]


def _make_mm(tag, n_ride=0):
    def impl(a, w3, *shards):
        return _mm_nn_call(a, w3, "mm_" + tag, _Riders(gathers=[s.astype(BF16) for s in shards]))

    @jax.custom_vjp
    def mm(a, w3, *shards):
        return impl(a, w3, *shards)

    def fwd(a, w3, *shards):
        return impl(a, w3, *shards), (a, w3)

    def bwd(res, cts):
        a, w3 = res
        d_gathered = list(cts[1:])
        da, gb, *halves = _mm_nt_call(cts[0], w3, "mm_da_" + tag, _Riders(sibling_swaps=d_gathered))
        pairs = [_pair_sum_call(d, h, "pair_%s_%d" % (tag, r)) for r, (d, h) in enumerate(zip(d_gathered, halves))]
        dw, *exchanged = _mm_dw_call(a.T.astype(BF16), gb, w3.shape[0], w3.dtype, "mm_dw_" + tag,
                                     _Riders(chip_exchanges=pairs))
        return (da, dw, *_sum_exchanged(exchanged, tag))

    mm.defvjp(fwd, bwd)
    return mm


def _make_rowwise(f, kinds, diff, out_cols, tag, nx_rows, tile=ROW_TILE):
    n_in = len(kinds)
    nxt = nx_rows // tile

    def in_spec(kind, arr):
        if kind == "row":
            return pl.BlockSpec((tile, arr.shape[1]), lambda i: (i, 0))
        if kind == "pair":
            return pl.BlockSpec((arr.shape[0], tile, arr.shape[2]), lambda i: (0, i, 0))
        if kind == "grp":
            return pl.BlockSpec((None, 1, arr.shape[2]), lambda i: (i // nxt, 0, 0))
        return pl.BlockSpec(arr.shape, lambda i: (0,) * arr.ndim)

    def fwd_call(*args):
        t = next(a.shape[0] for a, kd in zip(args, kinds) if kd == "row")

        def body(*refs):
            outs = f(*[r[...] for r in refs[:n_in]])
            for r, o in zip(refs[n_in:], outs):
                r[...] = o

        return pl.pallas_call(
            body, name="rw_" + tag, grid=(t // tile,),
            out_shape=tuple(jax.ShapeDtypeStruct((t, c), F32) for c in out_cols),
            in_specs=[in_spec(kd, a) for kd, a in zip(kinds, args)],
            out_specs=tuple(pl.BlockSpec((tile, c), lambda i: (i, 0)) for c in out_cols),
            compiler_params=_cparams(("parallel",)),
        )(*args)

    def bwd_call(args, cts):
        t = cts[0].shape[0]
        didx = [i for i in range(n_in) if diff[i]]

        def body(*refs):
            i = pl.program_id(0)
            vals = [r[...] for r in refs[:n_in]]
            ct = tuple(r[...] for r in refs[n_in:n_in + len(out_cols)])
            outs = refs[n_in + len(out_cols):]

            def g(*dv):
                full = list(vals)
                for j, v in zip(didx, dv):
                    full[j] = v
                return tuple(f(*full))

            _, vjp = jax.vjp(g, *[vals[j] for j in didx])
            grads = vjp(ct)
            for j, o_ref, d in zip(didx, outs, grads):
                if kinds[j] in ("row", "pair"):
                    o_ref[...] = d
                else:
                    first = (i % nxt == 0) if kinds[j] == "grp" else (i == 0)

                    @pl.when(first)
                    def _(o_ref=o_ref, d=d):
                        o_ref[...] = d

                    @pl.when(jnp.logical_not(first))
                    def _(o_ref=o_ref, d=d):
                        o_ref[...] += d

        def out_spec(j):
            return in_spec(kinds[j], args[j])

        return pl.pallas_call(
            body, name="rw_bwd_" + tag, grid=(t // tile,),
            out_shape=tuple(jax.ShapeDtypeStruct(args[j].shape, F32) for j in didx),
            in_specs=[in_spec(kd, a) for kd, a in zip(kinds, args)]
            + [pl.BlockSpec((tile, c), lambda i: (i, 0)) for c in out_cols],
            out_specs=tuple(out_spec(j) for j in didx),
            compiler_params=_cparams(("arbitrary",)),
        )(*args, *cts)

    @jax.custom_vjp
    def op(*args):
        return fwd_call(*args)

    def fwd(*args):
        return fwd_call(*args), args

    def bwd(args, cts):
        grads = bwd_call(args, cts)
        full = [None] * n_in
        for j, gr in zip([i for i in range(n_in) if diff[i]], grads):
            full[j] = gr
        return tuple(jnp.zeros_like(a) if gfull is None else gfull for a, gfull in zip(args, full))

    op.defvjp(fwd, bwd)
    return op


def _layer_norm(z, g, b):
    mu = jnp.mean(z, axis=-1, keepdims=True)
    var = jnp.mean(jnp.square(z - mu), axis=-1, keepdims=True)
    return (z - mu) * lax.rsqrt(var + LN_EPS) * g + b


def _f_modulate(h, sc, sh):
    return (h * (1.0 + sc) + sh,)


def _f_ln_res(h, y, gate, g, b):
    return (_layer_norm(DEEPNORM_ALPHA * h + gate * y, g, b),)


def _f_ln_res_mod(h, y, gate, g, b, sc, sh):
    h1 = _layer_norm(DEEPNORM_ALPHA * h + gate * y, g, b)
    return h1, h1 * (1.0 + sc) + sh


def _f_swiglu(ab):
    half = ab.shape[1] // 2
    return (_silu(ab[:, :half]) * ab[:, half:],)


def _f_ret_combine(gates, o):
    o_f, o_b = o[0], o[1]
    hv = o_f.shape[1]
    dv = hv // RET_HEADS

    def gn(o):
        parts = []
        for h in range(RET_HEADS):
            oh = o[:, h * dv:(h + 1) * dv]
            mu = jnp.mean(oh, axis=-1, keepdims=True)
            var = jnp.mean(jnp.square(oh - mu), axis=-1, keepdims=True)
            parts.append((oh - mu) * lax.rsqrt(var + GN_EPS))
        return jnp.concatenate(parts, axis=1)

    return (_silu(gates[:, :hv]) * gn(o_f) + _silu(gates[:, hv:]) * gn(o_b),)


def _rms(x, g):
    return x * lax.rsqrt(jnp.mean(jnp.square(x), axis=-1, keepdims=True) + RMS_EPS) * g


def _f_rms(x, g):
    return (_rms(x, g),)


def _rot(x, cos, sin_signed, perm):
    swapped = jnp.dot(x, perm, precision=lax.Precision.HIGHEST, preferred_element_type=F32)
    return x * cos + swapped * sin_signed


def _f_kv_latent(ckv, g, cos, sin_signed, perm):
    lora = ckv.shape[1] - MLA_PAD
    return _rms(ckv[:, :lora], g), _rot(ckv[:, lora:], cos, sin_signed, perm)


def _f_loss(y, tgt):
    row = 0.5 * jnp.mean(jnp.square(y - tgt), axis=-1, keepdims=True)
    return (jnp.broadcast_to(row, (y.shape[0], LANE)),)


def _ret_step(q, k, v, state, lg, cos, sin, d):
    c, dk = q.shape
    half = dk // 2

    def rope(t):
        t1, t2 = t[:, :half], t[:, half:]
        return jnp.concatenate([t1 * cos - t2 * sin, t1 * sin + t2 * cos], axis=1)

    q = rope(q)
    k = rope(k * (dk ** -0.5))
    sgn = (1 - 2 * d).astype(F32)
    ii = lax.broadcasted_iota(jnp.int32, (c, c), 0).astype(F32)
    jj = lax.broadcasted_iota(jnp.int32, (c, c), 1).astype(F32)
    e = (ii - jj) * sgn
    intra = jnp.where(e >= 0, jnp.exp(lg * jnp.maximum(e, 0.0)), 0.0)
    idx = lax.broadcasted_iota(jnp.int32, (c, 1), 0).astype(F32)
    pos = jnp.where(d == 0, idx, c - 1.0 - idx)
    q_dec = jnp.exp(lg * (pos + 1.0))
    k_dec = jnp.exp(lg * (c - 1.0 - pos))
    c_dec = jnp.exp(lg * float(c))
    scores = _dot(q, k, "nt") * intra
    o = _dot(scores, v, "nn") + _dot(q * q_dec, state, "nn")
    new_state = state * c_dec + _dot(k * k_dec, v, "tn")
    return o, new_state


def _ret_chunk_of(d, p, nxc, nc):
    return (1 - d) * ((p + nxc) % nc) + d * (nc - 1 - p)


def _ret_specs(t, dk, dv, nxc, nc, step_of):
    hq = RET_HEADS * dk // dk
    c = RET_CHUNK

    def chunk(d, h, s):
        return _ret_chunk_of(d, step_of(s), nxc, nc)

    q_spec = pl.BlockSpec((c, dk), lambda d, h, s: (chunk(d, h, s), h))
    k_spec = pl.BlockSpec((c, dk), lambda d, h, s: (chunk(d, h, s), hq + h))
    v_spec = pl.BlockSpec((c, dv), lambda d, h, s: (chunk(d, h, s), (2 * RET_HEADS * dk) // dv + h))
    tab_spec = pl.BlockSpec((c, dk // 2), lambda d, h, s: (chunk(d, h, s), 0))
    lg_spec = pl.BlockSpec((None, SUBLANE, LANE), lambda d, h, s: (d * RET_HEADS + h, 0, 0))
    o_spec = pl.BlockSpec((None, c, dv), lambda d, h, s: (d, chunk(d, h, s), h))
    st_spec = pl.BlockSpec((None, None, None, dk, dv), lambda d, h, s: (d, h, step_of(s), 0, 0))
    return q_spec, k_spec, v_spec, tab_spec, lg_spec, o_spec, st_spec


def _make_retention(tag, nx_rows):
    def dims(qkv):
        t, w = qkv.shape
        dk = w // (4 * RET_HEADS)
        return t, dk, 2 * dk, nx_rows // RET_CHUNK, t // RET_CHUNK

    def fwd_call(qkv, cos, sin, lgb, riders):
        t, dk, dv, nxc, nc = dims(qkv)
        q_spec, k_spec, v_spec, tab_spec, lg_spec, o_spec, st_spec = _ret_specs(t, dk, dv, nxc, nc, lambda s: s)

        def body(q_ref, k_ref, v_ref, cos_ref, sin_ref, lg_ref, o_ref, st_ref, state):
            d = pl.program_id(0)

            @pl.when(pl.program_id(2) == 0)
            def _():
                state[...] = jnp.zeros_like(state)

            st_ref[...] = state[...]
            o, new_state = _ret_step(q_ref[...], k_ref[...], v_ref[...], state[...], lg_ref[0:1, 0:1],
                                     cos_ref[...], sin_ref[...], d)
            o_ref[...] = o
            state[...] = new_state

        return _call(
            body, name="ret_" + tag, grid=(2, RET_HEADS, nc),
            out_shape=(jax.ShapeDtypeStruct((2, t, RET_HEADS * dv), F32),
                       jax.ShapeDtypeStruct((2, RET_HEADS, nc, dk, dv), F32)),
            in_specs=[q_spec, k_spec, v_spec, tab_spec, tab_spec, lg_spec],
            out_specs=(o_spec, st_spec),
            scratch_shapes=[pltpu.VMEM((dk, dv), F32)],
            operands=(qkv, qkv, qkv, cos, sin, lgb),
            semantics=("parallel", "parallel", "arbitrary"), riders=riders)

    def bwd_call(qkv, cos, sin, lgb, states, do, riders):
        t, dk, dv, nxc, nc = dims(qkv)
        q_spec, k_spec, v_spec, tab_spec, lg_spec, o_spec, st_spec = _ret_specs(
            t, dk, dv, nxc, nc, lambda s: nc - 1 - s)
        c = RET_CHUNK

        def chunk(d, h, s):
            return _ret_chunk_of(d, nc - 1 - s, nxc, nc)

        dq_spec = pl.BlockSpec((None, c, dk), lambda d, h, s: (d, chunk(d, h, s), h))
        dv_spec = pl.BlockSpec((None, c, dv), lambda d, h, s: (d, chunk(d, h, s), h))

        def body(q_ref, k_ref, v_ref, cos_ref, sin_ref, lg_ref, st_ref, do_ref,
                 dq_ref, dk_ref, dv_ref, dlg_ref, dstate):
            d = pl.program_id(0)
            s = pl.program_id(2)

            @pl.when(s == 0)
            def _():
                dstate[...] = jnp.zeros_like(dstate)

            cos_t, sin_t = cos_ref[...], sin_ref[...]

            def step(q, k, v, state, lg):
                return _ret_step(q, k, v, state, lg, cos_t, sin_t, d)

            _, vjp = jax.vjp(step, q_ref[...], k_ref[...], v_ref[...], st_ref[...], lg_ref[0:1, 0:1])
            dq, dkk, dvv, dst, dlg = vjp((do_ref[...], dstate[...]))
            dq_ref[...] = dq
            dk_ref[...] = dkk
            dv_ref[...] = dvv
            dstate[...] = dst
            corner = jnp.logical_and(lax.broadcasted_iota(jnp.int32, (SUBLANE, LANE), 0) == 0,
                                     lax.broadcasted_iota(jnp.int32, (SUBLANE, LANE), 1) == 0)
            dlg_full = jnp.where(corner, dlg, 0.0)

            @pl.when(s == 0)
            def _():
                dlg_ref[...] = dlg_full

            @pl.when(s > 0)
            def _():
                dlg_ref[...] += dlg_full

        return _call(
            body, name="ret_bwd_" + tag, grid=(2, RET_HEADS, nc),
            out_shape=(jax.ShapeDtypeStruct((2, t, RET_HEADS * dk), F32),
                       jax.ShapeDtypeStruct((2, t, RET_HEADS * dk), F32),
                       jax.ShapeDtypeStruct((2, t, RET_HEADS * dv), F32),
                       jax.ShapeDtypeStruct(lgb.shape, F32)),
            in_specs=[q_spec, k_spec, v_spec, tab_spec, tab_spec, lg_spec, st_spec, o_spec],
            out_specs=(dq_spec, dq_spec, dv_spec, lg_spec),
            scratch_shapes=[pltpu.VMEM((dk, dv), F32)],
            operands=(qkv, qkv, qkv, cos, sin, lgb, states, do),
            semantics=("parallel", "parallel", "arbitrary"), riders=riders)

    def impl(qkv, cos, sin, lgb, *shards):
        return fwd_call(qkv, cos, sin, lgb, _Riders(gathers=[s.astype(BF16) for s in shards]))

    @jax.custom_vjp
    def ret(qkv, cos, sin, lgb, *shards):
        o, _, *gathered = impl(qkv, cos, sin, lgb, *shards)
        return (o, *gathered)

    def fwd(qkv, cos, sin, lgb, *shards):
        o, states, *gathered = impl(qkv, cos, sin, lgb, *shards)
        return (o, *gathered), (qkv, cos, sin, lgb, states)

    def bwd(res, cts):
        qkv, cos, sin, lgb, states = res
        dq, dkk, dvv, dlg, *exchanged = bwd_call(qkv, cos, sin, lgb, states, cts[0],
                                                 _Riders(exchanges=list(cts[1:])))
        dqkv = jnp.concatenate([dq[0] + dq[1], dkk[0] + dkk[1], dvv[0] + dvv[1]], axis=1)
        return (dqkv, jnp.zeros_like(cos), jnp.zeros_like(sin), dlg, *_sum_exchanged(exchanged, "ret_" + tag))

    ret.defvjp(fwd, bwd)
    return ret


MLA_SCALE = (MLA_D_NOPE + MLA_D_ROPE) ** -0.5
MLA_Q_ROWS = 256


def _attn_query(q, cos, sin_signed, perm):
    qr = _rot(q[:, MLA_D_NOPE:], cos, sin_signed, perm)
    return jnp.concatenate([q[:, :MLA_D_NOPE], qr], axis=1).astype(BF16)


def _make_attention(tag, nx_rows):
    hw = MLA_D_NOPE + MLA_PAD
    tq = MLA_Q_ROWS

    def operands(kv, kr):
        t = kv.shape[0]
        kv3 = kv.reshape(t, MLA_HEADS, MLA_D_NOPE + MLA_D_V)
        k = jnp.concatenate([kv3[..., :MLA_D_NOPE], jnp.broadcast_to(kr[:, None, :], (t, MLA_HEADS, MLA_PAD))],
                            axis=2).reshape(t, MLA_HEADS * hw).astype(BF16)
        v = kv3[..., MLA_D_NOPE:].reshape(t, MLA_HEADS * MLA_D_V).astype(BF16)
        return k, k.T, v

    def softmax_t(s_t):
        m = jnp.max(s_t, axis=0, keepdims=True)
        e = jnp.exp((s_t - m) * MLA_SCALE)
        return e, m, 1.0 / jnp.sum(e, axis=0, keepdims=True)

    def fwd_call(q, k_t, v, cos, sin_signed, perm, riders):
        t = q.shape[0]
        nctx = t - nx_rows

        def body(q_ref, cos_ref, sin_ref, perm_ref, kt_ref, v_ref, o_ref, m_ref, linv_ref):
            i = pl.program_id(1)
            qb = _attn_query(q_ref[...], cos_ref[...], sin_ref[...], perm_ref[...])

            def attend(kt_part, v_part):
                s = _raw_dot(qb, kt_part, "nn")
                m = jnp.max(s, axis=1, keepdims=True)
                e = jnp.exp((s - m) * MLA_SCALE)
                linv = 1.0 / jnp.sum(e, axis=1, keepdims=True)
                o_ref[...] = _raw_dot(e * linv, v_part, "nn")
                m_ref[...] = m
                linv_ref[...] = linv

            @pl.when(i * tq < nx_rows)
            def _():
                attend(kt_ref[...], v_ref[...])

            @pl.when(i * tq >= nx_rows)
            def _():
                attend(kt_ref[:, pl.ds(nx_rows, nctx)], v_ref[pl.ds(nx_rows, nctx), :])

        stat = jax.ShapeDtypeStruct((MLA_HEADS, t, 1), F32)
        stat_spec = pl.BlockSpec((None, tq, 1), lambda h, i: (h, i, 0))
        o, m, linv, *gathered = _call(
            body, name="mla_" + tag, grid=(MLA_HEADS, t // tq),
            out_shape=(jax.ShapeDtypeStruct((t, MLA_HEADS * MLA_D_V), F32), stat, stat),
            in_specs=[pl.BlockSpec((tq, hw), lambda h, i: (i, h)),
                      pl.BlockSpec((tq, MLA_PAD), lambda h, i: (i, 0)),
                      pl.BlockSpec((tq, MLA_PAD), lambda h, i: (i, 0)),
                      pl.BlockSpec((MLA_PAD, MLA_PAD), lambda h, i: (0, 0)),
                      pl.BlockSpec((hw, t), lambda h, i: (h, 0)),
                      pl.BlockSpec((t, MLA_D_V), lambda h, i: (0, h))],
            out_specs=(pl.BlockSpec((tq, MLA_D_V), lambda h, i: (i, h)), stat_spec, stat_spec),
            operands=(q, cos, sin_signed, perm, k_t, v), semantics=("parallel", "arbitrary"), riders=riders)
        return (o, m.reshape(MLA_HEADS, 1, t), linv.reshape(MLA_HEADS, 1, t), *gathered)

    def bwd_call(q, k, k_t, v, cos, sin_signed, perm, o, m, linv, do, riders):
        t = q.shape[0]
        nctx = t - nx_rows

        def body(q_ref, cos_ref, sin_ref, perm_ref, k_ref, kt_ref, v_ref, o_ref, m_ref, linv_ref, do_ref,
                 dq_ref, dkn_ref, dv_ref, dkr_ref):
            i = pl.program_id(1)
            cos_t, sin_t, perm_t = cos_ref[...], sin_ref[...], perm_ref[...]
            qb = _attn_query(q_ref[...], cos_t, sin_t, perm_t)
            do_t = do_ref[...]
            dob = do_t.astype(BF16)
            delta = jnp.sum((do_t * o_ref[...]).T, axis=0, keepdims=True)
            m_t, linv_t = m_ref[...], linv_ref[...]

            def grads(k_part, kt_part, v_part):
                e = jnp.exp((_raw_dot(k_part, qb, "nt") - m_t) * MLA_SCALE)
                p_t = e * linv_t
                dp_t = _raw_dot(v_part, dob, "nt")
                ds_t = (p_t * (dp_t - delta) * MLA_SCALE).astype(BF16)
                dq_t = _raw_dot(kt_part, ds_t, "nn")
                dqb = dq_t.T
                g_r = dqb[:, MLA_D_NOPE:]
                dq_r = g_r * cos_t + jnp.dot(g_r * sin_t, perm_t, precision=lax.Precision.HIGHEST,
                                             preferred_element_type=F32)
                dq_ref[...] = jnp.concatenate([dqb[:, :MLA_D_NOPE], dq_r], axis=1)
                return _raw_dot(ds_t, qb, "nn"), _raw_dot(p_t, dob, "nn")

            @pl.when(i == 0)
            def _():
                dkn_ref[...] = jnp.zeros_like(dkn_ref)
                dv_ref[...] = jnp.zeros_like(dv_ref)

            @pl.when(jnp.logical_and(i == 0, pl.program_id(0) == 0))
            def _():
                dkr_ref[...] = jnp.zeros_like(dkr_ref)

            @pl.when(i * tq < nx_rows)
            def _():
                dk, dv = grads(k_ref[...], kt_ref[...], v_ref[...])
                dkn_ref[...] += dk[:, :MLA_D_NOPE]
                dkr_ref[...] += dk[:, MLA_D_NOPE:]
                dv_ref[...] += dv

            @pl.when(i * tq >= nx_rows)
            def _():
                rows = pl.ds(nx_rows, nctx)
                dk, dv = grads(k_ref[rows, :], kt_ref[:, rows], v_ref[rows, :])
                dkn_ref[rows, :] += dk[:, :MLA_D_NOPE]
                dkr_ref[rows, :] += dk[:, MLA_D_NOPE:]
                dv_ref[rows, :] += dv

        stat_spec = pl.BlockSpec((None, 1, tq), lambda h, i: (h, 0, i))
        return _call(
            body, name="mla_bwd_" + tag, grid=(MLA_HEADS, t // tq),
            out_shape=(jax.ShapeDtypeStruct(q.shape, F32),
                       jax.ShapeDtypeStruct((t, MLA_HEADS * MLA_D_NOPE), F32),
                       jax.ShapeDtypeStruct((t, MLA_HEADS * MLA_D_V), F32),
                       jax.ShapeDtypeStruct((t, MLA_PAD), F32)),
            in_specs=[pl.BlockSpec((tq, hw), lambda h, i: (i, h)),
                      pl.BlockSpec((tq, MLA_PAD), lambda h, i: (i, 0)),
                      pl.BlockSpec((tq, MLA_PAD), lambda h, i: (i, 0)),
                      pl.BlockSpec((MLA_PAD, MLA_PAD), lambda h, i: (0, 0)),
                      pl.BlockSpec((t, hw), lambda h, i: (0, h)),
                      pl.BlockSpec((hw, t), lambda h, i: (h, 0)),
                      pl.BlockSpec((t, MLA_D_V), lambda h, i: (0, h)),
                      pl.BlockSpec((tq, MLA_D_V), lambda h, i: (i, h)),
                      stat_spec, stat_spec,
                      pl.BlockSpec((tq, MLA_D_V), lambda h, i: (i, h))],
            out_specs=(pl.BlockSpec((tq, hw), lambda h, i: (i, h)),
                       pl.BlockSpec((t, MLA_D_NOPE), lambda h, i: (0, h)),
                       pl.BlockSpec((t, MLA_D_V), lambda h, i: (0, h)),
                       pl.BlockSpec((t, MLA_PAD), lambda h, i: (0, 0))),
            operands=(q, cos, sin_signed, perm, k, k_t, v, o, m, linv, do),
            semantics=("arbitrary", "arbitrary"), riders=riders)

    def impl(q, kv, kr, cos, sin_signed, perm, *shards):
        k, k_t, v = operands(kv, kr)
        o, m, linv, *gathered = fwd_call(q, k_t, v, cos, sin_signed, perm,
                                         _Riders(gathers=[s.astype(BF16) for s in shards]))
        return (o, *gathered), (q, k, k_t, v, cos, sin_signed, perm, o, m, linv)

    @jax.custom_vjp
    def attn(q, kv, kr, cos, sin_signed, perm, *shards):
        return impl(q, kv, kr, cos, sin_signed, perm, *shards)[0]

    def fwd(q, kv, kr, cos, sin_signed, perm, *shards):
        return impl(q, kv, kr, cos, sin_signed, perm, *shards)

    def bwd(res, cts):
        q, k, k_t, v, cos, sin_signed, perm, o, m, linv = res
        dq, dkn, dv, dkr, *exchanged = bwd_call(q, k, k_t, v, cos, sin_signed, perm, o, m, linv, cts[0],
                                                _Riders(exchanges=list(cts[1:])))
        t = q.shape[0]
        dkv = jnp.concatenate([dkn.reshape(t, MLA_HEADS, MLA_D_NOPE), dv.reshape(t, MLA_HEADS, MLA_D_V)],
                              axis=2).reshape(t, -1)
        return (dq, dkv, dkr, jnp.zeros_like(cos), jnp.zeros_like(sin_signed), jnp.zeros_like(perm),
                *_sum_exchanged(exchanged, "mla_" + tag))

    attn.defvjp(fwd, bwd)
    return attn


def _adamw_call(w, g, m, v, name):
    rows, cols = w.shape
    tr = _div(rows, max(SUBLANE, (1 << 18) // cols), SUBLANE) if rows % SUBLANE == 0 else rows

    def body(w_ref, g_ref, m_ref, v_ref, d_ref, nm_ref, nv_ref):
        gg = g_ref[...]
        nm = ADAM_B1 * m_ref[...] + (1.0 - ADAM_B1) * gg
        nv = ADAM_B2 * v_ref[...] + (1.0 - ADAM_B2) * jnp.square(gg)
        m_hat = nm / (1.0 - ADAM_B1 ** ADAM_STEP)
        v_hat = nv / (1.0 - ADAM_B2 ** ADAM_STEP)
        d_ref[...] = -ADAM_LR * (m_hat / (jnp.sqrt(v_hat) + ADAM_EPS) + ADAM_WD * w_ref[...])
        nm_ref[...] = nm
        nv_ref[...] = nv

    spec = pl.BlockSpec((tr, cols), lambda i: (i, 0))
    return pl.pallas_call(
        body, name=name, grid=(rows // tr,),
        out_shape=tuple(jax.ShapeDtypeStruct((rows, cols), F32) for _ in range(3)),
        in_specs=[spec] * 4, out_specs=(spec,) * 3,
        compiler_params=_cparams(("parallel",)),
    )(w, g, m, v)


def _adamw(w, g, m, v, name):
    shape = w.shape
    if w.ndim >= 2 and shape[-1] % LANE == 0:
        two = (-1, shape[-1])
    elif w.size % LANE == 0:
        two = (-1, LANE)
    else:
        two = (1, w.size)
    outs = _adamw_call(w.reshape(two), g.reshape(two), m.reshape(two), v.reshape(two), name)
    return tuple(o.reshape(shape) for o in outs)


def _ret_tables(nx, nctx, dk):
    inv = RET_ROPE_BASE ** (-jnp.linspace(0.0, 1.0, dk // 2, dtype=F32))
    ang = jnp.arange(nx, dtype=F32)[:, None] * inv[None, :]
    cos = jnp.concatenate([jnp.cos(ang), jnp.ones((nctx, dk // 2), F32)], axis=0)
    sin = jnp.concatenate([jnp.sin(ang), jnp.zeros((nctx, dk // 2), F32)], axis=0)
    return cos, sin


def _mla_tables(nx, nctx):
    quarter = MLA_D_ROPE // 4
    inv = AXIAL_ROPE_BASE ** (-jnp.arange(quarter, dtype=F32) * 2.0 / (MLA_D_ROPE // 2))
    tok = jnp.arange(nx)
    r_ang = (tok // GRID_W).astype(F32)[:, None] * inv[None, :]
    c_ang = (tok % GRID_W).astype(F32)[:, None] * inv[None, :]
    pad = MLA_PAD - MLA_D_ROPE
    cos = jnp.concatenate([jnp.cos(r_ang), jnp.cos(r_ang), jnp.cos(c_ang), jnp.cos(c_ang),
                           jnp.ones((nx, pad), F32)], axis=1)
    sin = jnp.concatenate([-jnp.sin(r_ang), jnp.sin(r_ang), -jnp.sin(c_ang), jnp.sin(c_ang),
                           jnp.zeros((nx, pad), F32)], axis=1)
    cos = jnp.concatenate([cos, jnp.ones((nctx, MLA_PAD), F32)], axis=0)
    sin = jnp.concatenate([sin, jnp.zeros((nctx, MLA_PAD), F32)], axis=0)
    lane = jnp.arange(MLA_PAD)
    partner = jnp.where(lane % (2 * quarter) < quarter, lane + quarter, lane - quarter)
    perm = ((lane[:, None] == partner[None, :]) & (lane[None, :] < MLA_D_ROPE)).astype(F32)
    return cos, sin, perm


def _columns(gathered):
    return gathered


def _rows(gathered):
    return gathered.reshape(1, gathered.shape[0] * gathered.shape[1], gathered.shape[2])


def _unshard_vec(gathered):
    return jnp.moveaxis(gathered, 0, 1).reshape(gathered.shape[1], -1)


def _pad_rows(a, rows):
    return jnp.concatenate([a, jnp.zeros((rows - a.shape[0],) + a.shape[1:], a.dtype)], axis=0)


def _loss_fn(weights, x, c_all, ctx, tgt, me):
    nx, d_model = x.shape
    nctx = ctx.shape[0]
    dk = d_model // RET_HEADS
    ret_cos, ret_sin = _ret_tables(nx, nctx, dk)
    mla_cos, mla_sin, mla_perm = _mla_tables(nx, nctx)

    rw = functools.partial(_make_rowwise, nx_rows=nx)

    n_ln = weights["ln_g"].size
    small = jnp.concatenate([weights["ln_g"].reshape(-1), weights["ln_b"].reshape(-1),
                             weights["mla_g_q"].reshape(-1), weights["mla_g_kv"].reshape(-1)])
    n_small = small.size
    small = _pad_rows(small.reshape(-1, 1), -(-n_small // LANE) * LANE).reshape(-1, LANE)
    small = _make_gather("small", F32)(small).reshape(N_DEV, -1)
    ln_g = _unshard_vec(small[:, :n_ln].reshape(N_DEV, DEPTH * 2, -1)).reshape(DEPTH, 2, d_model)
    ln_b = _unshard_vec(small[:, n_ln:2 * n_ln].reshape(N_DEV, DEPTH * 2, -1)).reshape(DEPTH, 2, d_model)
    n_g = weights["mla_g_q"].size
    g_q = _unshard_vec(small[:, 2 * n_ln:2 * n_ln + n_g].reshape(N_DEV, DEPTH // 2, -1))
    g_kv = _unshard_vec(small[:, 2 * n_ln + n_g:2 * n_ln + 2 * n_g].reshape(N_DEV, DEPTH // 2, -1))

    n_b = weights["ada_b"].size
    n_dec = weights["ret_decay_logit"].size
    repl = jnp.concatenate([weights["c_ctx"].reshape(-1), weights["ada_b"].reshape(-1),
                            weights["ret_decay_logit"].reshape(-1)])
    n_repl = repl.size
    repl = _pad_rows(repl.reshape(-1, 1), -(-n_repl // LANE) * LANE).reshape(-1, LANE)
    repl = _make_replicated("repl")(repl).reshape(-1)
    c_ctx = repl[:d_model]
    ada_b = repl[d_model:d_model + n_b].reshape(DEPTH, 6 * d_model)
    decay = repl[d_model + n_b:d_model + n_b + n_dec].reshape(weights["ret_decay_logit"].shape)

    cond_rows = 2 * SUBLANE
    cond = _pad_rows(jnp.concatenate([c_all, c_ctx[None, :]], axis=0), cond_rows)
    s_cond = _make_rowwise_plain_silu(cond)
    mods = [_make_mm("ada%d" % i)(s_cond, weights["ada_w"][i][None])[0] for i in range(DEPTH)]
    mod = jnp.concatenate(mods, axis=0)
    mod = _make_gather("mod", F32)(mod)
    mod = jnp.moveaxis(mod.reshape(N_DEV, DEPTH, cond_rows, -1), 0, 2).reshape(DEPTH, cond_rows, -1)
    mod = mod + ada_b[:, None, :]
    mine = lax.dynamic_slice_in_dim(mod, me, 1, axis=1)
    mod = jnp.concatenate([mine, mod[:, N_DEV:N_DEV + 1]], axis=1)
    mod = mod.reshape(DEPTH, 2, 6, 1, d_model)

    def grp(i, j):
        return mod[i, :, j]

    h = jnp.concatenate([x, ctx], axis=0)
    ahead = {"ret_qkv": _make_gather("ret_qkv", BF16)(weights["ret_w_qkv"][0])}
    for i in range(DEPTH):
        j = i // 2
        retention_layer, last = i % 2 == 0, i + 1 == DEPTH
        (u,) = rw(_f_modulate, ("row", "grp", "grp"), (True, True, True), (d_model,), "mod%d" % i)(
            h, grp(i, 1), grp(i, 0))
        if retention_layer:
            if "ret_g" in ahead:
                (qkv,) = _make_mm("ret_qkv%d" % j)(u, _columns(ahead.pop("ret_qkv")))
                (gates,) = _make_mm("ret_g%d" % j)(u, _columns(ahead.pop("ret_g")))
                w_o = ahead.pop("ret_o")
            else:
                qkv, w_g = _make_mm("ret_qkv%d" % j, 1)(u, _columns(ahead.pop("ret_qkv")), weights["ret_w_g"][j])
                gates, w_o = _make_mm("ret_g%d" % j, 1)(u, _columns(w_g), weights["ret_w_o"][j])
            lg = jax.nn.log_sigmoid(decay[j]).reshape(2 * RET_HEADS, 1, 1)
            lgb = jnp.broadcast_to(lg, (2 * RET_HEADS, SUBLANE, LANE))
            o, w_in, w_out = _make_retention("l%d" % j, nx)(
                qkv, ret_cos, ret_sin, lgb, weights["ffn_w_in"][i], weights["ffn_w_out"][i])
            (comb,) = rw(_f_ret_combine, ("row", "pair"), (True, True), (o.shape[2],),
                         "ret_comb%d" % j, tile=WIDE_ROW_TILE)(gates, o)
            nxt = [] if last else [weights[n][j] for n in MLA_MATRICES]
            y, *gathered = _make_mm("ret_o%d" % j, len(nxt))(comb, _rows(w_o), *nxt)
            if nxt:
                ahead["mla"] = gathered
        else:
            nxt = [] if last else [weights[n][j + 1] for n in ("ret_w_qkv", "ret_w_g", "ret_w_o")]
            y, *gathered = _mla_mixer(ahead.pop("mla"), j, u, g_q[j], g_kv[j], mla_cos, mla_sin, mla_perm, rw, nx, nxt)
            if nxt:
                ahead["ret_qkv"], ahead["ret_g"], ahead["ret_o"] = gathered
            w_in, w_out = ahead.pop("ffn_in"), ahead.pop("ffn_out")
        h1, u2 = rw(_f_ln_res_mod, ("row", "row", "grp", "par", "par", "grp", "grp"), (True,) * 7,
                    (d_model, d_model), "ln_a%d" % i)(
            h, y, grp(i, 2), ln_g[i, 0][None], ln_b[i, 0][None], grp(i, 4), grp(i, 3))
        if retention_layer and not last:
            ab, ahead["ffn_in"] = _make_mm("ffn_in%d" % i, 1)(u2, _columns(w_in), weights["ffn_w_in"][i + 1])
        else:
            (ab,) = _make_mm("ffn_in%d" % i)(u2, _columns(w_in))
        (act,) = rw(_f_swiglu, ("row",), (True,), (ab.shape[1] // 2,), "swiglu%d" % i, tile=WIDE_ROW_TILE)(ab)
        if retention_layer and not last:
            f, ahead["ffn_out"] = _make_mm("ffn_out%d" % i, 1)(act, _rows(w_out), weights["ffn_w_out"][i + 1])
        else:
            (f,) = _make_mm("ffn_out%d" % i)(act, _rows(w_out))
        (h,) = rw(_f_ln_res, ("row", "row", "grp", "par", "par"), (True,) * 5, (d_model,), "ln_f%d" % i)(
            h1, f, grp(i, 5), ln_g[i, 1][None], ln_b[i, 1][None])

    (rows,) = rw(_f_loss, ("row", "row"), (True, False), (LANE,), "loss")(h[:nx], tgt)
    return jnp.sum(rows[:, 0])


def _make_rowwise_plain_silu(cond):
    def body(c_ref, o_ref):
        o_ref[...] = _silu(c_ref[...])

    def call(c):
        return pl.pallas_call(body, name="silu_cond", out_shape=jax.ShapeDtypeStruct(c.shape, F32))(c)

    def bwd_body(c_ref, g_ref, o_ref):
        _, vjp = jax.vjp(_silu, c_ref[...])
        o_ref[...] = vjp(g_ref[...])[0]

    @jax.custom_vjp
    def op(c):
        return call(c)

    def fwd(c):
        return call(c), c

    def bwd(c, g):
        return (pl.pallas_call(bwd_body, name="silu_cond_bwd", out_shape=jax.ShapeDtypeStruct(c.shape, F32))(c, g),)

    op.defvjp(fwd, bwd)
    return op(cond)


MLA_MATRICES = ("mla_w_dq", "mla_w_uq", "mla_w_dkv", "mla_w_ukv", "mla_w_o")


def _mla_mixer(gathered, j, u, g_q, g_kv, cos, sin_signed, perm, rw, nx, ride):
    heads, dn, dr = MLA_HEADS, MLA_D_NOPE, MLA_D_ROPE
    w_dq, w_uq, w_dkv, w_ukv, w_o = gathered
    w_dq, w_dkv, w_ukv, w_o = _rows(w_dq), _rows(w_dkv), _columns(w_ukv), _rows(w_o)

    q_lora = w_uq.shape[1]
    w_uq = jnp.moveaxis(w_uq, 0, 1).reshape(q_lora, heads, dn + dr)
    w_uq = jnp.concatenate([w_uq, jnp.zeros((q_lora, heads, MLA_PAD - dr), w_uq.dtype)], axis=2)
    w_uq = w_uq.reshape(1, q_lora, heads * (dn + MLA_PAD))
    w_dkv = jnp.concatenate([w_dkv, jnp.zeros(w_dkv.shape[:2] + (MLA_PAD - dr,), w_dkv.dtype)], axis=2)

    (cq,) = _make_mm("mla_dq%d" % j)(u, w_dq)
    (cqn,) = rw(_f_rms, ("row", "par"), (True, True), (cq.shape[1],), "rms_q%d" % j)(cq, g_q[None])
    (q,) = _make_mm("mla_uq%d" % j)(cqn, w_uq)
    (ckv,) = _make_mm("mla_dkv%d" % j)(u, w_dkv)
    lora = ckv.shape[1] - MLA_PAD
    c_kv, kr = rw(_f_kv_latent, ("row", "par", "row", "row", "par"), (True, True, False, False, False),
                  (lora, MLA_PAD), "kv_lat%d" % j)(ckv, g_kv[None], cos, sin_signed, perm)
    (kv,) = _make_mm("mla_ukv%d" % j)(c_kv, w_ukv)
    o, *rode = _make_attention("l%d" % j, nx)(q, kv, kr, cos, sin_signed, perm, *ride)
    return (_make_mm("mla_o%d" % j)(o, w_o)[0], *rode)


WEIGHT_NAMES = ("c_ctx", "ada_w", "ada_b", "ln_g", "ln_b", "ret_w_qkv", "ret_w_g", "ret_decay_logit", "ret_w_o",
                "mla_w_dq", "mla_g_q", "mla_w_uq", "mla_w_dkv", "mla_g_kv", "mla_w_ukv", "mla_w_o",
                "ffn_w_in", "ffn_w_out")


def kernel(x, c, ctx, c_ctx, ada_w, ada_b, ln_g, ln_b, ret_w_qkv, ret_w_g, ret_decay_logit, ret_w_o, mla_w_dq, mla_g_q, mla_w_uq, mla_w_dkv, mla_g_kv, mla_w_ukv, mla_w_o, ffn_w_in, ffn_w_out, loss_target, m_c_ctx, m_ada_w, m_ada_b, m_ln_g, m_ln_b, m_ret_w_qkv, m_ret_w_g, m_ret_decay_logit, m_ret_w_o, m_mla_w_dq, m_mla_g_q, m_mla_w_uq, m_mla_w_dkv, m_mla_g_kv, m_mla_w_ukv, m_mla_w_o, m_ffn_w_in, m_ffn_w_out, v_c_ctx, v_ada_w, v_ada_b, v_ln_g, v_ln_b, v_ret_w_qkv, v_ret_w_g, v_ret_decay_logit, v_ret_w_o, v_mla_w_dq, v_mla_g_q, v_mla_w_uq, v_mla_w_dkv, v_mla_g_kv, v_mla_w_ukv, v_mla_w_o, v_ffn_w_in, v_ffn_w_out):
    given = dict(locals())
    weights = {n: given[n] for n in WEIGHT_NAMES}
    me = _my_index()
    c_all = _all_gather_call(c, "ag_cond").reshape(N_DEV, -1)

    def loss_of(wts, x2):
        return _loss_fn(wts, x2, c_all, ctx[0], loss_target[0], me)

    loss, (grad_w, grad_x) = jax.value_and_grad(loss_of, argnums=(0, 1))(weights, x[0])
    loss = lax.psum(loss, AXES)
    delta, new_m, new_v = {}, {}, {}
    for n in WEIGHT_NAMES:
        delta[n], new_m[n], new_v[n] = _adamw(weights[n], grad_w[n], given["m_" + n], given["v_" + n], "adamw_" + n)
    return (loss, grad_x[None], *[grad_w[n] for n in WEIGHT_NAMES], *[delta[n] for n in WEIGHT_NAMES],
            *[new_m[n] for n in WEIGHT_NAMES], *[new_v[n] for n in WEIGHT_NAMES])
```

```python
import functools
import math

import jax
import jax.numpy as jnp
from jax import lax
from jax.experimental import pallas as pl
from jax.experimental.pallas import tpu as pltpu

F32 = jnp.float32
BF16 = jnp.bfloat16

AXES = ("x", "y", "c")
N_DEV = 8
MESH_IDS = pl.DeviceIdType.MESH

DEPTH = 4
GRID_W = 64
RET_HEADS = 8
RET_CHUNK = 128
RET_ROPE_BASE = 10000.0
GN_EPS = 1e-6
MLA_HEADS = 16
MLA_D_NOPE = 128
MLA_D_ROPE = 64
MLA_D_V = 128
AXIAL_ROPE_BASE = 10000.0
RMS_EPS = 1e-6
LN_EPS = 1e-5
DEEPNORM_ALPHA = (2 * DEPTH) ** 0.25

ADAM_LR = 0.001
ADAM_B1 = 0.9
ADAM_B2 = 0.999
ADAM_EPS = 1e-08
ADAM_WD = 0.01
ADAM_STEP = 10

LANE = 128
SUBLANE = 8
VMEM_LIMIT = 56 * 1024 * 1024

ROW_TILE = 256
WIDE_ROW_TILE = 64
MM_ROWS = 544
MLA_PAD = 128


def _div(n, cap, mult):
    best = None
    for d in range(mult, min(n, cap) + 1, mult):
        if n % d == 0:
            best = d
    return n if best is None else best


def _cparams(sem=None):
    kw = dict(vmem_limit_bytes=VMEM_LIMIT)
    if sem is not None:
        kw["dimension_semantics"] = sem
    return pltpu.CompilerParams(**kw)


_DN = {"nn": (((1,), (0,)), ((), ())), "nt": (((1,), (1,)), ((), ())), "tn": (((0,), (0,)), ((), ()))}


def _raw_dot(a, b, kind):
    return lax.dot_general(a.astype(BF16), b.astype(BF16), _DN[kind], preferred_element_type=F32)


@functools.partial(jax.custom_vjp, nondiff_argnums=(2,))
def _dot(a, b, kind):
    return _raw_dot(a, b, kind)


def _dot_fwd(a, b, kind):
    return _raw_dot(a, b, kind), (a, b)


def _dot_bwd(kind, res, g):
    a, b = res
    if kind == "nn":
        return _raw_dot(g, b, "nt"), _raw_dot(a, g, "tn")
    if kind == "nt":
        return _raw_dot(g, b, "nn"), _raw_dot(g, a, "tn")
    return _raw_dot(b, g, "nt"), _raw_dot(a, g, "nn")


_dot.defvjp(_dot_fwd, _dot_bwd)


def _sigmoid(x):
    return 1.0 / (1.0 + jnp.exp(-x))


def _silu(x):
    return x * _sigmoid(x)


def _my_index():
    return 4 * lax.axis_index("x") + 2 * lax.axis_index("y") + lax.axis_index("c")


def _all_gather_call(x, name):
    def body(x_ref, out_ref, send_sems, recv_sems, local_sem):
        x_, y_, c_ = lax.axis_index("x"), lax.axis_index("y"), lax.axis_index("c")
        me, sibling = (x_, y_, c_), (x_, y_, 1 - c_)
        chips = [(1 - x_, y_), (x_, 1 - y_), (1 - x_, 1 - y_)]

        def slot(px, py, pc):
            return out_ref.at[4 * px + 2 * py + pc]

        def copy(k, block, to, src=None):
            return pltpu.make_async_remote_copy(
                src_ref=slot(*block) if src is None else src, dst_ref=slot(*block),
                send_sem=send_sems.at[k], recv_sem=recv_sems.at[k],
                device_id=to, device_id_type=MESH_IDS)

        mine = pltpu.make_async_copy(x_ref, slot(*me), local_sem)
        mine.start()
        first = [copy(0, me, sibling, src=x_ref)]
        first += [copy(1 + j, me, (*chip, c_), src=x_ref) for j, chip in enumerate(chips)]
        for cp in first:
            cp.start()
        passed = [copy(4 + j, (*chip, c_), sibling) for j, chip in enumerate(chips)]
        for j, chip in enumerate(chips):
            copy(1 + j, (*chip, c_), me).wait_recv()
            passed[j].start()
        copy(0, sibling, me).wait_recv()
        for j, chip in enumerate(chips):
            copy(4 + j, (*chip, 1 - c_), me).wait_recv()
        for cp in first + passed:
            cp.wait_send()
        mine.wait()

    return pl.pallas_call(
        body, name=name,
        out_shape=jax.ShapeDtypeStruct((N_DEV,) + x.shape, x.dtype),
        in_specs=[pl.BlockSpec(memory_space=pl.ANY)],
        out_specs=pl.BlockSpec(memory_space=pl.ANY),
        scratch_shapes=[pltpu.SemaphoreType.DMA((7,)), pltpu.SemaphoreType.DMA((7,)), pltpu.SemaphoreType.DMA],
    )(x)


def _all_to_all_call(g, name):
    def body(g_ref, out_ref, send_sems, recv_sems, local_sem):
        x_, y_, c_ = lax.axis_index("x"), lax.axis_index("y"), lax.axis_index("c")
        my = 4 * x_ + 2 * y_ + c_
        peers = []
        for k in range(1, N_DEV):
            px = 1 - x_ if (k >> 2) & 1 else x_
            py = 1 - y_ if (k >> 1) & 1 else y_
            pc = 1 - c_ if k & 1 else c_
            peers.append((px, py, pc))

        def copy(k, peer):
            pid = 4 * peer[0] + 2 * peer[1] + peer[2]
            return pltpu.make_async_remote_copy(
                src_ref=g_ref.at[pid], dst_ref=out_ref.at[my],
                send_sem=send_sems.at[k], recv_sem=recv_sems.at[k],
                device_id=peer, device_id_type=MESH_IDS)

        def arrival(k, peer):
            pid = 4 * peer[0] + 2 * peer[1] + peer[2]
            return pltpu.make_async_remote_copy(
                src_ref=g_ref.at[pid], dst_ref=out_ref.at[pid],
                send_sem=send_sems.at[k], recv_sem=recv_sems.at[k],
                device_id=peer, device_id_type=MESH_IDS)

        mine = pltpu.make_async_copy(g_ref.at[my], out_ref.at[my], local_sem)
        mine.start()
        sends = [copy(k, peer) for k, peer in enumerate(peers)]
        for cp in sends:
            cp.start()
        for k, peer in enumerate(peers):
            arrival(k, peer).wait_recv()
        for cp in sends:
            cp.wait_send()
        mine.wait()

    return pl.pallas_call(
        body, name=name,
        out_shape=jax.ShapeDtypeStruct(g.shape, g.dtype),
        in_specs=[pl.BlockSpec(memory_space=pl.ANY)],
        out_specs=pl.BlockSpec(memory_space=pl.ANY),
        scratch_shapes=[pltpu.SemaphoreType.DMA((7,)), pltpu.SemaphoreType.DMA((7,)), pltpu.SemaphoreType.DMA],
    )(g)


def _sum_slots_call(g, name):
    slots, rows, cols = g.shape
    tr = _div(rows, max(SUBLANE, (1 << 19) // cols), 16) if rows % 16 == 0 else rows

    def body(g_ref, o_ref):
        acc = g_ref[0].astype(F32)
        for s in range(1, slots):
            acc = acc + g_ref[s].astype(F32)
        o_ref[...] = acc

    return pl.pallas_call(
        body, name=name, grid=(rows // tr,),
        out_shape=jax.ShapeDtypeStruct((rows, cols), F32),
        in_specs=[pl.BlockSpec((slots, tr, cols), lambda i: (0, i, 0))],
        out_specs=pl.BlockSpec((tr, cols), lambda i: (i, 0)),
        compiler_params=_cparams(("parallel",)),
    )(g)


def _pair_sum_call(g, half, name):
    _, rows, cols = g.shape
    tr = _div(rows, max(SUBLANE, (1 << 20) // cols), 16) if rows % 16 == 0 else rows
    core = lax.axis_index("c").astype(jnp.int32).reshape(1)

    def body(core_ref, g_ref, h_ref, o_ref):
        del core_ref
        o_ref[...] = (g_ref[...].astype(F32) + h_ref[...].astype(F32)).astype(o_ref.dtype)

    return pl.pallas_call(
        body, name=name,
        grid_spec=pltpu.PrefetchScalarGridSpec(
            num_scalar_prefetch=1, grid=(N_DEV // 2, rows // tr),
            in_specs=[pl.BlockSpec((None, tr, cols), lambda k, i, core_ref: (2 * k + core_ref[0], i, 0)),
                      pl.BlockSpec((None, tr, cols), lambda k, i, core_ref: (k, i, 0))],
            out_specs=pl.BlockSpec((None, tr, cols), lambda k, i, core_ref: (k, i, 0))),
        out_shape=jax.ShapeDtypeStruct(half.shape, half.dtype),
        compiler_params=_cparams(("parallel", "parallel")),
    )(core, g, half)


def _make_gather(tag, travel_dtype):
    @jax.custom_vjp
    def gather(w):
        return _all_gather_call(w.astype(travel_dtype), "ag_" + tag)

    def fwd(w):
        return gather(w), None

    def bwd(_, g):
        got = _all_to_all_call(g.astype(travel_dtype), "a2a_" + tag)
        return (_sum_slots_call(got, "sum_" + tag),)

    gather.defvjp(fwd, bwd)
    return gather


def _make_replicated(tag):
    @jax.custom_vjp
    def rep(p):
        return p

    def fwd(p):
        return p, None

    def bwd(_, g):
        return (_sum_slots_call(_all_gather_call(g, "ag_" + tag), "sum_" + tag),)

    rep.defvjp(fwd, bwd)
    return rep


class _Riders:
    def __init__(self, gathers=(), exchanges=(), sibling_swaps=(), chip_exchanges=()):
        self.gathers = list(gathers)
        self.exchanges = list(exchanges)
        self.sibling_swaps = list(sibling_swaps)
        self.chip_exchanges = list(chip_exchanges)
        self.n = len(self.operands())

    def operands(self):
        return self.gathers + self.exchanges + self.sibling_swaps + self.chip_exchanges

    def out_shape(self):
        return ([jax.ShapeDtypeStruct((N_DEV,) + g.shape, g.dtype) for g in self.gathers]
                + [jax.ShapeDtypeStruct(e.shape, e.dtype) for e in self.exchanges]
                + [jax.ShapeDtypeStruct((N_DEV // 2,) + s.shape[1:], s.dtype) for s in self.sibling_swaps]
                + [jax.ShapeDtypeStruct(p.shape, p.dtype) for p in self.chip_exchanges])

    def scratch(self):
        return [pltpu.SemaphoreType.DMA((7 * self.n,)), pltpu.SemaphoreType.DMA((7 * self.n,)),
                pltpu.SemaphoreType.DMA((self.n,))]

    def _gather_phase(self, phase, r, x_ref, out_ref, send_sems, recv_sems, local_sem):
        x_, y_, c_ = lax.axis_index("x"), lax.axis_index("y"), lax.axis_index("c")
        me, sibling = (x_, y_, c_), (x_, y_, 1 - c_)
        chips = [(1 - x_, y_), (x_, 1 - y_), (1 - x_, 1 - y_)]

        def slot(px, py, pc):
            return out_ref.at[4 * px + 2 * py + pc]

        def copy(k, block, to, src=None):
            return pltpu.make_async_remote_copy(
                src_ref=slot(*block) if src is None else src, dst_ref=slot(*block),
                send_sem=send_sems.at[7 * r + k], recv_sem=recv_sems.at[7 * r + k],
                device_id=to, device_id_type=MESH_IDS)

        def mine():
            return pltpu.make_async_copy(x_ref, slot(*me), local_sem.at[r])

        def first():
            return [copy(0, me, sibling, src=x_ref)] + [copy(1 + j, me, (*chip, c_), src=x_ref)
                                                        for j, chip in enumerate(chips)]

        def passed(j):
            return copy(4 + j, (*chips[j], c_), sibling)

        if phase == "start":
            mine().start()
            for cp in first():
                cp.start()
        elif phase == "pass":
            for j, chip in enumerate(chips):
                copy(1 + j, (*chip, c_), me).wait_recv()
                passed(j).start()
        else:
            copy(0, sibling, me).wait_recv()
            for j, chip in enumerate(chips):
                copy(4 + j, (*chip, 1 - c_), me).wait_recv()
            for cp in first() + [passed(j) for j in range(3)]:
                cp.wait_send()
            mine().wait()

    def _exchange_phase(self, phase, r, g_ref, out_ref, send_sems, recv_sems, local_sem):
        x_, y_, c_ = lax.axis_index("x"), lax.axis_index("y"), lax.axis_index("c")
        my = 4 * x_ + 2 * y_ + c_
        peers = []
        for k in range(1, N_DEV):
            peers.append((1 - x_ if (k >> 2) & 1 else x_, 1 - y_ if (k >> 1) & 1 else y_, 1 - c_ if k & 1 else c_))

        def copy(k, peer, arriving):
            pid = 4 * peer[0] + 2 * peer[1] + peer[2]
            return pltpu.make_async_remote_copy(
                src_ref=g_ref.at[pid], dst_ref=out_ref.at[pid if arriving else my],
                send_sem=send_sems.at[7 * r + k], recv_sem=recv_sems.at[7 * r + k],
                device_id=peer, device_id_type=MESH_IDS)

        def mine():
            return pltpu.make_async_copy(g_ref.at[my], out_ref.at[my], local_sem.at[r])

        if phase == "start":
            mine().start()
            for k, peer in enumerate(peers):
                copy(k, peer, False).start()
        elif phase == "finish":
            for k, peer in enumerate(peers):
                copy(k, peer, True).wait_recv()
            for k, peer in enumerate(peers):
                copy(k, peer, False).wait_send()
            mine().wait()

    def _sibling_phase(self, phase, r, g_ref, out_ref, send_sems, recv_sems):
        x_, y_, c_ = lax.axis_index("x"), lax.axis_index("y"), lax.axis_index("c")

        def copy(chip):
            return pltpu.make_async_remote_copy(
                src_ref=g_ref.at[2 * chip + 1 - c_], dst_ref=out_ref.at[chip],
                send_sem=send_sems.at[7 * r + chip], recv_sem=recv_sems.at[7 * r + chip],
                device_id=(x_, y_, 1 - c_), device_id_type=MESH_IDS)

        if phase == "start":
            for chip in range(N_DEV // 2):
                copy(chip).start()
        elif phase == "finish":
            for chip in range(N_DEV // 2):
                copy(chip).wait_recv()
            for chip in range(N_DEV // 2):
                copy(chip).wait_send()

    def _chip_phase(self, phase, r, p_ref, out_ref, send_sems, recv_sems, local_sem):
        x_, y_, c_ = lax.axis_index("x"), lax.axis_index("y"), lax.axis_index("c")
        my = 2 * x_ + y_
        peers = [(1 - x_ if (k >> 1) & 1 else x_, 1 - y_ if k & 1 else y_) for k in range(1, N_DEV // 2)]

        def copy(k, peer, arriving):
            pid = 2 * peer[0] + peer[1]
            return pltpu.make_async_remote_copy(
                src_ref=p_ref.at[pid], dst_ref=out_ref.at[pid if arriving else my],
                send_sem=send_sems.at[7 * r + k], recv_sem=recv_sems.at[7 * r + k],
                device_id=(*peer, c_), device_id_type=MESH_IDS)

        def mine():
            return pltpu.make_async_copy(p_ref.at[my], out_ref.at[my], local_sem.at[r])

        if phase == "start":
            mine().start()
            for k, peer in enumerate(peers):
                copy(k, peer, False).start()
        elif phase == "finish":
            for k, peer in enumerate(peers):
                copy(k, peer, True).wait_recv()
            for k, peer in enumerate(peers):
                copy(k, peer, False).wait_send()
            mine().wait()

    def phase(self, phase, in_refs, out_refs, send_sems, recv_sems, local_sem):
        ng, ne, ns = len(self.gathers), len(self.exchanges), len(self.sibling_swaps)
        for r in range(self.n):
            if r < ng:
                self._gather_phase(phase, r, in_refs[r], out_refs[r], send_sems, recv_sems, local_sem)
            elif r < ng + ne:
                self._exchange_phase(phase, r, in_refs[r], out_refs[r], send_sems, recv_sems, local_sem)
            elif r < ng + ne + ns:
                self._sibling_phase(phase, r, in_refs[r], out_refs[r], send_sems, recv_sems)
            else:
                self._chip_phase(phase, r, in_refs[r], out_refs[r], send_sems, recv_sems, local_sem)


def _call(body, *, name, grid, in_specs, out_specs, out_shape, operands, scratch_shapes=(), semantics, riders=None):
    if riders is None or riders.n == 0:
        return tuple(pl.pallas_call(
            body, name=name, grid=grid, out_shape=tuple(out_shape), in_specs=list(in_specs),
            out_specs=tuple(out_specs), scratch_shapes=list(scratch_shapes), compiler_params=_cparams(semantics),
        )(*operands))

    n_in, n_out, n_scr, rn = len(in_specs), len(out_specs), len(scratch_shapes), riders.n
    steps = math.prod(grid)

    def full_body(*refs):
        ins, r_in = refs[:n_in], refs[n_in:n_in + rn]
        outs, r_out = refs[n_in + rn:n_in + rn + n_out], refs[n_in + rn + n_out:n_in + 2 * rn + n_out]
        scr = refs[n_in + 2 * rn + n_out:n_in + 2 * rn + n_out + n_scr]
        sems = refs[n_in + 2 * rn + n_out + n_scr:]
        step = 0
        for axis, size in enumerate(grid):
            step = step * size + pl.program_id(axis)

        @pl.when(step == 0)
        def _():
            riders.phase("start", r_in, r_out, *sems)

        @pl.when(step == steps - 1 - steps // 8)
        def _():
            riders.phase("pass", r_in, r_out, *sems)

        body(*ins, *outs, *scr)

        @pl.when(step == steps - 1)
        def _():
            riders.phase("finish", r_in, r_out, *sems)

    hbm = pl.BlockSpec(memory_space=pl.ANY)
    return tuple(pl.pallas_call(
        full_body, name=name, grid=grid,
        out_shape=tuple(out_shape) + tuple(riders.out_shape()),
        in_specs=list(in_specs) + [hbm] * rn,
        out_specs=tuple(out_specs) + (hbm,) * rn,
        scratch_shapes=list(scratch_shapes) + riders.scratch(),
        compiler_params=_cparams(("arbitrary",) * len(grid)),
    )(*operands, *riders.operands()))


def _mm_nn_call(a, w3, name, riders=None):
    m, k = a.shape
    nb, _, ns = w3.shape
    tm = _div(m, MM_ROWS, SUBLANE)
    tn = _div(ns, max(LANE, (6 << 20) // (2 * k)), LANE)
    tps = ns // tn

    def body(a_ref, w_ref, o_ref):
        o_ref[...] = _raw_dot(a_ref[...], w_ref[...], "nn")

    return _call(
        body, name=name, grid=(nb * tps, m // tm),
        out_shape=[jax.ShapeDtypeStruct((m, nb * ns), F32)],
        in_specs=[pl.BlockSpec((tm, k), lambda j, i: (i, 0)),
                  pl.BlockSpec((None, k, tn), lambda j, i: (j // tps, 0, j % tps))],
        out_specs=[pl.BlockSpec((tm, tn), lambda j, i: (i, j))],
        operands=(a, w3), semantics=("parallel", "parallel"), riders=riders)


def _mm_nt_call(g, w3, name, riders=None):
    m, n = g.shape
    nb, k, ns = w3.shape
    tm = _div(m, MM_ROWS, 16) if m % 16 == 0 else m
    tk = _div(k, 2048, LANE)
    tn = _div(ns, max(LANE, (6 << 20) // (2 * tk)), LANE)
    tps = ns // tn
    nj = nb * tps
    emit = tk == k

    def body(g_ref, w_ref, o_ref, *rest):
        acc = rest[-1]
        j = pl.program_id(2)
        gb = g_ref[...].astype(BF16)
        if emit:
            rest[0][...] = gb
        p = _raw_dot(gb, w_ref[...], "nt")

        @pl.when(j == 0)
        def _():
            acc[...] = p

        @pl.when(j > 0)
        def _():
            acc[...] += p

        @pl.when(j == nj - 1)
        def _():
            o_ref[...] = acc[...]

    da, *rest = _call(
        body, name=name, grid=(m // tm, k // tk, nj),
        out_shape=[jax.ShapeDtypeStruct((m, k), F32)] + ([jax.ShapeDtypeStruct((m, n), BF16)] if emit else []),
        in_specs=[pl.BlockSpec((tm, tn), lambda i, kk, j: (i, j)),
                  pl.BlockSpec((None, tk, tn), lambda i, kk, j: (j // tps, kk, j % tps))],
        out_specs=[pl.BlockSpec((tm, tk), lambda i, kk, j: (i, kk))]
        + ([pl.BlockSpec((tm, tn), lambda i, kk, j: (i, j))] if emit else []),
        scratch_shapes=[pltpu.VMEM((tm, tk), F32)],
        operands=(g, w3), semantics=("parallel", "arbitrary", "arbitrary"), riders=riders)
    return (da, *rest) if emit else (da, g.astype(BF16), *rest)


def _mm_dw_call(a, g, nb, out_dtype, name, riders=None):
    m, k = a.shape
    n = g.shape[1]
    ns = n // nb
    tn = _div(ns, max(LANE, (13 << 20) // (2 * m)), LANE)
    tk = _div(k, max(LANE, (9 << 20) // (4 * m)), LANE)
    tps = ns // tn

    def body(a_ref, g_ref, o_ref):
        o_ref[...] = _raw_dot(a_ref[...], g_ref[...], "tn").astype(o_ref.dtype)

    return _call(
        body, name=name, grid=(nb * tps, k // tk),
        out_shape=[jax.ShapeDtypeStruct((nb, k, ns), out_dtype)],
        in_specs=[pl.BlockSpec((m, tk), lambda j, kk: (0, kk)),
                  pl.BlockSpec((m, tn), lambda j, kk: (0, j))],
        out_specs=[pl.BlockSpec((None, tk, tn), lambda j, kk: (j // tps, kk, j % tps))],
        operands=(a, g), semantics=("parallel", "parallel"), riders=riders)


def _sum_exchanged(exchanged, tag):
    return [_sum_slots_call(e, "sum_%s_%d" % (tag, k)) for k, e in enumerate(exchanged)]


def _make_mm(tag, n_ride=0):
    def impl(a, w3, *shards):
        return _mm_nn_call(a, w3, "mm_" + tag, _Riders(gathers=[s.astype(BF16) for s in shards]))

    @jax.custom_vjp
    def mm(a, w3, *shards):
        return impl(a, w3, *shards)

    def fwd(a, w3, *shards):
        return impl(a, w3, *shards), (a, w3)

    def bwd(res, cts):
        a, w3 = res
        d_gathered = list(cts[1:])
        da, gb, *halves = _mm_nt_call(cts[0], w3, "mm_da_" + tag, _Riders(sibling_swaps=d_gathered))
        pairs = [_pair_sum_call(d, h, "pair_%s_%d" % (tag, r)) for r, (d, h) in enumerate(zip(d_gathered, halves))]
        dw, *exchanged = _mm_dw_call(a, gb, w3.shape[0], w3.dtype, "mm_dw_" + tag,
                                     _Riders(chip_exchanges=pairs))
        return (da, dw, *_sum_exchanged(exchanged, tag))

    mm.defvjp(fwd, bwd)
    return mm


def _make_rowwise(f, kinds, diff, out_cols, tag, nx_rows, tile=ROW_TILE):
    n_in = len(kinds)
    nxt = nx_rows // tile

    def in_spec(kind, arr):
        if kind == "row":
            return pl.BlockSpec((tile, arr.shape[1]), lambda i: (i, 0))
        if kind == "pair":
            return pl.BlockSpec((arr.shape[0], tile, arr.shape[2]), lambda i: (0, i, 0))
        if kind == "grp":
            return pl.BlockSpec((None, 1, arr.shape[2]), lambda i: (i // nxt, 0, 0))
        return pl.BlockSpec(arr.shape, lambda i: (0,) * arr.ndim)

    def fwd_call(*args):
        t = next(a.shape[0] for a, kd in zip(args, kinds) if kd == "row")

        def body(*refs):
            outs = f(*[r[...] for r in refs[:n_in]])
            for r, o in zip(refs[n_in:], outs):
                r[...] = o

        return pl.pallas_call(
            body, name="rw_" + tag, grid=(t // tile,),
            out_shape=tuple(jax.ShapeDtypeStruct((t, c), F32) for c in out_cols),
            in_specs=[in_spec(kd, a) for kd, a in zip(kinds, args)],
            out_specs=tuple(pl.BlockSpec((tile, c), lambda i: (i, 0)) for c in out_cols),
            compiler_params=_cparams(("parallel",)),
        )(*args)

    def bwd_call(args, cts):
        t = cts[0].shape[0]
        didx = [i for i in range(n_in) if diff[i]]

        def body(*refs):
            i = pl.program_id(0)
            vals = [r[...] for r in refs[:n_in]]
            ct = tuple(r[...] for r in refs[n_in:n_in + len(out_cols)])
            outs = refs[n_in + len(out_cols):]

            def g(*dv):
                full = list(vals)
                for j, v in zip(didx, dv):
                    full[j] = v
                return tuple(f(*full))

            _, vjp = jax.vjp(g, *[vals[j] for j in didx])
            grads = vjp(ct)
            for j, o_ref, d in zip(didx, outs, grads):
                if kinds[j] in ("row", "pair"):
                    o_ref[...] = d
                else:
                    first = (i % nxt == 0) if kinds[j] == "grp" else (i == 0)

                    @pl.when(first)
                    def _(o_ref=o_ref, d=d):
                        o_ref[...] = d

                    @pl.when(jnp.logical_not(first))
                    def _(o_ref=o_ref, d=d):
                        o_ref[...] += d

        def out_spec(j):
            return in_spec(kinds[j], args[j])

        return pl.pallas_call(
            body, name="rw_bwd_" + tag, grid=(t // tile,),
            out_shape=tuple(jax.ShapeDtypeStruct(args[j].shape, F32) for j in didx),
            in_specs=[in_spec(kd, a) for kd, a in zip(kinds, args)]
            + [pl.BlockSpec((tile, c), lambda i: (i, 0)) for c in out_cols],
            out_specs=tuple(out_spec(j) for j in didx),
            compiler_params=_cparams(("arbitrary",)),
        )(*args, *cts)

    @jax.custom_vjp
    def op(*args):
        return fwd_call(*args)

    def fwd(*args):
        return fwd_call(*args), args

    def bwd(args, cts):
        grads = bwd_call(args, cts)
        full = [None] * n_in
        for j, gr in zip([i for i in range(n_in) if diff[i]], grads):
            full[j] = gr
        return tuple(jnp.zeros_like(a) if gfull is None else gfull for a, gfull in zip(args, full))

    op.defvjp(fwd, bwd)
    return op


def _layer_norm(z, g, b):
    mu = jnp.mean(z, axis=-1, keepdims=True)
    var = jnp.mean(jnp.square(z - mu), axis=-1, keepdims=True)
    return (z - mu) * lax.rsqrt(var + LN_EPS) * g + b


def _f_modulate(h, sc, sh):
    return (h * (1.0 + sc) + sh,)


def _f_ln_res(h, y, gate, g, b):
    return (_layer_norm(DEEPNORM_ALPHA * h + gate * y, g, b),)


def _f_ln_res_mod(h, y, gate, g, b, sc, sh):
    h1 = _layer_norm(DEEPNORM_ALPHA * h + gate * y, g, b)
    return h1, h1 * (1.0 + sc) + sh


def _f_swiglu(ab):
    half = ab.shape[1] // 2
    return (_silu(ab[:, :half]) * ab[:, half:],)


def _f_ret_combine(gates, o):
    o_f, o_b = o[0], o[1]
    hv = o_f.shape[1]
    dv = hv // RET_HEADS

    def gn(o):
        parts = []
        for h in range(RET_HEADS):
            oh = o[:, h * dv:(h + 1) * dv]
            mu = jnp.mean(oh, axis=-1, keepdims=True)
            var = jnp.mean(jnp.square(oh - mu), axis=-1, keepdims=True)
            parts.append((oh - mu) * lax.rsqrt(var + GN_EPS))
        return jnp.concatenate(parts, axis=1)

    return (_silu(gates[:, :hv]) * gn(o_f) + _silu(gates[:, hv:]) * gn(o_b),)


def _rms(x, g):
    return x * lax.rsqrt(jnp.mean(jnp.square(x), axis=-1, keepdims=True) + RMS_EPS) * g


def _f_rms(x, g):
    return (_rms(x, g),)


def _rot(x, cos, sin_signed, perm):
    swapped = jnp.dot(x, perm, precision=lax.Precision.HIGHEST, preferred_element_type=F32)
    return x * cos + swapped * sin_signed


def _f_kv_latent(ckv, g, cos, sin_signed, perm):
    lora = ckv.shape[1] - MLA_PAD
    return _rms(ckv[:, :lora], g), _rot(ckv[:, lora:], cos, sin_signed, perm)


def _f_loss(y, tgt):
    row = 0.5 * jnp.mean(jnp.square(y - tgt), axis=-1, keepdims=True)
    return (jnp.broadcast_to(row, (y.shape[0], LANE)),)


def _ret_step(q, k, v, state, lg, cos, sin, d):
    c, dk = q.shape
    half = dk // 2

    def rope(t):
        t1, t2 = t[:, :half], t[:, half:]
        return jnp.concatenate([t1 * cos - t2 * sin, t1 * sin + t2 * cos], axis=1)

    q = rope(q)
    k = rope(k * (dk ** -0.5))
    sgn = (1 - 2 * d).astype(F32)
    ii = lax.broadcasted_iota(jnp.int32, (c, c), 0).astype(F32)
    jj = lax.broadcasted_iota(jnp.int32, (c, c), 1).astype(F32)
    e = (ii - jj) * sgn
    intra = jnp.where(e >= 0, jnp.exp(lg * jnp.maximum(e, 0.0)), 0.0)
    idx = lax.broadcasted_iota(jnp.int32, (c, 1), 0).astype(F32)
    pos = jnp.where(d == 0, idx, c - 1.0 - idx)
    q_dec = jnp.exp(lg * (pos + 1.0))
    k_dec = jnp.exp(lg * (c - 1.0 - pos))
    c_dec = jnp.exp(lg * float(c))
    scores = _dot(q, k, "nt") * intra
    o = _dot(scores, v, "nn") + _dot(q * q_dec, state, "nn")
    new_state = state * c_dec + _dot(k * k_dec, v, "tn")
    return o, new_state


def _ret_chunk_of(d, p, nxc, nc):
    return (1 - d) * ((p + nxc) % nc) + d * (nc - 1 - p)


def _ret_specs(t, dk, dv, nxc, nc, step_of):
    hq = RET_HEADS * dk // dk
    c = RET_CHUNK

    def chunk(d, h, s):
        return _ret_chunk_of(d, step_of(s), nxc, nc)

    q_spec = pl.BlockSpec((c, dk), lambda d, h, s: (chunk(d, h, s), h))
    k_spec = pl.BlockSpec((c, dk), lambda d, h, s: (chunk(d, h, s), hq + h))
    v_spec = pl.BlockSpec((c, dv), lambda d, h, s: (chunk(d, h, s), (2 * RET_HEADS * dk) // dv + h))
    tab_spec = pl.BlockSpec((c, dk // 2), lambda d, h, s: (chunk(d, h, s), 0))
    lg_spec = pl.BlockSpec((None, SUBLANE, LANE), lambda d, h, s: (d * RET_HEADS + h, 0, 0))
    o_spec = pl.BlockSpec((None, c, dv), lambda d, h, s: (d, chunk(d, h, s), h))
    st_spec = pl.BlockSpec((None, None, None, dk, dv), lambda d, h, s: (d, h, step_of(s), 0, 0))
    return q_spec, k_spec, v_spec, tab_spec, lg_spec, o_spec, st_spec


def _make_retention(tag, nx_rows):
    def dims(qkv):
        t, w = qkv.shape
        dk = w // (4 * RET_HEADS)
        return t, dk, 2 * dk, nx_rows // RET_CHUNK, t // RET_CHUNK

    def fwd_call(qkv, cos, sin, lgb, riders):
        t, dk, dv, nxc, nc = dims(qkv)
        q_spec, k_spec, v_spec, tab_spec, lg_spec, o_spec, st_spec = _ret_specs(t, dk, dv, nxc, nc, lambda s: s)

        def body(q_ref, k_ref, v_ref, cos_ref, sin_ref, lg_ref, o_ref, st_ref, state):
            d = pl.program_id(0)

            @pl.when(pl.program_id(2) == 0)
            def _():
                state[...] = jnp.zeros_like(state)

            st_ref[...] = state[...]
            o, new_state = _ret_step(q_ref[...], k_ref[...], v_ref[...], state[...], lg_ref[0:1, 0:1],
                                     cos_ref[...], sin_ref[...], d)
            o_ref[...] = o
            state[...] = new_state

        return _call(
            body, name="ret_" + tag, grid=(2, RET_HEADS, nc),
            out_shape=(jax.ShapeDtypeStruct((2, t, RET_HEADS * dv), F32),
                       jax.ShapeDtypeStruct((2, RET_HEADS, nc, dk, dv), F32)),
            in_specs=[q_spec, k_spec, v_spec, tab_spec, tab_spec, lg_spec],
            out_specs=(o_spec, st_spec),
            scratch_shapes=[pltpu.VMEM((dk, dv), F32)],
            operands=(qkv, qkv, qkv, cos, sin, lgb),
            semantics=("parallel", "parallel", "arbitrary"), riders=riders)

    def bwd_call(qkv, cos, sin, lgb, states, do, riders):
        t, dk, dv, nxc, nc = dims(qkv)
        q_spec, k_spec, v_spec, tab_spec, lg_spec, o_spec, st_spec = _ret_specs(
            t, dk, dv, nxc, nc, lambda s: nc - 1 - s)
        c = RET_CHUNK

        def chunk(d, h, s):
            return _ret_chunk_of(d, nc - 1 - s, nxc, nc)

        dq_spec = pl.BlockSpec((None, c, dk), lambda d, h, s: (d, chunk(d, h, s), h))
        dv_spec = pl.BlockSpec((None, c, dv), lambda d, h, s: (d, chunk(d, h, s), h))

        def body(q_ref, k_ref, v_ref, cos_ref, sin_ref, lg_ref, st_ref, do_ref,
                 dq_ref, dk_ref, dv_ref, dlg_ref, dstate):
            d = pl.program_id(0)
            s = pl.program_id(2)

            @pl.when(s == 0)
            def _():
                dstate[...] = jnp.zeros_like(dstate)

            cos_t, sin_t = cos_ref[...], sin_ref[...]

            def step(q, k, v, state, lg):
                return _ret_step(q, k, v, state, lg, cos_t, sin_t, d)

            _, vjp = jax.vjp(step, q_ref[...], k_ref[...], v_ref[...], st_ref[...], lg_ref[0:1, 0:1])
            dq, dkk, dvv, dst, dlg = vjp((do_ref[...], dstate[...]))
            dq_ref[...] = dq
            dk_ref[...] = dkk
            dv_ref[...] = dvv
            dstate[...] = dst
            corner = jnp.logical_and(lax.broadcasted_iota(jnp.int32, (SUBLANE, LANE), 0) == 0,
                                     lax.broadcasted_iota(jnp.int32, (SUBLANE, LANE), 1) == 0)
            dlg_full = jnp.where(corner, dlg, 0.0)

            @pl.when(s == 0)
            def _():
                dlg_ref[...] = dlg_full

            @pl.when(s > 0)
            def _():
                dlg_ref[...] += dlg_full

        return _call(
            body, name="ret_bwd_" + tag, grid=(2, RET_HEADS, nc),
            out_shape=(jax.ShapeDtypeStruct((2, t, RET_HEADS * dk), F32),
                       jax.ShapeDtypeStruct((2, t, RET_HEADS * dk), F32),
                       jax.ShapeDtypeStruct((2, t, RET_HEADS * dv), F32),
                       jax.ShapeDtypeStruct(lgb.shape, F32)),
            in_specs=[q_spec, k_spec, v_spec, tab_spec, tab_spec, lg_spec, st_spec, o_spec],
            out_specs=(dq_spec, dq_spec, dv_spec, lg_spec),
            scratch_shapes=[pltpu.VMEM((dk, dv), F32)],
            operands=(qkv, qkv, qkv, cos, sin, lgb, states, do),
            semantics=("parallel", "parallel", "arbitrary"), riders=riders)

    def impl(qkv, cos, sin, lgb, *shards):
        return fwd_call(qkv, cos, sin, lgb, _Riders(gathers=[s.astype(BF16) for s in shards]))

    @jax.custom_vjp
    def ret(qkv, cos, sin, lgb, *shards):
        o, _, *gathered = impl(qkv, cos, sin, lgb, *shards)
        return (o, *gathered)

    def fwd(qkv, cos, sin, lgb, *shards):
        o, states, *gathered = impl(qkv, cos, sin, lgb, *shards)
        return (o, *gathered), (qkv, cos, sin, lgb, states)

    def bwd(res, cts):
        qkv, cos, sin, lgb, states = res
        dq, dkk, dvv, dlg, *exchanged = bwd_call(qkv, cos, sin, lgb, states, cts[0],
                                                 _Riders(exchanges=list(cts[1:])))
        dqkv = jnp.concatenate([dq[0] + dq[1], dkk[0] + dkk[1], dvv[0] + dvv[1]], axis=1)
        return (dqkv, jnp.zeros_like(cos), jnp.zeros_like(sin), dlg, *_sum_exchanged(exchanged, "ret_" + tag))

    ret.defvjp(fwd, bwd)
    return ret


MLA_SCALE = (MLA_D_NOPE + MLA_D_ROPE) ** -0.5
MLA_Q_ROWS = 256
MLA_KEY_BLOCKS = 2


def _attn_query(q, cos, sin_signed, perm):
    qr = _rot(q[:, MLA_D_NOPE:], cos, sin_signed, perm)
    return jnp.concatenate([q[:, :MLA_D_NOPE], qr], axis=1).astype(BF16)


def _make_attention(tag, nx_rows):
    hw = MLA_D_NOPE + MLA_PAD
    tq = MLA_Q_ROWS

    def operands(kv, kr):
        t = kv.shape[0]
        kv3 = kv.reshape(t, MLA_HEADS, MLA_D_NOPE + MLA_D_V)
        k = jnp.concatenate([kv3[..., :MLA_D_NOPE], jnp.broadcast_to(kr[:, None, :], (t, MLA_HEADS, MLA_PAD))],
                            axis=2).reshape(t, MLA_HEADS * hw).astype(BF16)
        v = kv3[..., MLA_D_NOPE:].reshape(t, MLA_HEADS * MLA_D_V).astype(BF16)
        return k, k.T, v

    def softmax_t(s_t):
        m = jnp.max(s_t, axis=0, keepdims=True)
        e = jnp.exp((s_t - m) * MLA_SCALE)
        return e, m, 1.0 / jnp.sum(e, axis=0, keepdims=True)

    def fwd_call(q, k_t, v, cos, sin_signed, perm, riders):
        t = q.shape[0]
        nctx = t - nx_rows

        def body(q_ref, cos_ref, sin_ref, perm_ref, kt_ref, v_ref, o_ref, m_ref, linv_ref):
            i = pl.program_id(1)
            qb = _attn_query(q_ref[...], cos_ref[...], sin_ref[...], perm_ref[...])

            def attend(kt_part, v_part):
                s = _raw_dot(qb, kt_part, "nn")
                m = jnp.max(s, axis=1, keepdims=True)
                e = jnp.exp((s - m) * MLA_SCALE)
                linv = 1.0 / jnp.sum(e, axis=1, keepdims=True)
                o_ref[...] = _raw_dot(e * linv, v_part, "nn")
                m_ref[...] = m
                linv_ref[...] = linv

            @pl.when(i * tq < nx_rows)
            def _():
                attend(kt_ref[...], v_ref[...])

            @pl.when(i * tq >= nx_rows)
            def _():
                attend(kt_ref[:, pl.ds(nx_rows, nctx)], v_ref[pl.ds(nx_rows, nctx), :])

        stat = jax.ShapeDtypeStruct((MLA_HEADS, t, 1), F32)
        stat_spec = pl.BlockSpec((None, tq, 1), lambda h, i: (h, i, 0))
        o, m, linv, *gathered = _call(
            body, name="mla_" + tag, grid=(MLA_HEADS, t // tq),
            out_shape=(jax.ShapeDtypeStruct((t, MLA_HEADS * MLA_D_V), F32), stat, stat),
            in_specs=[pl.BlockSpec((tq, hw), lambda h, i: (i, h)),
                      pl.BlockSpec((tq, MLA_PAD), lambda h, i: (i, 0)),
                      pl.BlockSpec((tq, MLA_PAD), lambda h, i: (i, 0)),
                      pl.BlockSpec((MLA_PAD, MLA_PAD), lambda h, i: (0, 0)),
                      pl.BlockSpec((hw, t), lambda h, i: (h, 0)),
                      pl.BlockSpec((t, MLA_D_V), lambda h, i: (0, h))],
            out_specs=(pl.BlockSpec((tq, MLA_D_V), lambda h, i: (i, h)), stat_spec, stat_spec),
            operands=(q, cos, sin_signed, perm, k_t, v), semantics=("parallel", "arbitrary"), riders=riders)
        return (o, m.reshape(MLA_HEADS, 1, t), linv.reshape(MLA_HEADS, 1, t), *gathered)

    def bwd_call(q, k, k_t, v, cos, sin_signed, perm, o, m, linv, do, riders):
        t = q.shape[0]
        nctx = t - nx_rows

        def body(q_ref, cos_ref, sin_ref, perm_ref, k_ref, kt_ref, v_ref, o_ref, m_ref, linv_ref, do_ref,
                 dq_ref, dkn_ref, dv_ref, dkr_ref):
            i = pl.program_id(1)
            cos_t, sin_t, perm_t = cos_ref[...], sin_ref[...], perm_ref[...]
            qb = _attn_query(q_ref[...], cos_t, sin_t, perm_t)
            do_t = do_ref[...]
            dob = do_t.astype(BF16)
            delta = jnp.sum((do_t * o_ref[...]).T, axis=0, keepdims=True)
            m_t, linv_t = m_ref[...], linv_ref[...]

            def key_rows(rows):
                e = jnp.exp((_raw_dot(k_ref[rows, :], qb, "nt") - m_t) * MLA_SCALE)
                p_t = e * linv_t
                dp_t = _raw_dot(v_ref[rows, :], dob, "nt")
                ds_t = (p_t * (dp_t - delta) * MLA_SCALE).astype(BF16)
                dk = _raw_dot(ds_t, qb, "nn")
                dkn_ref[rows, :] += dk[:, :MLA_D_NOPE]
                dkr_ref[rows, :] += dk[:, MLA_D_NOPE:]
                dv_ref[rows, :] += _raw_dot(p_t, dob, "nn")
                return _raw_dot(kt_ref[:, rows], ds_t, "nn")

            def finish(dq_t):
                dqb = dq_t.T
                g_r = dqb[:, MLA_D_NOPE:]
                dq_r = g_r * cos_t + jnp.dot(g_r * sin_t, perm_t, precision=lax.Precision.HIGHEST,
                                             preferred_element_type=F32)
                dq_ref[...] = jnp.concatenate([dqb[:, :MLA_D_NOPE], dq_r], axis=1)

            @pl.when(i == 0)
            def _():
                dkn_ref[...] = jnp.zeros_like(dkn_ref)
                dv_ref[...] = jnp.zeros_like(dv_ref)

            @pl.when(jnp.logical_and(i == 0, pl.program_id(0) == 0))
            def _():
                dkr_ref[...] = jnp.zeros_like(dkr_ref)

            @pl.when(i * tq < nx_rows)
            def _():
                size = t // MLA_KEY_BLOCKS
                dq_t = key_rows(pl.ds(0, size))
                for b in range(1, MLA_KEY_BLOCKS):
                    dq_t = dq_t + key_rows(pl.ds(b * size, size))
                finish(dq_t)

            @pl.when(i * tq >= nx_rows)
            def _():
                finish(key_rows(pl.ds(nx_rows, nctx)))

        stat_spec = pl.BlockSpec((None, 1, tq), lambda h, i: (h, 0, i))
        return _call(
            body, name="mla_bwd_" + tag, grid=(MLA_HEADS, t // tq),
            out_shape=(jax.ShapeDtypeStruct(q.shape, F32),
                       jax.ShapeDtypeStruct((t, MLA_HEADS * MLA_D_NOPE), F32),
                       jax.ShapeDtypeStruct((t, MLA_HEADS * MLA_D_V), F32),
                       jax.ShapeDtypeStruct((t, MLA_PAD), F32)),
            in_specs=[pl.BlockSpec((tq, hw), lambda h, i: (i, h)),
                      pl.BlockSpec((tq, MLA_PAD), lambda h, i: (i, 0)),
                      pl.BlockSpec((tq, MLA_PAD), lambda h, i: (i, 0)),
                      pl.BlockSpec((MLA_PAD, MLA_PAD), lambda h, i: (0, 0)),
                      pl.BlockSpec((t, hw), lambda h, i: (0, h)),
                      pl.BlockSpec((hw, t), lambda h, i: (h, 0)),
                      pl.BlockSpec((t, MLA_D_V), lambda h, i: (0, h)),
                      pl.BlockSpec((tq, MLA_D_V), lambda h, i: (i, h)),
                      stat_spec, stat_spec,
                      pl.BlockSpec((tq, MLA_D_V), lambda h, i: (i, h))],
            out_specs=(pl.BlockSpec((tq, hw), lambda h, i: (i, h)),
                       pl.BlockSpec((t, MLA_D_NOPE), lambda h, i: (0, h)),
                       pl.BlockSpec((t, MLA_D_V), lambda h, i: (0, h)),
                       pl.BlockSpec((t, MLA_PAD), lambda h, i: (0, 0))),
            operands=(q, cos, sin_signed, perm, k, k_t, v, o, m, linv, do),
            semantics=("arbitrary", "arbitrary"), riders=riders)

    def impl(q, kv, kr, cos, sin_signed, perm, *shards):
        k, k_t, v = operands(kv, kr)
        o, m, linv, *gathered = fwd_call(q, k_t, v, cos, sin_signed, perm,
                                         _Riders(gathers=[s.astype(BF16) for s in shards]))
        return (o, *gathered), (q, k, k_t, v, cos, sin_signed, perm, o, m, linv)

    @jax.custom_vjp
    def attn(q, kv, kr, cos, sin_signed, perm, *shards):
        return impl(q, kv, kr, cos, sin_signed, perm, *shards)[0]

    def fwd(q, kv, kr, cos, sin_signed, perm, *shards):
        return impl(q, kv, kr, cos, sin_signed, perm, *shards)

    def bwd(res, cts):
        q, k, k_t, v, cos, sin_signed, perm, o, m, linv = res
        dq, dkn, dv, dkr, *exchanged = bwd_call(q, k, k_t, v, cos, sin_signed, perm, o, m, linv, cts[0],
                                                _Riders(exchanges=list(cts[1:])))
        t = q.shape[0]
        dkv = jnp.concatenate([dkn.reshape(t, MLA_HEADS, MLA_D_NOPE), dv.reshape(t, MLA_HEADS, MLA_D_V)],
                              axis=2).reshape(t, -1)
        return (dq, dkv, dkr, jnp.zeros_like(cos), jnp.zeros_like(sin_signed), jnp.zeros_like(perm),
                *_sum_exchanged(exchanged, "mla_" + tag))

    attn.defvjp(fwd, bwd)
    return attn


def _adamw_call(w, g, m, v, name):
    rows, cols = w.shape
    tr = _div(rows, max(SUBLANE, (1 << 18) // cols), SUBLANE) if rows % SUBLANE == 0 else rows

    def body(w_ref, g_ref, m_ref, v_ref, d_ref, nm_ref, nv_ref):
        gg = g_ref[...]
        nm = ADAM_B1 * m_ref[...] + (1.0 - ADAM_B1) * gg
        nv = ADAM_B2 * v_ref[...] + (1.0 - ADAM_B2) * jnp.square(gg)
        m_hat = nm / (1.0 - ADAM_B1 ** ADAM_STEP)
        v_hat = nv / (1.0 - ADAM_B2 ** ADAM_STEP)
        d_ref[...] = -ADAM_LR * (m_hat / (jnp.sqrt(v_hat) + ADAM_EPS) + ADAM_WD * w_ref[...])
        nm_ref[...] = nm
        nv_ref[...] = nv

    spec = pl.BlockSpec((tr, cols), lambda i: (i, 0))
    return pl.pallas_call(
        body, name=name, grid=(rows // tr,),
        out_shape=tuple(jax.ShapeDtypeStruct((rows, cols), F32) for _ in range(3)),
        in_specs=[spec] * 4, out_specs=(spec,) * 3,
        compiler_params=_cparams(("parallel",)),
    )(w, g, m, v)


def _adamw(w, g, m, v, name):
    shape = w.shape
    if w.ndim >= 2 and shape[-1] % LANE == 0:
        two = (-1, shape[-1])
    elif w.size % LANE == 0:
        two = (-1, LANE)
    else:
        two = (1, w.size)
    outs = _adamw_call(w.reshape(two), g.reshape(two), m.reshape(two), v.reshape(two), name)
    return tuple(o.reshape(shape) for o in outs)


def _ret_tables(nx, nctx, dk):
    inv = RET_ROPE_BASE ** (-jnp.linspace(0.0, 1.0, dk // 2, dtype=F32))
    ang = jnp.arange(nx, dtype=F32)[:, None] * inv[None, :]
    cos = jnp.concatenate([jnp.cos(ang), jnp.ones((nctx, dk // 2), F32)], axis=0)
    sin = jnp.concatenate([jnp.sin(ang), jnp.zeros((nctx, dk // 2), F32)], axis=0)
    return cos, sin


def _mla_tables(nx, nctx):
    quarter = MLA_D_ROPE // 4
    inv = AXIAL_ROPE_BASE ** (-jnp.arange(quarter, dtype=F32) * 2.0 / (MLA_D_ROPE // 2))
    tok = jnp.arange(nx)
    r_ang = (tok // GRID_W).astype(F32)[:, None] * inv[None, :]
    c_ang = (tok % GRID_W).astype(F32)[:, None] * inv[None, :]
    pad = MLA_PAD - MLA_D_ROPE
    cos = jnp.concatenate([jnp.cos(r_ang), jnp.cos(r_ang), jnp.cos(c_ang), jnp.cos(c_ang),
                           jnp.ones((nx, pad), F32)], axis=1)
    sin = jnp.concatenate([-jnp.sin(r_ang), jnp.sin(r_ang), -jnp.sin(c_ang), jnp.sin(c_ang),
                           jnp.zeros((nx, pad), F32)], axis=1)
    cos = jnp.concatenate([cos, jnp.ones((nctx, MLA_PAD), F32)], axis=0)
    sin = jnp.concatenate([sin, jnp.zeros((nctx, MLA_PAD), F32)], axis=0)
    lane = jnp.arange(MLA_PAD)
    partner = jnp.where(lane % (2 * quarter) < quarter, lane + quarter, lane - quarter)
    perm = ((lane[:, None] == partner[None, :]) & (lane[None, :] < MLA_D_ROPE)).astype(F32)
    return cos, sin, perm


def _columns(gathered):
    return gathered


def _rows(gathered):
    return gathered.reshape(1, gathered.shape[0] * gathered.shape[1], gathered.shape[2])


def _unshard_vec(gathered):
    return jnp.moveaxis(gathered, 0, 1).reshape(gathered.shape[1], -1)


def _pad_rows(a, rows):
    return jnp.concatenate([a, jnp.zeros((rows - a.shape[0],) + a.shape[1:], a.dtype)], axis=0)


def _loss_fn(weights, x, c_all, ctx, tgt, me):
    nx, d_model = x.shape
    nctx = ctx.shape[0]
    dk = d_model // RET_HEADS
    ret_cos, ret_sin = _ret_tables(nx, nctx, dk)
    mla_cos, mla_sin, mla_perm = _mla_tables(nx, nctx)

    rw = functools.partial(_make_rowwise, nx_rows=nx)

    n_ln = weights["ln_g"].size
    small = jnp.concatenate([weights["ln_g"].reshape(-1), weights["ln_b"].reshape(-1),
                             weights["mla_g_q"].reshape(-1), weights["mla_g_kv"].reshape(-1)])
    n_small = small.size
    small = _pad_rows(small.reshape(-1, 1), -(-n_small // LANE) * LANE).reshape(-1, LANE)
    small = _make_gather("small", F32)(small).reshape(N_DEV, -1)
    ln_g = _unshard_vec(small[:, :n_ln].reshape(N_DEV, DEPTH * 2, -1)).reshape(DEPTH, 2, d_model)
    ln_b = _unshard_vec(small[:, n_ln:2 * n_ln].reshape(N_DEV, DEPTH * 2, -1)).reshape(DEPTH, 2, d_model)
    n_g = weights["mla_g_q"].size
    g_q = _unshard_vec(small[:, 2 * n_ln:2 * n_ln + n_g].reshape(N_DEV, DEPTH // 2, -1))
    g_kv = _unshard_vec(small[:, 2 * n_ln + n_g:2 * n_ln + 2 * n_g].reshape(N_DEV, DEPTH // 2, -1))

    n_b = weights["ada_b"].size
    n_dec = weights["ret_decay_logit"].size
    repl = jnp.concatenate([weights["c_ctx"].reshape(-1), weights["ada_b"].reshape(-1),
                            weights["ret_decay_logit"].reshape(-1)])
    n_repl = repl.size
    repl = _pad_rows(repl.reshape(-1, 1), -(-n_repl // LANE) * LANE).reshape(-1, LANE)
    repl = _make_replicated("repl")(repl).reshape(-1)
    c_ctx = repl[:d_model]
    ada_b = repl[d_model:d_model + n_b].reshape(DEPTH, 6 * d_model)
    decay = repl[d_model + n_b:d_model + n_b + n_dec].reshape(weights["ret_decay_logit"].shape)

    cond_rows = 2 * SUBLANE
    cond = _pad_rows(jnp.concatenate([c_all, c_ctx[None, :]], axis=0), cond_rows)
    s_cond = _make_rowwise_plain_silu(cond)
    mods = [_make_mm("ada%d" % i)(s_cond, weights["ada_w"][i][None])[0] for i in range(DEPTH)]
    mod = jnp.concatenate(mods, axis=0)
    mod = _make_gather("mod", F32)(mod)
    mod = jnp.moveaxis(mod.reshape(N_DEV, DEPTH, cond_rows, -1), 0, 2).reshape(DEPTH, cond_rows, -1)
    mod = mod + ada_b[:, None, :]
    mine = lax.dynamic_slice_in_dim(mod, me, 1, axis=1)
    mod = jnp.concatenate([mine, mod[:, N_DEV:N_DEV + 1]], axis=1)
    mod = mod.reshape(DEPTH, 2, 6, 1, d_model)

    def grp(i, j):
        return mod[i, :, j]

    h = jnp.concatenate([x, ctx], axis=0)
    ahead = {"ret_qkv": _make_gather("ret_qkv", BF16)(weights["ret_w_qkv"][0])}
    for i in range(DEPTH):
        j = i // 2
        retention_layer, last = i % 2 == 0, i + 1 == DEPTH
        (u,) = rw(_f_modulate, ("row", "grp", "grp"), (True, True, True), (d_model,), "mod%d" % i)(
            h, grp(i, 1), grp(i, 0))
        if retention_layer:
            if "ret_g" in ahead:
                (qkv,) = _make_mm("ret_qkv%d" % j)(u, _columns(ahead.pop("ret_qkv")))
                (gates,) = _make_mm("ret_g%d" % j)(u, _columns(ahead.pop("ret_g")))
                w_o = ahead.pop("ret_o")
            else:
                qkv, w_g = _make_mm("ret_qkv%d" % j, 1)(u, _columns(ahead.pop("ret_qkv")), weights["ret_w_g"][j])
                gates, w_o = _make_mm("ret_g%d" % j, 1)(u, _columns(w_g), weights["ret_w_o"][j])
            lg = jax.nn.log_sigmoid(decay[j]).reshape(2 * RET_HEADS, 1, 1)
            lgb = jnp.broadcast_to(lg, (2 * RET_HEADS, SUBLANE, LANE))
            o, w_in, w_out = _make_retention("l%d" % j, nx)(
                qkv, ret_cos, ret_sin, lgb, weights["ffn_w_in"][i], weights["ffn_w_out"][i])
            (comb,) = rw(_f_ret_combine, ("row", "pair"), (True, True), (o.shape[2],),
                         "ret_comb%d" % j, tile=WIDE_ROW_TILE)(gates, o)
            nxt = [] if last else [weights[n][j] for n in MLA_MATRICES]
            y, *gathered = _make_mm("ret_o%d" % j, len(nxt))(comb, _rows(w_o), *nxt)
            if nxt:
                ahead["mla"] = gathered
        else:
            nxt = [] if last else [weights[n][j + 1] for n in ("ret_w_qkv", "ret_w_g", "ret_w_o")]
            y, *gathered = _mla_mixer(ahead.pop("mla"), j, u, g_q[j], g_kv[j], mla_cos, mla_sin, mla_perm, rw, nx, nxt)
            if nxt:
                ahead["ret_qkv"], ahead["ret_g"], ahead["ret_o"] = gathered
            w_in, w_out = ahead.pop("ffn_in"), ahead.pop("ffn_out")
        h1, u2 = rw(_f_ln_res_mod, ("row", "row", "grp", "par", "par", "grp", "grp"), (True,) * 7,
                    (d_model, d_model), "ln_a%d" % i)(
            h, y, grp(i, 2), ln_g[i, 0][None], ln_b[i, 0][None], grp(i, 4), grp(i, 3))
        if retention_layer and not last:
            ab, ahead["ffn_in"] = _make_mm("ffn_in%d" % i, 1)(u2, _columns(w_in), weights["ffn_w_in"][i + 1])
        else:
            (ab,) = _make_mm("ffn_in%d" % i)(u2, _columns(w_in))
        (act,) = rw(_f_swiglu, ("row",), (True,), (ab.shape[1] // 2,), "swiglu%d" % i, tile=WIDE_ROW_TILE)(ab)
        if retention_layer and not last:
            f, ahead["ffn_out"] = _make_mm("ffn_out%d" % i, 1)(act, _rows(w_out), weights["ffn_w_out"][i + 1])
        else:
            (f,) = _make_mm("ffn_out%d" % i)(act, _rows(w_out))
        (h,) = rw(_f_ln_res, ("row", "row", "grp", "par", "par"), (True,) * 5, (d_model,), "ln_f%d" % i)(
            h1, f, grp(i, 5), ln_g[i, 1][None], ln_b[i, 1][None])

    (rows,) = rw(_f_loss, ("row", "row"), (True, False), (LANE,), "loss")(h[:nx], tgt)
    return jnp.sum(rows[:, 0])


def _make_rowwise_plain_silu(cond):
    def body(c_ref, o_ref):
        o_ref[...] = _silu(c_ref[...])

    def call(c):
        return pl.pallas_call(body, name="silu_cond", out_shape=jax.ShapeDtypeStruct(c.shape, F32))(c)

    def bwd_body(c_ref, g_ref, o_ref):
        _, vjp = jax.vjp(_silu, c_ref[...])
        o_ref[...] = vjp(g_ref[...])[0]

    @jax.custom_vjp
    def op(c):
        return call(c)

    def fwd(c):
        return call(c), c

    def bwd(c, g):
        return (pl.pallas_call(bwd_body, name="silu_cond_bwd", out_shape=jax.ShapeDtypeStruct(c.shape, F32))(c, g),)

    op.defvjp(fwd, bwd)
    return op(cond)


MLA_MATRICES = ("mla_w_dq", "mla_w_uq", "mla_w_dkv", "mla_w_ukv", "mla_w_o")


def _mla_mixer(gathered, j, u, g_q, g_kv, cos, sin_signed, perm, rw, nx, ride):
    heads, dn, dr = MLA_HEADS, MLA_D_NOPE, MLA_D_ROPE
    w_dq, w_uq, w_dkv, w_ukv, w_o = gathered
    w_dq, w_dkv, w_ukv, w_o = _rows(w_dq), _rows(w_dkv), _columns(w_ukv), _rows(w_o)

    q_lora = w_uq.shape[1]
    w_uq = jnp.moveaxis(w_uq, 0, 1).reshape(q_lora, heads, dn + dr)
    w_uq = jnp.concatenate([w_uq, jnp.zeros((q_lora, heads, MLA_PAD - dr), w_uq.dtype)], axis=2)
    w_uq = w_uq.reshape(1, q_lora, heads * (dn + MLA_PAD))
    w_dkv = jnp.concatenate([w_dkv, jnp.zeros(w_dkv.shape[:2] + (MLA_PAD - dr,), w_dkv.dtype)], axis=2)

    (cq,) = _make_mm("mla_dq%d" % j)(u, w_dq)
    (cqn,) = rw(_f_rms, ("row", "par"), (True, True), (cq.shape[1],), "rms_q%d" % j)(cq, g_q[None])
    (q,) = _make_mm("mla_uq%d" % j)(cqn, w_uq)
    (ckv,) = _make_mm("mla_dkv%d" % j)(u, w_dkv)
    lora = ckv.shape[1] - MLA_PAD
    c_kv, kr = rw(_f_kv_latent, ("row", "par", "row", "row", "par"), (True, True, False, False, False),
                  (lora, MLA_PAD), "kv_lat%d" % j)(ckv, g_kv[None], cos, sin_signed, perm)
    (kv,) = _make_mm("mla_ukv%d" % j)(c_kv, w_ukv)
    o, *rode = _make_attention("l%d" % j, nx)(q, kv, kr, cos, sin_signed, perm, *ride)
    return (_make_mm("mla_o%d" % j)(o, w_o)[0], *rode)


WEIGHT_NAMES = ("c_ctx", "ada_w", "ada_b", "ln_g", "ln_b", "ret_w_qkv", "ret_w_g", "ret_decay_logit", "ret_w_o",
                "mla_w_dq", "mla_g_q", "mla_w_uq", "mla_w_dkv", "mla_g_kv", "mla_w_ukv", "mla_w_o",
                "ffn_w_in", "ffn_w_out")


def kernel(x, c, ctx, c_ctx, ada_w, ada_b, ln_g, ln_b, ret_w_qkv, ret_w_g, ret_decay_logit, ret_w_o, mla_w_dq, mla_g_q, mla_w_uq, mla_w_dkv, mla_g_kv, mla_w_ukv, mla_w_o, ffn_w_in, ffn_w_out, loss_target, m_c_ctx, m_ada_w, m_ada_b, m_ln_g, m_ln_b, m_ret_w_qkv, m_ret_w_g, m_ret_decay_logit, m_ret_w_o, m_mla_w_dq, m_mla_g_q, m_mla_w_uq, m_mla_w_dkv, m_mla_g_kv, m_mla_w_ukv, m_mla_w_o, m_ffn_w_in, m_ffn_w_out, v_c_ctx, v_ada_w, v_ada_b, v_ln_g, v_ln_b, v_ret_w_qkv, v_ret_w_g, v_ret_decay_logit, v_ret_w_o, v_mla_w_dq, v_mla_g_q, v_mla_w_uq, v_mla_w_dkv, v_mla_g_kv, v_mla_w_ukv, v_mla_w_o, v_ffn_w_in, v_ffn_w_out):
    given = dict(locals())
    weights = {n: given[n] for n in WEIGHT_NAMES}
    me = _my_index()
    c_all = _all_gather_call(c, "ag_cond").reshape(N_DEV, -1)

    def loss_of(wts, x2):
        return _loss_fn(wts, x2, c_all, ctx[0], loss_target[0], me)

    loss, (grad_w, grad_x) = jax.value_and_grad(loss_of, argnums=(0, 1))(weights, x[0])
    loss = lax.psum(loss, AXES)
    delta, new_m, new_v = {}, {}, {}
    for n in WEIGHT_NAMES:
        delta[n], new_m[n], new_v[n] = _adamw(weights[n], grad_w[n], given["m_" + n], given["v_" + n], "adamw_" + n)
    return (loss, grad_x[None], *[grad_w[n] for n in WEIGHT_NAMES], *[delta[n] for n in WEIGHT_NAMES],
            *[new_m[n] for n in WEIGHT_NAMES], *[new_v[n] for n in WEIGHT_NAMES])
```

```python
import functools
import math

import jax
import jax.numpy as jnp
from jax import lax
from jax.experimental import pallas as pl
from jax.experimental.pallas import tpu as pltpu

F32 = jnp.float32
BF16 = jnp.bfloat16

AXES = ("x", "y", "c")
N_DEV = 8
MESH_IDS = pl.DeviceIdType.MESH

DEPTH = 4
GRID_W = 64
RET_HEADS = 8
RET_CHUNK = 128
RET_ROPE_BASE = 10000.0
GN_EPS = 1e-6
MLA_HEADS = 16
MLA_D_NOPE = 128
MLA_D_ROPE = 64
MLA_D_V = 128
AXIAL_ROPE_BASE = 10000.0
RMS_EPS = 1e-6
LN_EPS = 1e-5
DEEPNORM_ALPHA = (2 * DEPTH) ** 0.25

ADAM_LR = 0.001
ADAM_B1 = 0.9
ADAM_B2 = 0.999
ADAM_EPS = 1e-08
ADAM_WD = 0.01
ADAM_STEP = 10

LANE = 128
SUBLANE = 8
VMEM_LIMIT = 56 * 1024 * 1024

ROW_TILE = 256
WIDE_ROW_TILE = 64
MM_ROWS = 544
MLA_PAD = 128


def _div(n, cap, mult):
    best = None
    for d in range(mult, min(n, cap) + 1, mult):
        if n % d == 0:
            best = d
    return n if best is None else best


def _cparams(sem=None):
    kw = dict(vmem_limit_bytes=VMEM_LIMIT)
    if sem is not None:
        kw["dimension_semantics"] = sem
    return pltpu.CompilerParams(**kw)


_DN = {"nn": (((1,), (0,)), ((), ())), "nt": (((1,), (1,)), ((), ())), "tn": (((0,), (0,)), ((), ()))}


def _raw_dot(a, b, kind):
    return lax.dot_general(a.astype(BF16), b.astype(BF16), _DN[kind], preferred_element_type=F32)


@functools.partial(jax.custom_vjp, nondiff_argnums=(2,))
def _dot(a, b, kind):
    return _raw_dot(a, b, kind)


def _dot_fwd(a, b, kind):
    return _raw_dot(a, b, kind), (a, b)


def _dot_bwd(kind, res, g):
    a, b = res
    if kind == "nn":
        return _raw_dot(g, b, "nt"), _raw_dot(a, g, "tn")
    if kind == "nt":
        return _raw_dot(g, b, "nn"), _raw_dot(g, a, "tn")
    return _raw_dot(b, g, "nt"), _raw_dot(a, g, "nn")


_dot.defvjp(_dot_fwd, _dot_bwd)


def _sigmoid(x):
    return 1.0 / (1.0 + jnp.exp(-x))


def _silu(x):
    return x * _sigmoid(x)


def _my_index():
    return 4 * lax.axis_index("x") + 2 * lax.axis_index("y") + lax.axis_index("c")


def _all_gather_call(x, name):
    def body(x_ref, out_ref, send_sems, recv_sems, local_sem):
        x_, y_, c_ = lax.axis_index("x"), lax.axis_index("y"), lax.axis_index("c")
        me, sibling = (x_, y_, c_), (x_, y_, 1 - c_)
        chips = [(1 - x_, y_), (x_, 1 - y_), (1 - x_, 1 - y_)]

        def slot(px, py, pc):
            return out_ref.at[4 * px + 2 * py + pc]

        def copy(k, block, to, src=None):
            return pltpu.make_async_remote_copy(
                src_ref=slot(*block) if src is None else src, dst_ref=slot(*block),
                send_sem=send_sems.at[k], recv_sem=recv_sems.at[k],
                device_id=to, device_id_type=MESH_IDS)

        mine = pltpu.make_async_copy(x_ref, slot(*me), local_sem)
        mine.start()
        first = [copy(0, me, sibling, src=x_ref)]
        first += [copy(1 + j, me, (*chip, c_), src=x_ref) for j, chip in enumerate(chips)]
        for cp in first:
            cp.start()
        passed = [copy(4 + j, (*chip, c_), sibling) for j, chip in enumerate(chips)]
        for j, chip in enumerate(chips):
            copy(1 + j, (*chip, c_), me).wait_recv()
            passed[j].start()
        copy(0, sibling, me).wait_recv()
        for j, chip in enumerate(chips):
            copy(4 + j, (*chip, 1 - c_), me).wait_recv()
        for cp in first + passed:
            cp.wait_send()
        mine.wait()

    return pl.pallas_call(
        body, name=name,
        out_shape=jax.ShapeDtypeStruct((N_DEV,) + x.shape, x.dtype),
        in_specs=[pl.BlockSpec(memory_space=pl.ANY)],
        out_specs=pl.BlockSpec(memory_space=pl.ANY),
        scratch_shapes=[pltpu.SemaphoreType.DMA((7,)), pltpu.SemaphoreType.DMA((7,)), pltpu.SemaphoreType.DMA],
    )(x)


def _all_to_all_call(g, name):
    def body(g_ref, out_ref, send_sems, recv_sems, local_sem):
        x_, y_, c_ = lax.axis_index("x"), lax.axis_index("y"), lax.axis_index("c")
        my = 4 * x_ + 2 * y_ + c_
        peers = []
        for k in range(1, N_DEV):
            px = 1 - x_ if (k >> 2) & 1 else x_
            py = 1 - y_ if (k >> 1) & 1 else y_
            pc = 1 - c_ if k & 1 else c_
            peers.append((px, py, pc))

        def copy(k, peer):
            pid = 4 * peer[0] + 2 * peer[1] + peer[2]
            return pltpu.make_async_remote_copy(
                src_ref=g_ref.at[pid], dst_ref=out_ref.at[my],
                send_sem=send_sems.at[k], recv_sem=recv_sems.at[k],
                device_id=peer, device_id_type=MESH_IDS)

        def arrival(k, peer):
            pid = 4 * peer[0] + 2 * peer[1] + peer[2]
            return pltpu.make_async_remote_copy(
                src_ref=g_ref.at[pid], dst_ref=out_ref.at[pid],
                send_sem=send_sems.at[k], recv_sem=recv_sems.at[k],
                device_id=peer, device_id_type=MESH_IDS)

        mine = pltpu.make_async_copy(g_ref.at[my], out_ref.at[my], local_sem)
        mine.start()
        sends = [copy(k, peer) for k, peer in enumerate(peers)]
        for cp in sends:
            cp.start()
        for k, peer in enumerate(peers):
            arrival(k, peer).wait_recv()
        for cp in sends:
            cp.wait_send()
        mine.wait()

    return pl.pallas_call(
        body, name=name,
        out_shape=jax.ShapeDtypeStruct(g.shape, g.dtype),
        in_specs=[pl.BlockSpec(memory_space=pl.ANY)],
        out_specs=pl.BlockSpec(memory_space=pl.ANY),
        scratch_shapes=[pltpu.SemaphoreType.DMA((7,)), pltpu.SemaphoreType.DMA((7,)), pltpu.SemaphoreType.DMA],
    )(g)


def _sum_slots_call(g, name):
    slots, rows, cols = g.shape
    tr = _div(rows, max(SUBLANE, (1 << 19) // cols), 16) if rows % 16 == 0 else rows

    def body(g_ref, o_ref):
        acc = g_ref[0].astype(F32)
        for s in range(1, slots):
            acc = acc + g_ref[s].astype(F32)
        o_ref[...] = acc

    return pl.pallas_call(
        body, name=name, grid=(rows // tr,),
        out_shape=jax.ShapeDtypeStruct((rows, cols), F32),
        in_specs=[pl.BlockSpec((slots, tr, cols), lambda i: (0, i, 0))],
        out_specs=pl.BlockSpec((tr, cols), lambda i: (i, 0)),
        compiler_params=_cparams(("parallel",)),
    )(g)


def _pair_sum_call(g, half, name):
    _, rows, cols = g.shape
    tr = _div(rows, max(SUBLANE, (1 << 20) // cols), 16) if rows % 16 == 0 else rows
    core = lax.axis_index("c").astype(jnp.int32).reshape(1)

    def body(core_ref, g_ref, h_ref, o_ref):
        del core_ref
        o_ref[...] = (g_ref[...].astype(F32) + h_ref[...].astype(F32)).astype(o_ref.dtype)

    return pl.pallas_call(
        body, name=name,
        grid_spec=pltpu.PrefetchScalarGridSpec(
            num_scalar_prefetch=1, grid=(N_DEV // 2, rows // tr),
            in_specs=[pl.BlockSpec((None, tr, cols), lambda k, i, core_ref: (2 * k + core_ref[0], i, 0)),
                      pl.BlockSpec((None, tr, cols), lambda k, i, core_ref: (k, i, 0))],
            out_specs=pl.BlockSpec((None, tr, cols), lambda k, i, core_ref: (k, i, 0))),
        out_shape=jax.ShapeDtypeStruct(half.shape, half.dtype),
        compiler_params=_cparams(("parallel", "parallel")),
    )(core, g, half)


def _make_gather(tag, travel_dtype):
    @jax.custom_vjp
    def gather(w):
        return _all_gather_call(w.astype(travel_dtype), "ag_" + tag)

    def fwd(w):
        return gather(w), None

    def bwd(_, g):
        got = _all_to_all_call(g.astype(travel_dtype), "a2a_" + tag)
        return (_sum_slots_call(got, "sum_" + tag),)

    gather.defvjp(fwd, bwd)
    return gather


def _make_replicated(tag):
    @jax.custom_vjp
    def rep(p):
        return p

    def fwd(p):
        return p, None

    def bwd(_, g):
        return (_sum_slots_call(_all_gather_call(g, "ag_" + tag), "sum_" + tag),)

    rep.defvjp(fwd, bwd)
    return rep


class _Riders:
    def __init__(self, gathers=(), exchanges=(), sibling_swaps=(), chip_exchanges=()):
        self.gathers = list(gathers)
        self.exchanges = list(exchanges)
        self.sibling_swaps = list(sibling_swaps)
        self.chip_exchanges = list(chip_exchanges)
        self.n = len(self.operands())

    def operands(self):
        return self.gathers + self.exchanges + self.sibling_swaps + self.chip_exchanges

    def out_shape(self):
        return ([jax.ShapeDtypeStruct((N_DEV,) + g.shape, g.dtype) for g in self.gathers]
                + [jax.ShapeDtypeStruct(e.shape, e.dtype) for e in self.exchanges]
                + [jax.ShapeDtypeStruct((N_DEV // 2,) + s.shape[1:], s.dtype) for s in self.sibling_swaps]
                + [jax.ShapeDtypeStruct(p.shape, p.dtype) for p in self.chip_exchanges])

    def scratch(self):
        return [pltpu.SemaphoreType.DMA((7 * self.n,)), pltpu.SemaphoreType.DMA((7 * self.n,)),
                pltpu.SemaphoreType.DMA((self.n,))]

    def _gather_phase(self, phase, r, x_ref, out_ref, send_sems, recv_sems, local_sem):
        x_, y_, c_ = lax.axis_index("x"), lax.axis_index("y"), lax.axis_index("c")
        me, sibling = (x_, y_, c_), (x_, y_, 1 - c_)
        chips = [(1 - x_, y_), (x_, 1 - y_), (1 - x_, 1 - y_)]

        def slot(px, py, pc):
            return out_ref.at[4 * px + 2 * py + pc]

        def copy(k, block, to, src=None):
            return pltpu.make_async_remote_copy(
                src_ref=slot(*block) if src is None else src, dst_ref=slot(*block),
                send_sem=send_sems.at[7 * r + k], recv_sem=recv_sems.at[7 * r + k],
                device_id=to, device_id_type=MESH_IDS)

        def mine():
            return pltpu.make_async_copy(x_ref, slot(*me), local_sem.at[r])

        def first():
            return [copy(0, me, sibling, src=x_ref)] + [copy(1 + j, me, (*chip, c_), src=x_ref)
                                                        for j, chip in enumerate(chips)]

        def passed(j):
            return copy(4 + j, (*chips[j], c_), sibling)

        if phase == "start":
            mine().start()
            for cp in first():
                cp.start()
        elif phase == "pass":
            for j, chip in enumerate(chips):
                copy(1 + j, (*chip, c_), me).wait_recv()
                passed(j).start()
        else:
            copy(0, sibling, me).wait_recv()
            for j, chip in enumerate(chips):
                copy(4 + j, (*chip, 1 - c_), me).wait_recv()
            for cp in first() + [passed(j) for j in range(3)]:
                cp.wait_send()
            mine().wait()

    def _exchange_phase(self, phase, r, g_ref, out_ref, send_sems, recv_sems, local_sem):
        x_, y_, c_ = lax.axis_index("x"), lax.axis_index("y"), lax.axis_index("c")
        my = 4 * x_ + 2 * y_ + c_
        peers = []
        for k in range(1, N_DEV):
            peers.append((1 - x_ if (k >> 2) & 1 else x_, 1 - y_ if (k >> 1) & 1 else y_, 1 - c_ if k & 1 else c_))

        def copy(k, peer, arriving):
            pid = 4 * peer[0] + 2 * peer[1] + peer[2]
            return pltpu.make_async_remote_copy(
                src_ref=g_ref.at[pid], dst_ref=out_ref.at[pid if arriving else my],
                send_sem=send_sems.at[7 * r + k], recv_sem=recv_sems.at[7 * r + k],
                device_id=peer, device_id_type=MESH_IDS)

        def mine():
            return pltpu.make_async_copy(g_ref.at[my], out_ref.at[my], local_sem.at[r])

        if phase == "start":
            mine().start()
            for k, peer in enumerate(peers):
                copy(k, peer, False).start()
        elif phase == "finish":
            for k, peer in enumerate(peers):
                copy(k, peer, True).wait_recv()
            for k, peer in enumerate(peers):
                copy(k, peer, False).wait_send()
            mine().wait()

    def _sibling_phase(self, phase, r, g_ref, out_ref, send_sems, recv_sems):
        x_, y_, c_ = lax.axis_index("x"), lax.axis_index("y"), lax.axis_index("c")

        def copy(chip):
            return pltpu.make_async_remote_copy(
                src_ref=g_ref.at[2 * chip + 1 - c_], dst_ref=out_ref.at[chip],
                send_sem=send_sems.at[7 * r + chip], recv_sem=recv_sems.at[7 * r + chip],
                device_id=(x_, y_, 1 - c_), device_id_type=MESH_IDS)

        if phase == "start":
            for chip in range(N_DEV // 2):
                copy(chip).start()
        elif phase == "finish":
            for chip in range(N_DEV // 2):
                copy(chip).wait_recv()
            for chip in range(N_DEV // 2):
                copy(chip).wait_send()

    def _chip_phase(self, phase, r, p_ref, out_ref, send_sems, recv_sems, local_sem):
        x_, y_, c_ = lax.axis_index("x"), lax.axis_index("y"), lax.axis_index("c")
        my = 2 * x_ + y_
        peers = [(1 - x_ if (k >> 1) & 1 else x_, 1 - y_ if k & 1 else y_) for k in range(1, N_DEV // 2)]

        def copy(k, peer, arriving):
            pid = 2 * peer[0] + peer[1]
            return pltpu.make_async_remote_copy(
                src_ref=p_ref.at[pid], dst_ref=out_ref.at[pid if arriving else my],
                send_sem=send_sems.at[7 * r + k], recv_sem=recv_sems.at[7 * r + k],
                device_id=(*peer, c_), device_id_type=MESH_IDS)

        def mine():
            return pltpu.make_async_copy(p_ref.at[my], out_ref.at[my], local_sem.at[r])

        if phase == "start":
            mine().start()
            for k, peer in enumerate(peers):
                copy(k, peer, False).start()
        elif phase == "finish":
            for k, peer in enumerate(peers):
                copy(k, peer, True).wait_recv()
            for k, peer in enumerate(peers):
                copy(k, peer, False).wait_send()
            mine().wait()

    def phase(self, phase, in_refs, out_refs, send_sems, recv_sems, local_sem):
        ng, ne, ns = len(self.gathers), len(self.exchanges), len(self.sibling_swaps)
        for r in range(self.n):
            if r < ng:
                self._gather_phase(phase, r, in_refs[r], out_refs[r], send_sems, recv_sems, local_sem)
            elif r < ng + ne:
                self._exchange_phase(phase, r, in_refs[r], out_refs[r], send_sems, recv_sems, local_sem)
            elif r < ng + ne + ns:
                self._sibling_phase(phase, r, in_refs[r], out_refs[r], send_sems, recv_sems)
            else:
                self._chip_phase(phase, r, in_refs[r], out_refs[r], send_sems, recv_sems, local_sem)


def _call(body, *, name, grid, in_specs, out_specs, out_shape, operands, scratch_shapes=(), semantics, riders=None):
    if riders is None or riders.n == 0:
        return tuple(pl.pallas_call(
            body, name=name, grid=grid, out_shape=tuple(out_shape), in_specs=list(in_specs),
            out_specs=tuple(out_specs), scratch_shapes=list(scratch_shapes), compiler_params=_cparams(semantics),
        )(*operands))

    n_in, n_out, n_scr, rn = len(in_specs), len(out_specs), len(scratch_shapes), riders.n
    steps = math.prod(grid)

    def full_body(*refs):
        ins, r_in = refs[:n_in], refs[n_in:n_in + rn]
        outs, r_out = refs[n_in + rn:n_in + rn + n_out], refs[n_in + rn + n_out:n_in + 2 * rn + n_out]
        scr = refs[n_in + 2 * rn + n_out:n_in + 2 * rn + n_out + n_scr]
        sems = refs[n_in + 2 * rn + n_out + n_scr:]
        step = 0
        for axis, size in enumerate(grid):
            step = step * size + pl.program_id(axis)

        @pl.when(step == 0)
        def _():
            riders.phase("start", r_in, r_out, *sems)

        @pl.when(step == steps - 1 - steps // 8)
        def _():
            riders.phase("pass", r_in, r_out, *sems)

        body(*ins, *outs, *scr)

        @pl.when(step == steps - 1)
        def _():
            riders.phase("finish", r_in, r_out, *sems)

    hbm = pl.BlockSpec(memory_space=pl.ANY)
    return tuple(pl.pallas_call(
        full_body, name=name, grid=grid,
        out_shape=tuple(out_shape) + tuple(riders.out_shape()),
        in_specs=list(in_specs) + [hbm] * rn,
        out_specs=tuple(out_specs) + (hbm,) * rn,
        scratch_shapes=list(scratch_shapes) + riders.scratch(),
        compiler_params=_cparams(("arbitrary",) * len(grid)),
    )(*operands, *riders.operands()))


def _mm_nn_call(a, w3, name, riders=None):
    m, k = a.shape
    nb, _, ns = w3.shape
    tm = _div(m, MM_ROWS, SUBLANE)
    tn = _div(ns, max(LANE, (6 << 20) // (2 * k)), LANE)
    tps = ns // tn

    def body(a_ref, w_ref, o_ref):
        o_ref[...] = _raw_dot(a_ref[...], w_ref[...], "nn")

    return _call(
        body, name=name, grid=(nb * tps, m // tm),
        out_shape=[jax.ShapeDtypeStruct((m, nb * ns), F32)],
        in_specs=[pl.BlockSpec((tm, k), lambda j, i: (i, 0)),
                  pl.BlockSpec((None, k, tn), lambda j, i: (j // tps, 0, j % tps))],
        out_specs=[pl.BlockSpec((tm, tn), lambda j, i: (i, j))],
        operands=(a, w3), semantics=("parallel", "parallel"), riders=riders)


def _mm_nt_call(g, w3, name, riders=None):
    m, n = g.shape
    nb, k, ns = w3.shape
    tm = _div(m, MM_ROWS, 16) if m % 16 == 0 else m
    tk = _div(k, 2048, LANE)
    tn = _div(ns, max(LANE, (6 << 20) // (2 * tk)), LANE)
    tps = ns // tn
    nj = nb * tps
    emit = tk == k

    def body(g_ref, w_ref, o_ref, *rest):
        acc = rest[-1]
        j = pl.program_id(2)
        gb = g_ref[...].astype(BF16)
        if emit:
            rest[0][...] = gb
        p = _raw_dot(gb, w_ref[...], "nt")

        @pl.when(j == 0)
        def _():
            acc[...] = p

        @pl.when(j > 0)
        def _():
            acc[...] += p

        @pl.when(j == nj - 1)
        def _():
            o_ref[...] = acc[...]

    da, *rest = _call(
        body, name=name, grid=(m // tm, k // tk, nj),
        out_shape=[jax.ShapeDtypeStruct((m, k), F32)] + ([jax.ShapeDtypeStruct((m, n), BF16)] if emit else []),
        in_specs=[pl.BlockSpec((tm, tn), lambda i, kk, j: (i, j)),
                  pl.BlockSpec((None, tk, tn), lambda i, kk, j: (j // tps, kk, j % tps))],
        out_specs=[pl.BlockSpec((tm, tk), lambda i, kk, j: (i, kk))]
        + ([pl.BlockSpec((tm, tn), lambda i, kk, j: (i, j))] if emit else []),
        scratch_shapes=[pltpu.VMEM((tm, tk), F32)],
        operands=(g, w3), semantics=("parallel", "arbitrary", "arbitrary"), riders=riders)
    return (da, *rest) if emit else (da, g.astype(BF16), *rest)


def _mm_dw_call(a, g, nb, out_dtype, name, riders=None):
    m, k = a.shape
    n = g.shape[1]
    ns = n // nb
    tn = _div(ns, max(LANE, (13 << 20) // (2 * m)), LANE)
    tk = _div(k, max(LANE, (9 << 20) // (4 * m)), LANE)
    tps = ns // tn

    def body(a_ref, g_ref, o_ref):
        o_ref[...] = _raw_dot(a_ref[...], g_ref[...], "tn").astype(o_ref.dtype)

    return _call(
        body, name=name, grid=(nb * tps, k // tk),
        out_shape=[jax.ShapeDtypeStruct((nb, k, ns), out_dtype)],
        in_specs=[pl.BlockSpec((m, tk), lambda j, kk: (0, kk)),
                  pl.BlockSpec((m, tn), lambda j, kk: (0, j))],
        out_specs=[pl.BlockSpec((None, tk, tn), lambda j, kk: (j // tps, kk, j % tps))],
        operands=(a, g), semantics=("parallel", "parallel"), riders=riders)


def _sum_exchanged(exchanged, tag):
    return [_sum_slots_call(e, "sum_%s_%d" % (tag, k)) for k, e in enumerate(exchanged)]


def _make_mm(tag, n_ride=0):
    def impl(a, w3, *shards):
        return _mm_nn_call(a, w3, "mm_" + tag, _Riders(gathers=[s.astype(BF16) for s in shards]))

    @jax.custom_vjp
    def mm(a, w3, *shards):
        return impl(a, w3, *shards)

    def fwd(a, w3, *shards):
        return impl(a, w3, *shards), (a, w3)

    def bwd(res, cts):
        a, w3 = res
        d_gathered = list(cts[1:])
        da, gb, *halves = _mm_nt_call(cts[0], w3, "mm_da_" + tag, _Riders(sibling_swaps=d_gathered))
        pairs = [_pair_sum_call(d, h, "pair_%s_%d" % (tag, r)) for r, (d, h) in enumerate(zip(d_gathered, halves))]
        dw, *exchanged = _mm_dw_call(a, gb, w3.shape[0], w3.dtype, "mm_dw_" + tag,
                                     _Riders(chip_exchanges=pairs))
        return (da, dw, *_sum_exchanged(exchanged, tag))

    mm.defvjp(fwd, bwd)
    return mm


def _make_rowwise(f, kinds, diff, out_cols, tag, nx_rows, tile=ROW_TILE):
    n_in = len(kinds)
    nxt = nx_rows // tile

    def in_spec(kind, arr):
        if kind == "row":
            return pl.BlockSpec((tile, arr.shape[1]), lambda i: (i, 0))
        if kind == "pair":
            return pl.BlockSpec((arr.shape[0], tile, arr.shape[2]), lambda i: (0, i, 0))
        if kind == "grp":
            return pl.BlockSpec((None, 1, arr.shape[2]), lambda i: (i // nxt, 0, 0))
        return pl.BlockSpec(arr.shape, lambda i: (0,) * arr.ndim)

    def fwd_call(*args):
        t = next(a.shape[0] for a, kd in zip(args, kinds) if kd == "row")

        def body(*refs):
            outs = f(*[r[...] for r in refs[:n_in]])
            for r, o in zip(refs[n_in:], outs):
                r[...] = o

        return pl.pallas_call(
            body, name="rw_" + tag, grid=(t // tile,),
            out_shape=tuple(jax.ShapeDtypeStruct((t, c), F32) for c in out_cols),
            in_specs=[in_spec(kd, a) for kd, a in zip(kinds, args)],
            out_specs=tuple(pl.BlockSpec((tile, c), lambda i: (i, 0)) for c in out_cols),
            compiler_params=_cparams(("parallel",)),
        )(*args)

    def bwd_call(args, cts):
        t = cts[0].shape[0]
        didx = [i for i in range(n_in) if diff[i]]

        def body(*refs):
            i = pl.program_id(0)
            vals = [r[...] for r in refs[:n_in]]
            ct = tuple(r[...] for r in refs[n_in:n_in + len(out_cols)])
            outs = refs[n_in + len(out_cols):]

            def g(*dv):
                full = list(vals)
                for j, v in zip(didx, dv):
                    full[j] = v
                return tuple(f(*full))

            _, vjp = jax.vjp(g, *[vals[j] for j in didx])
            grads = vjp(ct)
            for j, o_ref, d in zip(didx, outs, grads):
                if kinds[j] in ("row", "pair"):
                    o_ref[...] = d
                else:
                    first = (i % nxt == 0) if kinds[j] == "grp" else (i == 0)

                    @pl.when(first)
                    def _(o_ref=o_ref, d=d):
                        o_ref[...] = d

                    @pl.when(jnp.logical_not(first))
                    def _(o_ref=o_ref, d=d):
                        o_ref[...] += d

        def out_spec(j):
            return in_spec(kinds[j], args[j])

        return pl.pallas_call(
            body, name="rw_bwd_" + tag, grid=(t // tile,),
            out_shape=tuple(jax.ShapeDtypeStruct(args[j].shape, F32) for j in didx),
            in_specs=[in_spec(kd, a) for kd, a in zip(kinds, args)]
            + [pl.BlockSpec((tile, c), lambda i: (i, 0)) for c in out_cols],
            out_specs=tuple(out_spec(j) for j in didx),
            compiler_params=_cparams(("arbitrary",)),
        )(*args, *cts)

    @jax.custom_vjp
    def op(*args):
        return fwd_call(*args)

    def fwd(*args):
        return fwd_call(*args), args

    def bwd(args, cts):
        grads = bwd_call(args, cts)
        full = [None] * n_in
        for j, gr in zip([i for i in range(n_in) if diff[i]], grads):
            full[j] = gr
        return tuple(jnp.zeros_like(a) if gfull is None else gfull for a, gfull in zip(args, full))

    op.defvjp(fwd, bwd)
    return op


def _layer_norm(z, g, b):
    mu = jnp.mean(z, axis=-1, keepdims=True)
    var = jnp.mean(jnp.square(z - mu), axis=-1, keepdims=True)
    return (z - mu) * lax.rsqrt(var + LN_EPS) * g + b


def _f_modulate(h, sc, sh):
    return (h * (1.0 + sc) + sh,)


def _f_ln_res(h, y, gate, g, b):
    return (_layer_norm(DEEPNORM_ALPHA * h + gate * y, g, b),)


def _f_ln_res_mod(h, y, gate, g, b, sc, sh):
    h1 = _layer_norm(DEEPNORM_ALPHA * h + gate * y, g, b)
    return h1, h1 * (1.0 + sc) + sh


def _f_swiglu(ab):
    half = ab.shape[1] // 2
    return (_silu(ab[:, :half]) * ab[:, half:],)


def _f_ret_combine(gates, o):
    o_f, o_b = o[0], o[1]
    hv = o_f.shape[1]
    dv = hv // RET_HEADS

    def gn(o):
        parts = []
        for h in range(RET_HEADS):
            oh = o[:, h * dv:(h + 1) * dv]
            mu = jnp.mean(oh, axis=-1, keepdims=True)
            var = jnp.mean(jnp.square(oh - mu), axis=-1, keepdims=True)
            parts.append((oh - mu) * lax.rsqrt(var + GN_EPS))
        return jnp.concatenate(parts, axis=1)

    return (_silu(gates[:, :hv]) * gn(o_f) + _silu(gates[:, hv:]) * gn(o_b),)


def _rms(x, g):
    return x * lax.rsqrt(jnp.mean(jnp.square(x), axis=-1, keepdims=True) + RMS_EPS) * g


def _f_rms(x, g):
    return (_rms(x, g),)


def _rot(x, cos, sin_signed, perm):
    swapped = jnp.dot(x, perm, precision=lax.Precision.HIGHEST, preferred_element_type=F32)
    return x * cos + swapped * sin_signed


def _f_kv_latent(ckv, g, cos, sin_signed, perm):
    lora = ckv.shape[1] - MLA_PAD
    return _rms(ckv[:, :lora], g), _rot(ckv[:, lora:], cos, sin_signed, perm)


def _f_loss(y, tgt):
    row = 0.5 * jnp.mean(jnp.square(y - tgt), axis=-1, keepdims=True)
    return (jnp.broadcast_to(row, (y.shape[0], LANE)),)


def _ret_step(q, k, v, state, lg, cos, sin, d):
    c, dk = q.shape
    half = dk // 2

    def rope(t):
        t1, t2 = t[:, :half], t[:, half:]
        return jnp.concatenate([t1 * cos - t2 * sin, t1 * sin + t2 * cos], axis=1)

    q = rope(q)
    k = rope(k * (dk ** -0.5))
    sgn = (1 - 2 * d).astype(F32)
    ii = lax.broadcasted_iota(jnp.int32, (c, c), 0).astype(F32)
    jj = lax.broadcasted_iota(jnp.int32, (c, c), 1).astype(F32)
    e = (ii - jj) * sgn
    intra = jnp.where(e >= 0, jnp.exp(lg * jnp.maximum(e, 0.0)), 0.0)
    idx = lax.broadcasted_iota(jnp.int32, (c, 1), 0).astype(F32)
    pos = jnp.where(d == 0, idx, c - 1.0 - idx)
    q_dec = jnp.exp(lg * (pos + 1.0))
    k_dec = jnp.exp(lg * (c - 1.0 - pos))
    c_dec = jnp.exp(lg * float(c))
    scores = _dot(q, k, "nt") * intra
    o = _dot(scores, v, "nn") + _dot(q * q_dec, state, "nn")
    new_state = state * c_dec + _dot(k * k_dec, v, "tn")
    return o, new_state


def _ret_chunk_of(d, p, nxc, nc):
    return (1 - d) * ((p + nxc) % nc) + d * (nc - 1 - p)


RET_GROUP = 4
RET_GROUPS = RET_HEADS // RET_GROUP


def _ret_specs(t, dk, dv, nxc, nc, step_of):
    c, g = RET_CHUNK, RET_GROUP

    def chunk(d, h, s):
        return _ret_chunk_of(d, step_of(s), nxc, nc)

    q_spec = pl.BlockSpec((c, g * dk), lambda d, h, s: (chunk(d, h, s), h))
    k_spec = pl.BlockSpec((c, g * dk), lambda d, h, s: (chunk(d, h, s), RET_GROUPS + h))
    v_spec = pl.BlockSpec((c, g * dv), lambda d, h, s: (chunk(d, h, s), (2 * RET_HEADS * dk) // (g * dv) + h))
    tab_spec = pl.BlockSpec((c, dk // 2), lambda d, h, s: (chunk(d, h, s), 0))
    lg_spec = pl.BlockSpec((g, SUBLANE, LANE), lambda d, h, s: (d * RET_GROUPS + h, 0, 0))
    o_spec = pl.BlockSpec((None, c, g * dv), lambda d, h, s: (d, chunk(d, h, s), h))
    st_spec = pl.BlockSpec((None, g, None, dk, dv), lambda d, h, s: (d, h, step_of(s), 0, 0))
    return q_spec, k_spec, v_spec, tab_spec, lg_spec, o_spec, st_spec


def _make_retention(tag, nx_rows):
    def dims(qkv):
        t, w = qkv.shape
        dk = w // (4 * RET_HEADS)
        return t, dk, 2 * dk, nx_rows // RET_CHUNK, t // RET_CHUNK

    def fwd_call(qkv, cos, sin, lgb, riders):
        t, dk, dv, nxc, nc = dims(qkv)
        q_spec, k_spec, v_spec, tab_spec, lg_spec, o_spec, st_spec = _ret_specs(t, dk, dv, nxc, nc, lambda s: s)

        def body(q_ref, k_ref, v_ref, cos_ref, sin_ref, lg_ref, o_ref, st_ref, state):
            d = pl.program_id(0)

            @pl.when(pl.program_id(2) == 0)
            def _():
                state[...] = jnp.zeros_like(state)

            st_ref[...] = state[...]
            cos_t, sin_t = cos_ref[...], sin_ref[...]
            for g in range(RET_GROUP):
                o, new_state = _ret_step(q_ref[:, g * dk:(g + 1) * dk], k_ref[:, g * dk:(g + 1) * dk],
                                         v_ref[:, g * dv:(g + 1) * dv], state[g], lg_ref[g, 0:1, 0:1],
                                         cos_t, sin_t, d)
                o_ref[:, g * dv:(g + 1) * dv] = o
                state[g] = new_state

        return _call(
            body, name="ret_" + tag, grid=(2, RET_GROUPS, nc),
            out_shape=(jax.ShapeDtypeStruct((2, t, RET_HEADS * dv), F32),
                       jax.ShapeDtypeStruct((2, RET_HEADS, nc, dk, dv), F32)),
            in_specs=[q_spec, k_spec, v_spec, tab_spec, tab_spec, lg_spec],
            out_specs=(o_spec, st_spec),
            scratch_shapes=[pltpu.VMEM((RET_GROUP, dk, dv), F32)],
            operands=(qkv, qkv, qkv, cos, sin, lgb),
            semantics=("parallel", "parallel", "arbitrary"), riders=riders)

    def bwd_call(qkv, cos, sin, lgb, states, do, riders):
        t, dk, dv, nxc, nc = dims(qkv)
        q_spec, k_spec, v_spec, tab_spec, lg_spec, o_spec, st_spec = _ret_specs(
            t, dk, dv, nxc, nc, lambda s: nc - 1 - s)
        c = RET_CHUNK

        def chunk(d, h, s):
            return _ret_chunk_of(d, nc - 1 - s, nxc, nc)

        dq_spec = pl.BlockSpec((None, c, RET_GROUP * dk), lambda d, h, s: (d, chunk(d, h, s), h))
        dv_spec = pl.BlockSpec((None, c, RET_GROUP * dv), lambda d, h, s: (d, chunk(d, h, s), h))

        def body(q_ref, k_ref, v_ref, cos_ref, sin_ref, lg_ref, st_ref, do_ref,
                 dq_ref, dk_ref, dv_ref, dlg_ref, dstate):
            d = pl.program_id(0)
            s = pl.program_id(2)

            @pl.when(s == 0)
            def _():
                dstate[...] = jnp.zeros_like(dstate)

            cos_t, sin_t = cos_ref[...], sin_ref[...]

            def step(q, k, v, state, lg):
                return _ret_step(q, k, v, state, lg, cos_t, sin_t, d)

            corner = jnp.logical_and(lax.broadcasted_iota(jnp.int32, (SUBLANE, LANE), 0) == 0,
                                     lax.broadcasted_iota(jnp.int32, (SUBLANE, LANE), 1) == 0)
            for g in range(RET_GROUP):
                qs, vs = slice(g * dk, (g + 1) * dk), slice(g * dv, (g + 1) * dv)
                _, vjp = jax.vjp(step, q_ref[:, qs], k_ref[:, qs], v_ref[:, vs], st_ref[g], lg_ref[g, 0:1, 0:1])
                dq, dkk, dvv, dst, dlg = vjp((do_ref[:, vs], dstate[g]))
                dq_ref[:, qs] = dq
                dk_ref[:, qs] = dkk
                dv_ref[:, vs] = dvv
                dstate[g] = dst
                dlg_full = jnp.where(corner, dlg, 0.0)

                @pl.when(s == 0)
                def _(g=g, dlg_full=dlg_full):
                    dlg_ref[g] = dlg_full

                @pl.when(s > 0)
                def _(g=g, dlg_full=dlg_full):
                    dlg_ref[g] += dlg_full

        return _call(
            body, name="ret_bwd_" + tag, grid=(2, RET_GROUPS, nc),
            out_shape=(jax.ShapeDtypeStruct((2, t, RET_HEADS * dk), F32),
                       jax.ShapeDtypeStruct((2, t, RET_HEADS * dk), F32),
                       jax.ShapeDtypeStruct((2, t, RET_HEADS * dv), F32),
                       jax.ShapeDtypeStruct(lgb.shape, F32)),
            in_specs=[q_spec, k_spec, v_spec, tab_spec, tab_spec, lg_spec, st_spec, o_spec],
            out_specs=(dq_spec, dq_spec, dv_spec, lg_spec),
            scratch_shapes=[pltpu.VMEM((RET_GROUP, dk, dv), F32)],
            operands=(qkv, qkv, qkv, cos, sin, lgb, states, do),
            semantics=("parallel", "parallel", "arbitrary"), riders=riders)

    def impl(qkv, cos, sin, lgb, *shards):
        return fwd_call(qkv, cos, sin, lgb, _Riders(gathers=[s.astype(BF16) for s in shards]))

    @jax.custom_vjp
    def ret(qkv, cos, sin, lgb, *shards):
        o, _, *gathered = impl(qkv, cos, sin, lgb, *shards)
        return (o, *gathered)

    def fwd(qkv, cos, sin, lgb, *shards):
        o, states, *gathered = impl(qkv, cos, sin, lgb, *shards)
        return (o, *gathered), (qkv, cos, sin, lgb, states)

    def bwd(res, cts):
        qkv, cos, sin, lgb, states = res
        dq, dkk, dvv, dlg, *exchanged = bwd_call(qkv, cos, sin, lgb, states, cts[0],
                                                 _Riders(exchanges=list(cts[1:])))
        dqkv = jnp.concatenate([dq[0] + dq[1], dkk[0] + dkk[1], dvv[0] + dvv[1]], axis=1)
        return (dqkv, jnp.zeros_like(cos), jnp.zeros_like(sin), dlg, *_sum_exchanged(exchanged, "ret_" + tag))

    ret.defvjp(fwd, bwd)
    return ret


MLA_SCALE = (MLA_D_NOPE + MLA_D_ROPE) ** -0.5
MLA_Q_ROWS = 256
MLA_KEY_BLOCKS = 2


def _attn_query(q, cos, sin_signed, perm):
    qr = _rot(q[:, MLA_D_NOPE:], cos, sin_signed, perm)
    return jnp.concatenate([q[:, :MLA_D_NOPE], qr], axis=1).astype(BF16)


def _make_attention(tag, nx_rows):
    hw = MLA_D_NOPE + MLA_PAD
    tq = MLA_Q_ROWS

    def operands(kv, kr):
        t = kv.shape[0]
        kv3 = kv.reshape(t, MLA_HEADS, MLA_D_NOPE + MLA_D_V)
        k = jnp.concatenate([kv3[..., :MLA_D_NOPE], jnp.broadcast_to(kr[:, None, :], (t, MLA_HEADS, MLA_PAD))],
                            axis=2).reshape(t, MLA_HEADS * hw).astype(BF16)
        v = kv3[..., MLA_D_NOPE:].reshape(t, MLA_HEADS * MLA_D_V).astype(BF16)
        return k, k.T, v

    def softmax_t(s_t):
        m = jnp.max(s_t, axis=0, keepdims=True)
        e = jnp.exp((s_t - m) * MLA_SCALE)
        return e, m, 1.0 / jnp.sum(e, axis=0, keepdims=True)

    def fwd_call(q, k_t, v, cos, sin_signed, perm, riders):
        t = q.shape[0]
        nctx = t - nx_rows

        def body(q_ref, cos_ref, sin_ref, perm_ref, kt_ref, v_ref, o_ref, m_ref, linv_ref):
            i = pl.program_id(1)
            qb = _attn_query(q_ref[...], cos_ref[...], sin_ref[...], perm_ref[...])

            def attend(kt_part, v_part):
                s = _raw_dot(qb, kt_part, "nn")
                m = jnp.max(s, axis=1, keepdims=True)
                e = jnp.exp((s - m) * MLA_SCALE)
                linv = 1.0 / jnp.sum(e, axis=1, keepdims=True)
                o_ref[...] = _raw_dot(e * linv, v_part, "nn")
                m_ref[...] = m
                linv_ref[...] = linv

            @pl.when(i * tq < nx_rows)
            def _():
                attend(kt_ref[...], v_ref[...])

            @pl.when(i * tq >= nx_rows)
            def _():
                attend(kt_ref[:, pl.ds(nx_rows, nctx)], v_ref[pl.ds(nx_rows, nctx), :])

        stat = jax.ShapeDtypeStruct((MLA_HEADS, t, 1), F32)
        stat_spec = pl.BlockSpec((None, tq, 1), lambda h, i: (h, i, 0))
        o, m, linv, *gathered = _call(
            body, name="mla_" + tag, grid=(MLA_HEADS, t // tq),
            out_shape=(jax.ShapeDtypeStruct((t, MLA_HEADS * MLA_D_V), F32), stat, stat),
            in_specs=[pl.BlockSpec((tq, hw), lambda h, i: (i, h)),
                      pl.BlockSpec((tq, MLA_PAD), lambda h, i: (i, 0)),
                      pl.BlockSpec((tq, MLA_PAD), lambda h, i: (i, 0)),
                      pl.BlockSpec((MLA_PAD, MLA_PAD), lambda h, i: (0, 0)),
                      pl.BlockSpec((hw, t), lambda h, i: (h, 0)),
                      pl.BlockSpec((t, MLA_D_V), lambda h, i: (0, h))],
            out_specs=(pl.BlockSpec((tq, MLA_D_V), lambda h, i: (i, h)), stat_spec, stat_spec),
            operands=(q, cos, sin_signed, perm, k_t, v), semantics=("parallel", "arbitrary"), riders=riders)
        return (o, m.reshape(MLA_HEADS, 1, t), linv.reshape(MLA_HEADS, 1, t), *gathered)

    def bwd_call(q, k, k_t, v, cos, sin_signed, perm, o, m, linv, do, riders):
        t = q.shape[0]
        nctx = t - nx_rows

        def body(q_ref, cos_ref, sin_ref, perm_ref, k_ref, kt_ref, v_ref, o_ref, m_ref, linv_ref, do_ref,
                 dq_ref, dkn_ref, dv_ref, dkr_ref):
            i = pl.program_id(1)
            cos_t, sin_t, perm_t = cos_ref[...], sin_ref[...], perm_ref[...]
            qb = _attn_query(q_ref[...], cos_t, sin_t, perm_t)
            do_t = do_ref[...]
            dob = do_t.astype(BF16)
            delta = jnp.sum((do_t * o_ref[...]).T, axis=0, keepdims=True)
            m_t, linv_t = m_ref[...], linv_ref[...]

            def key_rows(rows):
                e = jnp.exp((_raw_dot(k_ref[rows, :], qb, "nt") - m_t) * MLA_SCALE)
                p_t = e * linv_t
                dp_t = _raw_dot(v_ref[rows, :], dob, "nt")
                ds_t = (p_t * (dp_t - delta) * MLA_SCALE).astype(BF16)
                dk = _raw_dot(ds_t, qb, "nn")
                dkn_ref[rows, :] += dk[:, :MLA_D_NOPE]
                dkr_ref[rows, :] += dk[:, MLA_D_NOPE:]
                dv_ref[rows, :] += _raw_dot(p_t, dob, "nn")
                return _raw_dot(kt_ref[:, rows], ds_t, "nn")

            def finish(dq_t):
                dqb = dq_t.T
                g_r = dqb[:, MLA_D_NOPE:]
                dq_r = g_r * cos_t + jnp.dot(g_r * sin_t, perm_t, precision=lax.Precision.HIGHEST,
                                             preferred_element_type=F32)
                dq_ref[...] = jnp.concatenate([dqb[:, :MLA_D_NOPE], dq_r], axis=1)

            @pl.when(i == 0)
            def _():
                dkn_ref[...] = jnp.zeros_like(dkn_ref)
                dv_ref[...] = jnp.zeros_like(dv_ref)

            @pl.when(jnp.logical_and(i == 0, pl.program_id(0) == 0))
            def _():
                dkr_ref[...] = jnp.zeros_like(dkr_ref)

            @pl.when(i * tq < nx_rows)
            def _():
                size = t // MLA_KEY_BLOCKS
                dq_t = key_rows(pl.ds(0, size))
                for b in range(1, MLA_KEY_BLOCKS):
                    dq_t = dq_t + key_rows(pl.ds(b * size, size))
                finish(dq_t)

            @pl.when(i * tq >= nx_rows)
            def _():
                finish(key_rows(pl.ds(nx_rows, nctx)))

        stat_spec = pl.BlockSpec((None, 1, tq), lambda h, i: (h, 0, i))
        return _call(
            body, name="mla_bwd_" + tag, grid=(MLA_HEADS, t // tq),
            out_shape=(jax.ShapeDtypeStruct(q.shape, F32),
                       jax.ShapeDtypeStruct((t, MLA_HEADS * MLA_D_NOPE), F32),
                       jax.ShapeDtypeStruct((t, MLA_HEADS * MLA_D_V), F32),
                       jax.ShapeDtypeStruct((t, MLA_PAD), F32)),
            in_specs=[pl.BlockSpec((tq, hw), lambda h, i: (i, h)),
                      pl.BlockSpec((tq, MLA_PAD), lambda h, i: (i, 0)),
                      pl.BlockSpec((tq, MLA_PAD), lambda h, i: (i, 0)),
                      pl.BlockSpec((MLA_PAD, MLA_PAD), lambda h, i: (0, 0)),
                      pl.BlockSpec((t, hw), lambda h, i: (0, h)),
                      pl.BlockSpec((hw, t), lambda h, i: (h, 0)),
                      pl.BlockSpec((t, MLA_D_V), lambda h, i: (0, h)),
                      pl.BlockSpec((tq, MLA_D_V), lambda h, i: (i, h)),
                      stat_spec, stat_spec,
                      pl.BlockSpec((tq, MLA_D_V), lambda h, i: (i, h))],
            out_specs=(pl.BlockSpec((tq, hw), lambda h, i: (i, h)),
                       pl.BlockSpec((t, MLA_D_NOPE), lambda h, i: (0, h)),
                       pl.BlockSpec((t, MLA_D_V), lambda h, i: (0, h)),
                       pl.BlockSpec((t, MLA_PAD), lambda h, i: (0, 0))),
            operands=(q, cos, sin_signed, perm, k, k_t, v, o, m, linv, do),
            semantics=("arbitrary", "arbitrary"), riders=riders)

    def impl(q, kv, kr, cos, sin_signed, perm, *shards):
        k, k_t, v = operands(kv, kr)
        o, m, linv, *gathered = fwd_call(q, k_t, v, cos, sin_signed, perm,
                                         _Riders(gathers=[s.astype(BF16) for s in shards]))
        return (o, *gathered), (q, k, k_t, v, cos, sin_signed, perm, o, m, linv)

    @jax.custom_vjp
    def attn(q, kv, kr, cos, sin_signed, perm, *shards):
        return impl(q, kv, kr, cos, sin_signed, perm, *shards)[0]

    def fwd(q, kv, kr, cos, sin_signed, perm, *shards):
        return impl(q, kv, kr, cos, sin_signed, perm, *shards)

    def bwd(res, cts):
        q, k, k_t, v, cos, sin_signed, perm, o, m, linv = res
        dq, dkn, dv, dkr, *exchanged = bwd_call(q, k, k_t, v, cos, sin_signed, perm, o, m, linv, cts[0],
                                                _Riders(exchanges=list(cts[1:])))
        t = q.shape[0]
        dkv = jnp.concatenate([dkn.reshape(t, MLA_HEADS, MLA_D_NOPE), dv.reshape(t, MLA_HEADS, MLA_D_V)],
                              axis=2).reshape(t, -1)
        return (dq, dkv, dkr, jnp.zeros_like(cos), jnp.zeros_like(sin_signed), jnp.zeros_like(perm),
                *_sum_exchanged(exchanged, "mla_" + tag))

    attn.defvjp(fwd, bwd)
    return attn


def _adamw_call(w, g, m, v, name):
    rows, cols = w.shape
    tr = _div(rows, max(SUBLANE, (1 << 18) // cols), SUBLANE) if rows % SUBLANE == 0 else rows

    def body(w_ref, g_ref, m_ref, v_ref, d_ref, nm_ref, nv_ref):
        gg = g_ref[...]
        nm = ADAM_B1 * m_ref[...] + (1.0 - ADAM_B1) * gg
        nv = ADAM_B2 * v_ref[...] + (1.0 - ADAM_B2) * jnp.square(gg)
        m_hat = nm / (1.0 - ADAM_B1 ** ADAM_STEP)
        v_hat = nv / (1.0 - ADAM_B2 ** ADAM_STEP)
        d_ref[...] = -ADAM_LR * (m_hat / (jnp.sqrt(v_hat) + ADAM_EPS) + ADAM_WD * w_ref[...])
        nm_ref[...] = nm
        nv_ref[...] = nv

    spec = pl.BlockSpec((tr, cols), lambda i: (i, 0))
    return pl.pallas_call(
        body, name=name, grid=(rows // tr,),
        out_shape=tuple(jax.ShapeDtypeStruct((rows, cols), F32) for _ in range(3)),
        in_specs=[spec] * 4, out_specs=(spec,) * 3,
        compiler_params=_cparams(("parallel",)),
    )(w, g, m, v)


def _adamw(w, g, m, v, name):
    shape = w.shape
    if w.ndim >= 2 and shape[-1] % LANE == 0:
        two = (-1, shape[-1])
    elif w.size % LANE == 0:
        two = (-1, LANE)
    else:
        two = (1, w.size)
    outs = _adamw_call(w.reshape(two), g.reshape(two), m.reshape(two), v.reshape(two), name)
    return tuple(o.reshape(shape) for o in outs)


def _ret_tables(nx, nctx, dk):
    inv = RET_ROPE_BASE ** (-jnp.linspace(0.0, 1.0, dk // 2, dtype=F32))
    ang = jnp.arange(nx, dtype=F32)[:, None] * inv[None, :]
    cos = jnp.concatenate([jnp.cos(ang), jnp.ones((nctx, dk // 2), F32)], axis=0)
    sin = jnp.concatenate([jnp.sin(ang), jnp.zeros((nctx, dk // 2), F32)], axis=0)
    return cos, sin


def _mla_tables(nx, nctx):
    quarter = MLA_D_ROPE // 4
    inv = AXIAL_ROPE_BASE ** (-jnp.arange(quarter, dtype=F32) * 2.0 / (MLA_D_ROPE // 2))
    tok = jnp.arange(nx)
    r_ang = (tok // GRID_W).astype(F32)[:, None] * inv[None, :]
    c_ang = (tok % GRID_W).astype(F32)[:, None] * inv[None, :]
    pad = MLA_PAD - MLA_D_ROPE
    cos = jnp.concatenate([jnp.cos(r_ang), jnp.cos(r_ang), jnp.cos(c_ang), jnp.cos(c_ang),
                           jnp.ones((nx, pad), F32)], axis=1)
    sin = jnp.concatenate([-jnp.sin(r_ang), jnp.sin(r_ang), -jnp.sin(c_ang), jnp.sin(c_ang),
                           jnp.zeros((nx, pad), F32)], axis=1)
    cos = jnp.concatenate([cos, jnp.ones((nctx, MLA_PAD), F32)], axis=0)
    sin = jnp.concatenate([sin, jnp.zeros((nctx, MLA_PAD), F32)], axis=0)
    lane = jnp.arange(MLA_PAD)
    partner = jnp.where(lane % (2 * quarter) < quarter, lane + quarter, lane - quarter)
    perm = ((lane[:, None] == partner[None, :]) & (lane[None, :] < MLA_D_ROPE)).astype(F32)
    return cos, sin, perm


def _columns(gathered):
    return gathered


def _rows(gathered):
    return gathered.reshape(1, gathered.shape[0] * gathered.shape[1], gathered.shape[2])


def _unshard_vec(gathered):
    return jnp.moveaxis(gathered, 0, 1).reshape(gathered.shape[1], -1)


def _pad_rows(a, rows):
    return jnp.concatenate([a, jnp.zeros((rows - a.shape[0],) + a.shape[1:], a.dtype)], axis=0)


def _loss_fn(weights, x, c_all, ctx, tgt, me):
    nx, d_model = x.shape
    nctx = ctx.shape[0]
    dk = d_model // RET_HEADS
    ret_cos, ret_sin = _ret_tables(nx, nctx, dk)
    mla_cos, mla_sin, mla_perm = _mla_tables(nx, nctx)

    rw = functools.partial(_make_rowwise, nx_rows=nx)

    n_ln = weights["ln_g"].size
    small = jnp.concatenate([weights["ln_g"].reshape(-1), weights["ln_b"].reshape(-1),
                             weights["mla_g_q"].reshape(-1), weights["mla_g_kv"].reshape(-1)])
    n_small = small.size
    small = _pad_rows(small.reshape(-1, 1), -(-n_small // LANE) * LANE).reshape(-1, LANE)
    small = _make_gather("small", F32)(small).reshape(N_DEV, -1)
    ln_g = _unshard_vec(small[:, :n_ln].reshape(N_DEV, DEPTH * 2, -1)).reshape(DEPTH, 2, d_model)
    ln_b = _unshard_vec(small[:, n_ln:2 * n_ln].reshape(N_DEV, DEPTH * 2, -1)).reshape(DEPTH, 2, d_model)
    n_g = weights["mla_g_q"].size
    g_q = _unshard_vec(small[:, 2 * n_ln:2 * n_ln + n_g].reshape(N_DEV, DEPTH // 2, -1))
    g_kv = _unshard_vec(small[:, 2 * n_ln + n_g:2 * n_ln + 2 * n_g].reshape(N_DEV, DEPTH // 2, -1))

    n_b = weights["ada_b"].size
    n_dec = weights["ret_decay_logit"].size
    repl = jnp.concatenate([weights["c_ctx"].reshape(-1), weights["ada_b"].reshape(-1),
                            weights["ret_decay_logit"].reshape(-1)])
    n_repl = repl.size
    repl = _pad_rows(repl.reshape(-1, 1), -(-n_repl // LANE) * LANE).reshape(-1, LANE)
    repl = _make_replicated("repl")(repl).reshape(-1)
    c_ctx = repl[:d_model]
    ada_b = repl[d_model:d_model + n_b].reshape(DEPTH, 6 * d_model)
    decay = repl[d_model + n_b:d_model + n_b + n_dec].reshape(weights["ret_decay_logit"].shape)

    cond_rows = 2 * SUBLANE
    cond = _pad_rows(jnp.concatenate([c_all, c_ctx[None, :]], axis=0), cond_rows)
    s_cond = _make_rowwise_plain_silu(cond)
    mods = [_make_mm("ada%d" % i)(s_cond, weights["ada_w"][i][None])[0] for i in range(DEPTH)]
    mod = jnp.concatenate(mods, axis=0)
    mod = _make_gather("mod", F32)(mod)
    mod = jnp.moveaxis(mod.reshape(N_DEV, DEPTH, cond_rows, -1), 0, 2).reshape(DEPTH, cond_rows, -1)
    mod = mod + ada_b[:, None, :]
    mine = lax.dynamic_slice_in_dim(mod, me, 1, axis=1)
    mod = jnp.concatenate([mine, mod[:, N_DEV:N_DEV + 1]], axis=1)
    mod = mod.reshape(DEPTH, 2, 6, 1, d_model)

    def grp(i, j):
        return mod[i, :, j]

    h = jnp.concatenate([x, ctx], axis=0)
    ahead = {"ret_qkv": _make_gather("ret_qkv", BF16)(weights["ret_w_qkv"][0])}
    for i in range(DEPTH):
        j = i // 2
        retention_layer, last = i % 2 == 0, i + 1 == DEPTH
        (u,) = rw(_f_modulate, ("row", "grp", "grp"), (True, True, True), (d_model,), "mod%d" % i)(
            h, grp(i, 1), grp(i, 0))
        if retention_layer:
            lg = jax.nn.log_sigmoid(decay[j]).reshape(2 * RET_HEADS, 1, 1)
            lgb = jnp.broadcast_to(lg, (2 * RET_HEADS, SUBLANE, LANE))
            if "ret_g" in ahead:
                qkv, w_in = _make_mm("ret_qkv%d" % j, 1)(u, _columns(ahead.pop("ret_qkv")), weights["ffn_w_in"][i])
                gates, w_out = _make_mm("ret_g%d" % j, 1)(u, _columns(ahead.pop("ret_g")), weights["ffn_w_out"][i])
                w_o = ahead.pop("ret_o")
                (o,) = _make_retention("l%d" % j, nx)(qkv, ret_cos, ret_sin, lgb)
            else:
                qkv, w_g = _make_mm("ret_qkv%d" % j, 1)(u, _columns(ahead.pop("ret_qkv")), weights["ret_w_g"][j])
                gates, w_o, w_in = _make_mm("ret_g%d" % j, 2)(u, _columns(w_g), weights["ret_w_o"][j],
                                                              weights["ffn_w_in"][i])
                o, w_out = _make_retention("l%d" % j, nx)(qkv, ret_cos, ret_sin, lgb, weights["ffn_w_out"][i])
            (comb,) = rw(_f_ret_combine, ("row", "pair"), (True, True), (o.shape[2],),
                         "ret_comb%d" % j, tile=WIDE_ROW_TILE)(gates, o)
            nxt = [] if last else [weights[n][j] for n in MLA_MATRICES]
            y, *gathered = _make_mm("ret_o%d" % j, len(nxt))(comb, _rows(w_o), *nxt)
            if nxt:
                ahead["mla"] = gathered
        else:
            nxt = [] if last else [weights[n][j + 1] for n in ("ret_w_qkv", "ret_w_g", "ret_w_o")]
            y, *gathered = _mla_mixer(ahead.pop("mla"), j, u, g_q[j], g_kv[j], mla_cos, mla_sin, mla_perm, rw, nx, nxt)
            if nxt:
                ahead["ret_qkv"], ahead["ret_g"], ahead["ret_o"] = gathered
            w_in, w_out = ahead.pop("ffn_in"), ahead.pop("ffn_out")
        h1, u2 = rw(_f_ln_res_mod, ("row", "row", "grp", "par", "par", "grp", "grp"), (True,) * 7,
                    (d_model, d_model), "ln_a%d" % i)(
            h, y, grp(i, 2), ln_g[i, 0][None], ln_b[i, 0][None], grp(i, 4), grp(i, 3))
        if retention_layer and not last:
            ab, ahead["ffn_in"] = _make_mm("ffn_in%d" % i, 1)(u2, _columns(w_in), weights["ffn_w_in"][i + 1])
        else:
            (ab,) = _make_mm("ffn_in%d" % i)(u2, _columns(w_in))
        (act,) = rw(_f_swiglu, ("row",), (True,), (ab.shape[1] // 2,), "swiglu%d" % i, tile=WIDE_ROW_TILE)(ab)
        if retention_layer and not last:
            f, ahead["ffn_out"] = _make_mm("ffn_out%d" % i, 1)(act, _rows(w_out), weights["ffn_w_out"][i + 1])
        else:
            (f,) = _make_mm("ffn_out%d" % i)(act, _rows(w_out))
        (h,) = rw(_f_ln_res, ("row", "row", "grp", "par", "par"), (True,) * 5, (d_model,), "ln_f%d" % i)(
            h1, f, grp(i, 5), ln_g[i, 1][None], ln_b[i, 1][None])

    (rows,) = rw(_f_loss, ("row", "row"), (True, False), (LANE,), "loss")(h[:nx], tgt)
    return jnp.sum(rows[:, 0])


def _make_rowwise_plain_silu(cond):
    def body(c_ref, o_ref):
        o_ref[...] = _silu(c_ref[...])

    def call(c):
        return pl.pallas_call(body, name="silu_cond", out_shape=jax.ShapeDtypeStruct(c.shape, F32))(c)

    def bwd_body(c_ref, g_ref, o_ref):
        _, vjp = jax.vjp(_silu, c_ref[...])
        o_ref[...] = vjp(g_ref[...])[0]

    @jax.custom_vjp
    def op(c):
        return call(c)

    def fwd(c):
        return call(c), c

    def bwd(c, g):
        return (pl.pallas_call(bwd_body, name="silu_cond_bwd", out_shape=jax.ShapeDtypeStruct(c.shape, F32))(c, g),)

    op.defvjp(fwd, bwd)
    return op(cond)


MLA_MATRICES = ("mla_w_dq", "mla_w_uq", "mla_w_dkv", "mla_w_ukv", "mla_w_o")


def _mla_mixer(gathered, j, u, g_q, g_kv, cos, sin_signed, perm, rw, nx, ride):
    heads, dn, dr = MLA_HEADS, MLA_D_NOPE, MLA_D_ROPE
    w_dq, w_uq, w_dkv, w_ukv, w_o = gathered
    w_dq, w_dkv, w_ukv, w_o = _rows(w_dq), _rows(w_dkv), _columns(w_ukv), _rows(w_o)

    q_lora = w_uq.shape[1]
    w_uq = jnp.moveaxis(w_uq, 0, 1).reshape(q_lora, heads, dn + dr)
    w_uq = jnp.concatenate([w_uq, jnp.zeros((q_lora, heads, MLA_PAD - dr), w_uq.dtype)], axis=2)
    w_uq = w_uq.reshape(1, q_lora, heads * (dn + MLA_PAD))
    w_dkv = jnp.concatenate([w_dkv, jnp.zeros(w_dkv.shape[:2] + (MLA_PAD - dr,), w_dkv.dtype)], axis=2)

    (cq,) = _make_mm("mla_dq%d" % j)(u, w_dq)
    (cqn,) = rw(_f_rms, ("row", "par"), (True, True), (cq.shape[1],), "rms_q%d" % j)(cq, g_q[None])
    (q,) = _make_mm("mla_uq%d" % j)(cqn, w_uq)
    (ckv,) = _make_mm("mla_dkv%d" % j)(u, w_dkv)
    lora = ckv.shape[1] - MLA_PAD
    c_kv, kr = rw(_f_kv_latent, ("row", "par", "row", "row", "par"), (True, True, False, False, False),
                  (lora, MLA_PAD), "kv_lat%d" % j)(ckv, g_kv[None], cos, sin_signed, perm)
    (kv,) = _make_mm("mla_ukv%d" % j)(c_kv, w_ukv)
    o, *rode = _make_attention("l%d" % j, nx)(q, kv, kr, cos, sin_signed, perm, *ride)
    return (_make_mm("mla_o%d" % j)(o, w_o)[0], *rode)


WEIGHT_NAMES = ("c_ctx", "ada_w", "ada_b", "ln_g", "ln_b", "ret_w_qkv", "ret_w_g", "ret_decay_logit", "ret_w_o",
                "mla_w_dq", "mla_g_q", "mla_w_uq", "mla_w_dkv", "mla_g_kv", "mla_w_ukv", "mla_w_o",
                "ffn_w_in", "ffn_w_out")


def kernel(x, c, ctx, c_ctx, ada_w, ada_b, ln_g, ln_b, ret_w_qkv, ret_w_g, ret_decay_logit, ret_w_o, mla_w_dq, mla_g_q, mla_w_uq, mla_w_dkv, mla_g_kv, mla_w_ukv, mla_w_o, ffn_w_in, ffn_w_out, loss_target, m_c_ctx, m_ada_w, m_ada_b, m_ln_g, m_ln_b, m_ret_w_qkv, m_ret_w_g, m_ret_decay_logit, m_ret_w_o, m_mla_w_dq, m_mla_g_q, m_mla_w_uq, m_mla_w_dkv, m_mla_g_kv, m_mla_w_ukv, m_mla_w_o, m_ffn_w_in, m_ffn_w_out, v_c_ctx, v_ada_w, v_ada_b, v_ln_g, v_ln_b, v_ret_w_qkv, v_ret_w_g, v_ret_decay_logit, v_ret_w_o, v_mla_w_dq, v_mla_g_q, v_mla_w_uq, v_mla_w_dkv, v_mla_g_kv, v_mla_w_ukv, v_mla_w_o, v_ffn_w_in, v_ffn_w_out):
    given = dict(locals())
    weights = {n: given[n] for n in WEIGHT_NAMES}
    me = _my_index()
    c_all = _all_gather_call(c, "ag_cond").reshape(N_DEV, -1)

    def loss_of(wts, x2):
        return _loss_fn(wts, x2, c_all, ctx[0], loss_target[0], me)

    loss, (grad_w, grad_x) = jax.value_and_grad(loss_of, argnums=(0, 1))(weights, x[0])
    loss = lax.psum(loss, AXES)
    delta, new_m, new_v = {}, {}, {}
    for n in WEIGHT_NAMES:
        delta[n], new_m[n], new_v[n] = _adamw(weights[n], grad_w[n], given["m_" + n], given["v_" + n], "adamw_" + n)
    return (loss, grad_x[None], *[grad_w[n] for n in WEIGHT_NAMES], *[delta[n] for n in WEIGHT_NAMES],
            *[new_m[n] for n in WEIGHT_NAMES], *[new_v[n] for n in WEIGHT_NAMES])
```

```python
import functools
import math

import jax
import jax.numpy as jnp
from jax import lax
from jax.experimental import pallas as pl
from jax.experimental.pallas import tpu as pltpu

F32 = jnp.float32
BF16 = jnp.bfloat16

AXES = ("x", "y", "c")
N_DEV = 8
MESH_IDS = pl.DeviceIdType.MESH

DEPTH = 4
GRID_W = 64
RET_HEADS = 8
RET_CHUNK = 128
RET_ROPE_BASE = 10000.0
GN_EPS = 1e-6
MLA_HEADS = 16
MLA_D_NOPE = 128
MLA_D_ROPE = 64
MLA_D_V = 128
AXIAL_ROPE_BASE = 10000.0
RMS_EPS = 1e-6
LN_EPS = 1e-5
DEEPNORM_ALPHA = (2 * DEPTH) ** 0.25

ADAM_LR = 0.001
ADAM_B1 = 0.9
ADAM_B2 = 0.999
ADAM_EPS = 1e-08
ADAM_WD = 0.01
ADAM_STEP = 10

LANE = 128
SUBLANE = 8
VMEM_LIMIT = 56 * 1024 * 1024

ROW_TILE = 256
WIDE_ROW_TILE = 64
MM_ROWS = 544
MLA_PAD = 128


def _div(n, cap, mult):
    best = None
    for d in range(mult, min(n, cap) + 1, mult):
        if n % d == 0:
            best = d
    return n if best is None else best


def _cparams(sem=None):
    kw = dict(vmem_limit_bytes=VMEM_LIMIT)
    if sem is not None:
        kw["dimension_semantics"] = sem
    return pltpu.CompilerParams(**kw)


_DN = {"nn": (((1,), (0,)), ((), ())), "nt": (((1,), (1,)), ((), ())), "tn": (((0,), (0,)), ((), ()))}


def _raw_dot(a, b, kind):
    return lax.dot_general(a.astype(BF16), b.astype(BF16), _DN[kind], preferred_element_type=F32)


@functools.partial(jax.custom_vjp, nondiff_argnums=(2,))
def _dot(a, b, kind):
    return _raw_dot(a, b, kind)


def _dot_fwd(a, b, kind):
    return _raw_dot(a, b, kind), (a, b)


def _dot_bwd(kind, res, g):
    a, b = res
    if kind == "nn":
        return _raw_dot(g, b, "nt"), _raw_dot(a, g, "tn")
    if kind == "nt":
        return _raw_dot(g, b, "nn"), _raw_dot(g, a, "tn")
    return _raw_dot(b, g, "nt"), _raw_dot(a, g, "nn")


_dot.defvjp(_dot_fwd, _dot_bwd)


def _sigmoid(x):
    return 1.0 / (1.0 + jnp.exp(-x))


def _silu(x):
    return x * _sigmoid(x)


def _my_index():
    return 4 * lax.axis_index("x") + 2 * lax.axis_index("y") + lax.axis_index("c")


def _all_gather_call(x, name):
    def body(x_ref, out_ref, send_sems, recv_sems, local_sem):
        x_, y_, c_ = lax.axis_index("x"), lax.axis_index("y"), lax.axis_index("c")
        me, sibling = (x_, y_, c_), (x_, y_, 1 - c_)
        chips = [(1 - x_, y_), (x_, 1 - y_), (1 - x_, 1 - y_)]

        def slot(px, py, pc):
            return out_ref.at[4 * px + 2 * py + pc]

        def copy(k, block, to, src=None):
            return pltpu.make_async_remote_copy(
                src_ref=slot(*block) if src is None else src, dst_ref=slot(*block),
                send_sem=send_sems.at[k], recv_sem=recv_sems.at[k],
                device_id=to, device_id_type=MESH_IDS)

        mine = pltpu.make_async_copy(x_ref, slot(*me), local_sem)
        mine.start()
        first = [copy(0, me, sibling, src=x_ref)]
        first += [copy(1 + j, me, (*chip, c_), src=x_ref) for j, chip in enumerate(chips)]
        for cp in first:
            cp.start()
        passed = [copy(4 + j, (*chip, c_), sibling) for j, chip in enumerate(chips)]
        for j, chip in enumerate(chips):
            copy(1 + j, (*chip, c_), me).wait_recv()
            passed[j].start()
        copy(0, sibling, me).wait_recv()
        for j, chip in enumerate(chips):
            copy(4 + j, (*chip, 1 - c_), me).wait_recv()
        for cp in first + passed:
            cp.wait_send()
        mine.wait()

    return pl.pallas_call(
        body, name=name,
        out_shape=jax.ShapeDtypeStruct((N_DEV,) + x.shape, x.dtype),
        in_specs=[pl.BlockSpec(memory_space=pl.ANY)],
        out_specs=pl.BlockSpec(memory_space=pl.ANY),
        scratch_shapes=[pltpu.SemaphoreType.DMA((7,)), pltpu.SemaphoreType.DMA((7,)), pltpu.SemaphoreType.DMA],
    )(x)


def _all_to_all_call(g, name):
    def body(g_ref, out_ref, send_sems, recv_sems, local_sem):
        x_, y_, c_ = lax.axis_index("x"), lax.axis_index("y"), lax.axis_index("c")
        my = 4 * x_ + 2 * y_ + c_
        peers = []
        for k in range(1, N_DEV):
            px = 1 - x_ if (k >> 2) & 1 else x_
            py = 1 - y_ if (k >> 1) & 1 else y_
            pc = 1 - c_ if k & 1 else c_
            peers.append((px, py, pc))

        def copy(k, peer):
            pid = 4 * peer[0] + 2 * peer[1] + peer[2]
            return pltpu.make_async_remote_copy(
                src_ref=g_ref.at[pid], dst_ref=out_ref.at[my],
                send_sem=send_sems.at[k], recv_sem=recv_sems.at[k],
                device_id=peer, device_id_type=MESH_IDS)

        def arrival(k, peer):
            pid = 4 * peer[0] + 2 * peer[1] + peer[2]
            return pltpu.make_async_remote_copy(
                src_ref=g_ref.at[pid], dst_ref=out_ref.at[pid],
                send_sem=send_sems.at[k], recv_sem=recv_sems.at[k],
                device_id=peer, device_id_type=MESH_IDS)

        mine = pltpu.make_async_copy(g_ref.at[my], out_ref.at[my], local_sem)
        mine.start()
        sends = [copy(k, peer) for k, peer in enumerate(peers)]
        for cp in sends:
            cp.start()
        for k, peer in enumerate(peers):
            arrival(k, peer).wait_recv()
        for cp in sends:
            cp.wait_send()
        mine.wait()

    return pl.pallas_call(
        body, name=name,
        out_shape=jax.ShapeDtypeStruct(g.shape, g.dtype),
        in_specs=[pl.BlockSpec(memory_space=pl.ANY)],
        out_specs=pl.BlockSpec(memory_space=pl.ANY),
        scratch_shapes=[pltpu.SemaphoreType.DMA((7,)), pltpu.SemaphoreType.DMA((7,)), pltpu.SemaphoreType.DMA],
    )(g)


def _sum_slots_call(g, name):
    slots, rows, cols = g.shape
    tr = _div(rows, max(SUBLANE, (1 << 19) // cols), 16) if rows % 16 == 0 else rows

    def body(g_ref, o_ref):
        acc = g_ref[0].astype(F32)
        for s in range(1, slots):
            acc = acc + g_ref[s].astype(F32)
        o_ref[...] = acc

    return pl.pallas_call(
        body, name=name, grid=(rows // tr,),
        out_shape=jax.ShapeDtypeStruct((rows, cols), F32),
        in_specs=[pl.BlockSpec((slots, tr, cols), lambda i: (0, i, 0))],
        out_specs=pl.BlockSpec((tr, cols), lambda i: (i, 0)),
        compiler_params=_cparams(("parallel",)),
    )(g)


def _pair_sum_call(g, half, name):
    _, rows, cols = g.shape
    tr = _div(rows, max(SUBLANE, (1 << 20) // cols), 16) if rows % 16 == 0 else rows
    core = lax.axis_index("c").astype(jnp.int32).reshape(1)

    def body(core_ref, g_ref, h_ref, o_ref):
        del core_ref
        o_ref[...] = (g_ref[...].astype(F32) + h_ref[...].astype(F32)).astype(o_ref.dtype)

    return pl.pallas_call(
        body, name=name,
        grid_spec=pltpu.PrefetchScalarGridSpec(
            num_scalar_prefetch=1, grid=(N_DEV // 2, rows // tr),
            in_specs=[pl.BlockSpec((None, tr, cols), lambda k, i, core_ref: (2 * k + core_ref[0], i, 0)),
                      pl.BlockSpec((None, tr, cols), lambda k, i, core_ref: (k, i, 0))],
            out_specs=pl.BlockSpec((None, tr, cols), lambda k, i, core_ref: (k, i, 0))),
        out_shape=jax.ShapeDtypeStruct(half.shape, half.dtype),
        compiler_params=_cparams(("parallel", "parallel")),
    )(core, g, half)


def _make_gather(tag, travel_dtype):
    @jax.custom_vjp
    def gather(w):
        return _all_gather_call(w.astype(travel_dtype), "ag_" + tag)

    def fwd(w):
        return gather(w), None

    def bwd(_, g):
        got = _all_to_all_call(g.astype(travel_dtype), "a2a_" + tag)
        return (_sum_slots_call(got, "sum_" + tag),)

    gather.defvjp(fwd, bwd)
    return gather


def _make_replicated(tag):
    @jax.custom_vjp
    def rep(p):
        return p

    def fwd(p):
        return p, None

    def bwd(_, g):
        return (_sum_slots_call(_all_gather_call(g, "ag_" + tag), "sum_" + tag),)

    rep.defvjp(fwd, bwd)
    return rep


class _Riders:
    def __init__(self, gathers=(), exchanges=(), sibling_swaps=(), chip_exchanges=()):
        self.gathers = list(gathers)
        self.exchanges = list(exchanges)
        self.sibling_swaps = list(sibling_swaps)
        self.chip_exchanges = list(chip_exchanges)
        self.n = len(self.operands())

    def operands(self):
        return self.gathers + self.exchanges + self.sibling_swaps + self.chip_exchanges

    def out_shape(self):
        return ([jax.ShapeDtypeStruct((N_DEV,) + g.shape, g.dtype) for g in self.gathers]
                + [jax.ShapeDtypeStruct(e.shape, e.dtype) for e in self.exchanges]
                + [jax.ShapeDtypeStruct((N_DEV // 2,) + s.shape[1:], s.dtype) for s in self.sibling_swaps]
                + [jax.ShapeDtypeStruct(p.shape, p.dtype) for p in self.chip_exchanges])

    def scratch(self):
        return [pltpu.SemaphoreType.DMA((7 * self.n,)), pltpu.SemaphoreType.DMA((7 * self.n,)),
                pltpu.SemaphoreType.DMA((self.n,))]

    def _gather_phase(self, phase, r, x_ref, out_ref, send_sems, recv_sems, local_sem):
        x_, y_, c_ = lax.axis_index("x"), lax.axis_index("y"), lax.axis_index("c")
        me, sibling = (x_, y_, c_), (x_, y_, 1 - c_)
        chips = [(1 - x_, y_), (x_, 1 - y_), (1 - x_, 1 - y_)]

        def slot(px, py, pc):
            return out_ref.at[4 * px + 2 * py + pc]

        def copy(k, block, to, src=None):
            return pltpu.make_async_remote_copy(
                src_ref=slot(*block) if src is None else src, dst_ref=slot(*block),
                send_sem=send_sems.at[7 * r + k], recv_sem=recv_sems.at[7 * r + k],
                device_id=to, device_id_type=MESH_IDS)

        def mine():
            return pltpu.make_async_copy(x_ref, slot(*me), local_sem.at[r])

        def first():
            return [copy(0, me, sibling, src=x_ref)] + [copy(1 + j, me, (*chip, c_), src=x_ref)
                                                        for j, chip in enumerate(chips)]

        def passed(j):
            return copy(4 + j, (*chips[j], c_), sibling)

        if phase == "start":
            mine().start()
            for cp in first():
                cp.start()
        elif phase == "pass":
            for j, chip in enumerate(chips):
                copy(1 + j, (*chip, c_), me).wait_recv()
                passed(j).start()
        else:
            copy(0, sibling, me).wait_recv()
            for j, chip in enumerate(chips):
                copy(4 + j, (*chip, 1 - c_), me).wait_recv()
            for cp in first() + [passed(j) for j in range(3)]:
                cp.wait_send()
            mine().wait()

    def _exchange_phase(self, phase, r, g_ref, out_ref, send_sems, recv_sems, local_sem):
        x_, y_, c_ = lax.axis_index("x"), lax.axis_index("y"), lax.axis_index("c")
        my = 4 * x_ + 2 * y_ + c_
        peers = []
        for k in range(1, N_DEV):
            peers.append((1 - x_ if (k >> 2) & 1 else x_, 1 - y_ if (k >> 1) & 1 else y_, 1 - c_ if k & 1 else c_))

        def copy(k, peer, arriving):
            pid = 4 * peer[0] + 2 * peer[1] + peer[2]
            return pltpu.make_async_remote_copy(
                src_ref=g_ref.at[pid], dst_ref=out_ref.at[pid if arriving else my],
                send_sem=send_sems.at[7 * r + k], recv_sem=recv_sems.at[7 * r + k],
                device_id=peer, device_id_type=MESH_IDS)

        def mine():
            return pltpu.make_async_copy(g_ref.at[my], out_ref.at[my], local_sem.at[r])

        if phase == "start":
            mine().start()
            for k, peer in enumerate(peers):
                copy(k, peer, False).start()
        elif phase == "finish":
            for k, peer in enumerate(peers):
                copy(k, peer, True).wait_recv()
            for k, peer in enumerate(peers):
                copy(k, peer, False).wait_send()
            mine().wait()

    def _sibling_phase(self, phase, r, g_ref, out_ref, send_sems, recv_sems):
        x_, y_, c_ = lax.axis_index("x"), lax.axis_index("y"), lax.axis_index("c")

        def copy(chip):
            return pltpu.make_async_remote_copy(
                src_ref=g_ref.at[2 * chip + 1 - c_], dst_ref=out_ref.at[chip],
                send_sem=send_sems.at[7 * r + chip], recv_sem=recv_sems.at[7 * r + chip],
                device_id=(x_, y_, 1 - c_), device_id_type=MESH_IDS)

        if phase == "start":
            for chip in range(N_DEV // 2):
                copy(chip).start()
        elif phase == "finish":
            for chip in range(N_DEV // 2):
                copy(chip).wait_recv()
            for chip in range(N_DEV // 2):
                copy(chip).wait_send()

    def _chip_phase(self, phase, r, p_ref, out_ref, send_sems, recv_sems, local_sem):
        x_, y_, c_ = lax.axis_index("x"), lax.axis_index("y"), lax.axis_index("c")
        my = 2 * x_ + y_
        peers = [(1 - x_ if (k >> 1) & 1 else x_, 1 - y_ if k & 1 else y_) for k in range(1, N_DEV // 2)]

        def copy(k, peer, arriving):
            pid = 2 * peer[0] + peer[1]
            return pltpu.make_async_remote_copy(
                src_ref=p_ref.at[pid], dst_ref=out_ref.at[pid if arriving else my],
                send_sem=send_sems.at[7 * r + k], recv_sem=recv_sems.at[7 * r + k],
                device_id=(*peer, c_), device_id_type=MESH_IDS)

        def mine():
            return pltpu.make_async_copy(p_ref.at[my], out_ref.at[my], local_sem.at[r])

        if phase == "start":
            mine().start()
            for k, peer in enumerate(peers):
                copy(k, peer, False).start()
        elif phase == "finish":
            for k, peer in enumerate(peers):
                copy(k, peer, True).wait_recv()
            for k, peer in enumerate(peers):
                copy(k, peer, False).wait_send()
            mine().wait()

    def phase(self, phase, in_refs, out_refs, send_sems, recv_sems, local_sem):
        ng, ne, ns = len(self.gathers), len(self.exchanges), len(self.sibling_swaps)
        for r in range(self.n):
            if r < ng:
                self._gather_phase(phase, r, in_refs[r], out_refs[r], send_sems, recv_sems, local_sem)
            elif r < ng + ne:
                self._exchange_phase(phase, r, in_refs[r], out_refs[r], send_sems, recv_sems, local_sem)
            elif r < ng + ne + ns:
                self._sibling_phase(phase, r, in_refs[r], out_refs[r], send_sems, recv_sems)
            else:
                self._chip_phase(phase, r, in_refs[r], out_refs[r], send_sems, recv_sems, local_sem)


def _call(body, *, name, grid, in_specs, out_specs, out_shape, operands, scratch_shapes=(), semantics, riders=None):
    if riders is None or riders.n == 0:
        return tuple(pl.pallas_call(
            body, name=name, grid=grid, out_shape=tuple(out_shape), in_specs=list(in_specs),
            out_specs=tuple(out_specs), scratch_shapes=list(scratch_shapes), compiler_params=_cparams(semantics),
        )(*operands))

    n_in, n_out, n_scr, rn = len(in_specs), len(out_specs), len(scratch_shapes), riders.n
    steps = math.prod(grid)

    def full_body(*refs):
        ins, r_in = refs[:n_in], refs[n_in:n_in + rn]
        outs, r_out = refs[n_in + rn:n_in + rn + n_out], refs[n_in + rn + n_out:n_in + 2 * rn + n_out]
        scr = refs[n_in + 2 * rn + n_out:n_in + 2 * rn + n_out + n_scr]
        sems = refs[n_in + 2 * rn + n_out + n_scr:]
        step = 0
        for axis, size in enumerate(grid):
            step = step * size + pl.program_id(axis)

        @pl.when(step == 0)
        def _():
            riders.phase("start", r_in, r_out, *sems)

        @pl.when(step == steps - 1 - steps // 8)
        def _():
            riders.phase("pass", r_in, r_out, *sems)

        body(*ins, *outs, *scr)

        @pl.when(step == steps - 1)
        def _():
            riders.phase("finish", r_in, r_out, *sems)

    hbm = pl.BlockSpec(memory_space=pl.ANY)
    return tuple(pl.pallas_call(
        full_body, name=name, grid=grid,
        out_shape=tuple(out_shape) + tuple(riders.out_shape()),
        in_specs=list(in_specs) + [hbm] * rn,
        out_specs=tuple(out_specs) + (hbm,) * rn,
        scratch_shapes=list(scratch_shapes) + riders.scratch(),
        compiler_params=_cparams(("arbitrary",) * len(grid)),
    )(*operands, *riders.operands()))


def _mm_nn_call(a, w3, name, riders=None):
    m, k = a.shape
    nb, _, ns = w3.shape
    tm = _div(m, MM_ROWS, SUBLANE)
    tn = _div(ns, max(LANE, (6 << 20) // (2 * k)), LANE)
    tps = ns // tn

    def body(a_ref, w_ref, o_ref):
        o_ref[...] = _raw_dot(a_ref[...], w_ref[...], "nn")

    return _call(
        body, name=name, grid=(nb * tps, m // tm),
        out_shape=[jax.ShapeDtypeStruct((m, nb * ns), F32)],
        in_specs=[pl.BlockSpec((tm, k), lambda j, i: (i, 0)),
                  pl.BlockSpec((None, k, tn), lambda j, i: (j // tps, 0, j % tps))],
        out_specs=[pl.BlockSpec((tm, tn), lambda j, i: (i, j))],
        operands=(a, w3), semantics=("parallel", "parallel"), riders=riders)


def _mm_nt_call(g, w3, name, riders=None):
    m, n = g.shape
    nb, k, ns = w3.shape
    tm = _div(m, MM_ROWS, 16) if m % 16 == 0 else m
    tk = _div(k, 2048, LANE)
    tn = _div(ns, max(LANE, (6 << 20) // (2 * tk)), LANE)
    tps = ns // tn
    nj = nb * tps
    emit = tk == k

    def body(g_ref, w_ref, o_ref, *rest):
        acc = rest[-1]
        j = pl.program_id(2)
        gb = g_ref[...].astype(BF16)
        if emit:
            rest[0][...] = gb
        p = _raw_dot(gb, w_ref[...], "nt")

        @pl.when(j == 0)
        def _():
            acc[...] = p

        @pl.when(j > 0)
        def _():
            acc[...] += p

        @pl.when(j == nj - 1)
        def _():
            o_ref[...] = acc[...]

    da, *rest = _call(
        body, name=name, grid=(m // tm, k // tk, nj),
        out_shape=[jax.ShapeDtypeStruct((m, k), F32)] + ([jax.ShapeDtypeStruct((m, n), BF16)] if emit else []),
        in_specs=[pl.BlockSpec((tm, tn), lambda i, kk, j: (i, j)),
                  pl.BlockSpec((None, tk, tn), lambda i, kk, j: (j // tps, kk, j % tps))],
        out_specs=[pl.BlockSpec((tm, tk), lambda i, kk, j: (i, kk))]
        + ([pl.BlockSpec((tm, tn), lambda i, kk, j: (i, j))] if emit else []),
        scratch_shapes=[pltpu.VMEM((tm, tk), F32)],
        operands=(g, w3), semantics=("parallel", "arbitrary", "arbitrary"), riders=riders)
    return (da, *rest) if emit else (da, g.astype(BF16), *rest)


def _mm_dw_call(a, g, nb, out_dtype, name, riders=None):
    m, k = a.shape
    n = g.shape[1]
    ns = n // nb
    tn = _div(ns, max(LANE, (13 << 20) // (2 * m)), LANE)
    tk = _div(k, max(LANE, (9 << 20) // (4 * m)), LANE)
    tps = ns // tn

    def body(a_ref, g_ref, o_ref):
        o_ref[...] = _raw_dot(a_ref[...], g_ref[...], "tn").astype(o_ref.dtype)

    return _call(
        body, name=name, grid=(nb * tps, k // tk),
        out_shape=[jax.ShapeDtypeStruct((nb, k, ns), out_dtype)],
        in_specs=[pl.BlockSpec((m, tk), lambda j, kk: (0, kk)),
                  pl.BlockSpec((m, tn), lambda j, kk: (0, j))],
        out_specs=[pl.BlockSpec((None, tk, tn), lambda j, kk: (j // tps, kk, j % tps))],
        operands=(a, g), semantics=("parallel", "parallel"), riders=riders)


def _sum_exchanged(exchanged, tag):
    return [_sum_slots_call(e, "sum_%s_%d" % (tag, k)) for k, e in enumerate(exchanged)]


def _make_mm(tag, n_ride=0):
    def impl(a, w3, *shards):
        return _mm_nn_call(a, w3, "mm_" + tag, _Riders(gathers=[s.astype(BF16) for s in shards]))

    @jax.custom_vjp
    def mm(a, w3, *shards):
        return impl(a, w3, *shards)

    def fwd(a, w3, *shards):
        return impl(a, w3, *shards), (a, w3)

    def bwd(res, cts):
        a, w3 = res
        d_gathered = list(cts[1:])
        da, gb, *halves = _mm_nt_call(cts[0], w3, "mm_da_" + tag, _Riders(sibling_swaps=d_gathered))
        pairs = [_pair_sum_call(d, h, "pair_%s_%d" % (tag, r)) for r, (d, h) in enumerate(zip(d_gathered, halves))]
        dw, *exchanged = _mm_dw_call(a, gb, w3.shape[0], w3.dtype, "mm_dw_" + tag,
                                     _Riders(chip_exchanges=pairs))
        return (da, dw, *_sum_exchanged(exchanged, tag))

    mm.defvjp(fwd, bwd)
    return mm


def _make_rowwise(f, kinds, diff, out_cols, tag, nx_rows, tile=ROW_TILE):
    n_in = len(kinds)
    nxt = nx_rows // tile

    def in_spec(kind, arr):
        if kind == "row":
            return pl.BlockSpec((tile, arr.shape[1]), lambda i: (i, 0))
        if kind == "pair":
            return pl.BlockSpec((arr.shape[0], tile, arr.shape[2]), lambda i: (0, i, 0))
        if kind == "grp":
            return pl.BlockSpec((None, 1, arr.shape[2]), lambda i: (i // nxt, 0, 0))
        return pl.BlockSpec(arr.shape, lambda i: (0,) * arr.ndim)

    def fwd_call(*args):
        t = next(a.shape[0] for a, kd in zip(args, kinds) if kd == "row")

        def body(*refs):
            outs = f(*[r[...] for r in refs[:n_in]])
            for r, o in zip(refs[n_in:], outs):
                r[...] = o

        return pl.pallas_call(
            body, name="rw_" + tag, grid=(t // tile,),
            out_shape=tuple(jax.ShapeDtypeStruct((t, c), F32) for c in out_cols),
            in_specs=[in_spec(kd, a) for kd, a in zip(kinds, args)],
            out_specs=tuple(pl.BlockSpec((tile, c), lambda i: (i, 0)) for c in out_cols),
            compiler_params=_cparams(("parallel",)),
        )(*args)

    def bwd_call(args, cts):
        t = cts[0].shape[0]
        didx = [i for i in range(n_in) if diff[i]]

        def body(*refs):
            i = pl.program_id(0)
            vals = [r[...] for r in refs[:n_in]]
            ct = tuple(r[...] for r in refs[n_in:n_in + len(out_cols)])
            outs = refs[n_in + len(out_cols):]

            def g(*dv):
                full = list(vals)
                for j, v in zip(didx, dv):
                    full[j] = v
                return tuple(f(*full))

            _, vjp = jax.vjp(g, *[vals[j] for j in didx])
            grads = vjp(ct)
            for j, o_ref, d in zip(didx, outs, grads):
                if kinds[j] in ("row", "pair"):
                    o_ref[...] = d
                else:
                    first = (i % nxt == 0) if kinds[j] == "grp" else (i == 0)

                    @pl.when(first)
                    def _(o_ref=o_ref, d=d):
                        o_ref[...] = d

                    @pl.when(jnp.logical_not(first))
                    def _(o_ref=o_ref, d=d):
                        o_ref[...] += d

        def out_spec(j):
            return in_spec(kinds[j], args[j])

        return pl.pallas_call(
            body, name="rw_bwd_" + tag, grid=(t // tile,),
            out_shape=tuple(jax.ShapeDtypeStruct(args[j].shape, F32) for j in didx),
            in_specs=[in_spec(kd, a) for kd, a in zip(kinds, args)]
            + [pl.BlockSpec((tile, c), lambda i: (i, 0)) for c in out_cols],
            out_specs=tuple(out_spec(j) for j in didx),
            compiler_params=_cparams(("arbitrary",)),
        )(*args, *cts)

    @jax.custom_vjp
    def op(*args):
        return fwd_call(*args)

    def fwd(*args):
        return fwd_call(*args), args

    def bwd(args, cts):
        grads = bwd_call(args, cts)
        full = [None] * n_in
        for j, gr in zip([i for i in range(n_in) if diff[i]], grads):
            full[j] = gr
        return tuple(jnp.zeros_like(a) if gfull is None else gfull for a, gfull in zip(args, full))

    op.defvjp(fwd, bwd)
    return op


def _layer_norm(z, g, b):
    mu = jnp.mean(z, axis=-1, keepdims=True)
    var = jnp.mean(jnp.square(z - mu), axis=-1, keepdims=True)
    return (z - mu) * lax.rsqrt(var + LN_EPS) * g + b


def _f_modulate(h, sc, sh):
    return (h * (1.0 + sc) + sh,)


def _f_ln_res(h, y, gate, g, b):
    return (_layer_norm(DEEPNORM_ALPHA * h + gate * y, g, b),)


def _f_ln_res_mod(h, y, gate, g, b, sc, sh):
    h1 = _layer_norm(DEEPNORM_ALPHA * h + gate * y, g, b)
    return h1, h1 * (1.0 + sc) + sh


def _f_swiglu(ab):
    half = ab.shape[1] // 2
    return (_silu(ab[:, :half]) * ab[:, half:],)


def _f_ret_combine(gates, o):
    o_f, o_b = o[0], o[1]
    hv = o_f.shape[1]
    dv = hv // RET_HEADS

    def gn(o):
        parts = []
        for h in range(RET_HEADS):
            oh = o[:, h * dv:(h + 1) * dv]
            mu = jnp.mean(oh, axis=-1, keepdims=True)
            var = jnp.mean(jnp.square(oh - mu), axis=-1, keepdims=True)
            parts.append((oh - mu) * lax.rsqrt(var + GN_EPS))
        return jnp.concatenate(parts, axis=1)

    return (_silu(gates[:, :hv]) * gn(o_f) + _silu(gates[:, hv:]) * gn(o_b),)


def _rms(x, g):
    return x * lax.rsqrt(jnp.mean(jnp.square(x), axis=-1, keepdims=True) + RMS_EPS) * g


def _f_rms(x, g):
    return (_rms(x, g),)


def _rot(x, cos, sin_signed, perm):
    swapped = jnp.dot(x, perm, precision=lax.Precision.HIGHEST, preferred_element_type=F32)
    return x * cos + swapped * sin_signed


def _f_kv_latent(ckv, g, cos, sin_signed, perm):
    lora = ckv.shape[1] - MLA_PAD
    return _rms(ckv[:, :lora], g), _rot(ckv[:, lora:], cos, sin_signed, perm)


def _f_loss(y, tgt):
    row = 0.5 * jnp.mean(jnp.square(y - tgt), axis=-1, keepdims=True)
    return (jnp.broadcast_to(row, (y.shape[0], LANE)),)


def _ret_step(q, k, v, state, lg, cos, sin, d):
    c, dk = q.shape
    half = dk // 2

    def rope(t):
        t1, t2 = t[:, :half], t[:, half:]
        return jnp.concatenate([t1 * cos - t2 * sin, t1 * sin + t2 * cos], axis=1)

    q = rope(q)
    k = rope(k * (dk ** -0.5))
    sgn = (1 - 2 * d).astype(F32)
    ii = lax.broadcasted_iota(jnp.int32, (c, c), 0).astype(F32)
    jj = lax.broadcasted_iota(jnp.int32, (c, c), 1).astype(F32)
    e = (ii - jj) * sgn
    intra = jnp.where(e >= 0, jnp.exp(lg * jnp.maximum(e, 0.0)), 0.0)
    idx = lax.broadcasted_iota(jnp.int32, (c, 1), 0).astype(F32)
    pos = jnp.where(d == 0, idx, c - 1.0 - idx)
    q_dec = jnp.exp(lg * (pos + 1.0))
    k_dec = jnp.exp(lg * (c - 1.0 - pos))
    c_dec = jnp.exp(lg * float(c))
    scores = _dot(q, k, "nt") * intra
    o = _dot(scores, v, "nn") + _dot(q * q_dec, state, "nn")
    new_state = state * c_dec + _dot(k * k_dec, v, "tn")
    return o, new_state


def _ret_chunk_of(d, p, nxc, nc):
    return (1 - d) * ((p + nxc) % nc) + d * (nc - 1 - p)


RET_GROUP = 4
RET_GROUPS = RET_HEADS // RET_GROUP


def _ret_specs(t, dk, dv, nxc, nc, step_of):
    c, g = RET_CHUNK, RET_GROUP

    def chunk(d, h, s):
        return _ret_chunk_of(d, step_of(s), nxc, nc)

    q_spec = pl.BlockSpec((c, g * dk), lambda d, h, s: (chunk(d, h, s), h))
    k_spec = pl.BlockSpec((c, g * dk), lambda d, h, s: (chunk(d, h, s), RET_GROUPS + h))
    v_spec = pl.BlockSpec((c, g * dv), lambda d, h, s: (chunk(d, h, s), (2 * RET_HEADS * dk) // (g * dv) + h))
    tab_spec = pl.BlockSpec((c, dk // 2), lambda d, h, s: (chunk(d, h, s), 0))
    lg_spec = pl.BlockSpec((g, SUBLANE, LANE), lambda d, h, s: (d * RET_GROUPS + h, 0, 0))
    o_spec = pl.BlockSpec((None, c, g * dv), lambda d, h, s: (d, chunk(d, h, s), h))
    st_spec = pl.BlockSpec((None, g, None, dk, dv), lambda d, h, s: (d, h, step_of(s), 0, 0))
    return q_spec, k_spec, v_spec, tab_spec, lg_spec, o_spec, st_spec


def _make_retention(tag, nx_rows):
    def dims(qkv):
        t, w = qkv.shape
        dk = w // (4 * RET_HEADS)
        return t, dk, 2 * dk, nx_rows // RET_CHUNK, t // RET_CHUNK

    def fwd_call(qkv, cos, sin, lgb, riders):
        t, dk, dv, nxc, nc = dims(qkv)
        q_spec, k_spec, v_spec, tab_spec, lg_spec, o_spec, st_spec = _ret_specs(t, dk, dv, nxc, nc, lambda s: s)

        def body(q_ref, k_ref, v_ref, cos_ref, sin_ref, lg_ref, o_ref, st_ref, state):
            d = pl.program_id(0)

            @pl.when(pl.program_id(2) == 0)
            def _():
                state[...] = jnp.zeros_like(state)

            st_ref[...] = state[...]
            cos_t, sin_t = cos_ref[...], sin_ref[...]
            for g in range(RET_GROUP):
                o, new_state = _ret_step(q_ref[:, g * dk:(g + 1) * dk], k_ref[:, g * dk:(g + 1) * dk],
                                         v_ref[:, g * dv:(g + 1) * dv], state[g], lg_ref[g, 0:1, 0:1],
                                         cos_t, sin_t, d)
                o_ref[:, g * dv:(g + 1) * dv] = o
                state[g] = new_state

        return _call(
            body, name="ret_" + tag, grid=(2, RET_GROUPS, nc),
            out_shape=(jax.ShapeDtypeStruct((2, t, RET_HEADS * dv), F32),
                       jax.ShapeDtypeStruct((2, RET_HEADS, nc, dk, dv), F32)),
            in_specs=[q_spec, k_spec, v_spec, tab_spec, tab_spec, lg_spec],
            out_specs=(o_spec, st_spec),
            scratch_shapes=[pltpu.VMEM((RET_GROUP, dk, dv), F32)],
            operands=(qkv, qkv, qkv, cos, sin, lgb),
            semantics=("parallel", "parallel", "arbitrary"), riders=riders)

    def bwd_call(qkv, cos, sin, lgb, states, do, riders):
        t, dk, dv, nxc, nc = dims(qkv)
        q_spec, k_spec, v_spec, tab_spec, lg_spec, o_spec, st_spec = _ret_specs(
            t, dk, dv, nxc, nc, lambda s: nc - 1 - s)
        c = RET_CHUNK

        def chunk(d, h, s):
            return _ret_chunk_of(d, nc - 1 - s, nxc, nc)

        dq_spec = pl.BlockSpec((None, c, RET_GROUP * dk), lambda d, h, s: (d, chunk(d, h, s), h))
        dv_spec = pl.BlockSpec((None, c, RET_GROUP * dv), lambda d, h, s: (d, chunk(d, h, s), h))

        def body(q_ref, k_ref, v_ref, cos_ref, sin_ref, lg_ref, st_ref, do_ref,
                 dq_ref, dk_ref, dv_ref, dlg_ref, dstate):
            d = pl.program_id(0)
            s = pl.program_id(2)

            @pl.when(s == 0)
            def _():
                dstate[...] = jnp.zeros_like(dstate)

            cos_t, sin_t = cos_ref[...], sin_ref[...]

            def step(q, k, v, state, lg):
                return _ret_step(q, k, v, state, lg, cos_t, sin_t, d)

            corner = jnp.logical_and(lax.broadcasted_iota(jnp.int32, (SUBLANE, LANE), 0) == 0,
                                     lax.broadcasted_iota(jnp.int32, (SUBLANE, LANE), 1) == 0)
            for g in range(RET_GROUP):
                qs, vs = slice(g * dk, (g + 1) * dk), slice(g * dv, (g + 1) * dv)
                _, vjp = jax.vjp(step, q_ref[:, qs], k_ref[:, qs], v_ref[:, vs], st_ref[g], lg_ref[g, 0:1, 0:1])
                dq, dkk, dvv, dst, dlg = vjp((do_ref[:, vs], dstate[g]))
                dq_ref[:, qs] = dq
                dk_ref[:, qs] = dkk
                dv_ref[:, vs] = dvv
                dstate[g] = dst
                dlg_full = jnp.where(corner, dlg, 0.0)

                @pl.when(s == 0)
                def _(g=g, dlg_full=dlg_full):
                    dlg_ref[g] = dlg_full

                @pl.when(s > 0)
                def _(g=g, dlg_full=dlg_full):
                    dlg_ref[g] += dlg_full

        return _call(
            body, name="ret_bwd_" + tag, grid=(2, RET_GROUPS, nc),
            out_shape=(jax.ShapeDtypeStruct((2, t, RET_HEADS * dk), F32),
                       jax.ShapeDtypeStruct((2, t, RET_HEADS * dk), F32),
                       jax.ShapeDtypeStruct((2, t, RET_HEADS * dv), F32),
                       jax.ShapeDtypeStruct(lgb.shape, F32)),
            in_specs=[q_spec, k_spec, v_spec, tab_spec, tab_spec, lg_spec, st_spec, o_spec],
            out_specs=(dq_spec, dq_spec, dv_spec, lg_spec),
            scratch_shapes=[pltpu.VMEM((RET_GROUP, dk, dv), F32)],
            operands=(qkv, qkv, qkv, cos, sin, lgb, states, do),
            semantics=("parallel", "parallel", "arbitrary"), riders=riders)

    def impl(qkv, cos, sin, lgb, *shards):
        return fwd_call(qkv, cos, sin, lgb, _Riders(gathers=[s.astype(BF16) for s in shards]))

    @jax.custom_vjp
    def ret(qkv, cos, sin, lgb, *shards):
        o, _, *gathered = impl(qkv, cos, sin, lgb, *shards)
        return (o, *gathered)

    def fwd(qkv, cos, sin, lgb, *shards):
        o, states, *gathered = impl(qkv, cos, sin, lgb, *shards)
        return (o, *gathered), (qkv, cos, sin, lgb, states)

    def bwd(res, cts):
        qkv, cos, sin, lgb, states = res
        dq, dkk, dvv, dlg, *exchanged = bwd_call(qkv, cos, sin, lgb, states, cts[0],
                                                 _Riders(exchanges=list(cts[1:])))
        dqkv = jnp.concatenate([dq[0] + dq[1], dkk[0] + dkk[1], dvv[0] + dvv[1]], axis=1)
        return (dqkv, jnp.zeros_like(cos), jnp.zeros_like(sin), dlg, *_sum_exchanged(exchanged, "ret_" + tag))

    ret.defvjp(fwd, bwd)
    return ret


MLA_SCALE = (MLA_D_NOPE + MLA_D_ROPE) ** -0.5
MLA_Q_ROWS = 256
MLA_KEY_BLOCKS = 2


def _attn_query(q, cos, sin_signed, perm):
    qr = _rot(q[:, MLA_D_NOPE:], cos, sin_signed, perm)
    return jnp.concatenate([q[:, :MLA_D_NOPE], qr], axis=1).astype(BF16)


def _make_attention(tag, nx_rows):
    hw = MLA_D_NOPE + MLA_PAD
    tq = MLA_Q_ROWS

    def operands(kv, kr):
        t = kv.shape[0]
        kv3 = kv.reshape(t, MLA_HEADS, MLA_D_NOPE + MLA_D_V)
        k = jnp.concatenate([kv3[..., :MLA_D_NOPE], jnp.broadcast_to(kr[:, None, :], (t, MLA_HEADS, MLA_PAD))],
                            axis=2).reshape(t, MLA_HEADS * hw).astype(BF16)
        v = kv3[..., MLA_D_NOPE:].reshape(t, MLA_HEADS * MLA_D_V).astype(BF16)
        return k, k.T, v

    def softmax_t(s_t):
        m = jnp.max(s_t, axis=0, keepdims=True)
        e = jnp.exp((s_t - m) * MLA_SCALE)
        return e, m, 1.0 / jnp.sum(e, axis=0, keepdims=True)

    def fwd_call(q, k_t, v, cos, sin_signed, perm, riders):
        t = q.shape[0]
        nctx = t - nx_rows

        def body(q_ref, cos_ref, sin_ref, perm_ref, kt_ref, v_ref, o_ref, m_ref, linv_ref):
            i = pl.program_id(1)
            qb = _attn_query(q_ref[...], cos_ref[...], sin_ref[...], perm_ref[...])

            def attend(kt_part, v_part):
                s = _raw_dot(qb, kt_part, "nn")
                m = jnp.max(s, axis=1, keepdims=True)
                e = jnp.exp((s - m) * MLA_SCALE)
                linv = 1.0 / jnp.sum(e, axis=1, keepdims=True)
                o_ref[...] = _raw_dot(e * linv, v_part, "nn")
                m_ref[...] = m
                linv_ref[...] = linv

            @pl.when(i * tq < nx_rows)
            def _():
                attend(kt_ref[...], v_ref[...])

            @pl.when(i * tq >= nx_rows)
            def _():
                attend(kt_ref[:, pl.ds(nx_rows, nctx)], v_ref[pl.ds(nx_rows, nctx), :])

        stat = jax.ShapeDtypeStruct((MLA_HEADS, t, 1), F32)
        stat_spec = pl.BlockSpec((None, tq, 1), lambda h, i: (h, i, 0))
        o, m, linv, *gathered = _call(
            body, name="mla_" + tag, grid=(MLA_HEADS, t // tq),
            out_shape=(jax.ShapeDtypeStruct((t, MLA_HEADS * MLA_D_V), F32), stat, stat),
            in_specs=[pl.BlockSpec((tq, hw), lambda h, i: (i, h)),
                      pl.BlockSpec((tq, MLA_PAD), lambda h, i: (i, 0)),
                      pl.BlockSpec((tq, MLA_PAD), lambda h, i: (i, 0)),
                      pl.BlockSpec((MLA_PAD, MLA_PAD), lambda h, i: (0, 0)),
                      pl.BlockSpec((hw, t), lambda h, i: (h, 0)),
                      pl.BlockSpec((t, MLA_D_V), lambda h, i: (0, h))],
            out_specs=(pl.BlockSpec((tq, MLA_D_V), lambda h, i: (i, h)), stat_spec, stat_spec),
            operands=(q, cos, sin_signed, perm, k_t, v), semantics=("parallel", "arbitrary"), riders=riders)
        return (o, m.reshape(MLA_HEADS, 1, t), linv.reshape(MLA_HEADS, 1, t), *gathered)

    def bwd_call(q, k, k_t, v, cos, sin_signed, perm, o, m, linv, do, riders):
        t = q.shape[0]
        nctx = t - nx_rows

        def body(q_ref, cos_ref, sin_ref, perm_ref, k_ref, kt_ref, v_ref, o_ref, m_ref, linv_ref, do_ref,
                 dq_ref, dkn_ref, dv_ref, dkr_ref):
            i = pl.program_id(1)
            cos_t, sin_t, perm_t = cos_ref[...], sin_ref[...], perm_ref[...]
            qb = _attn_query(q_ref[...], cos_t, sin_t, perm_t)
            do_t = do_ref[...]
            dob = do_t.astype(BF16)
            delta = jnp.sum((do_t * o_ref[...]).T, axis=0, keepdims=True)
            m_t, linv_t = m_ref[...], linv_ref[...]

            def key_rows(rows):
                e = jnp.exp((_raw_dot(k_ref[rows, :], qb, "nt") - m_t) * MLA_SCALE)
                p_t = e * linv_t
                dp_t = _raw_dot(v_ref[rows, :], dob, "nt")
                ds_t = (p_t * (dp_t - delta) * MLA_SCALE).astype(BF16)
                dk = _raw_dot(ds_t, qb, "nn")
                dkn_ref[rows, :] += dk[:, :MLA_D_NOPE]
                dkr_ref[rows, :] += dk[:, MLA_D_NOPE:]
                dv_ref[rows, :] += _raw_dot(p_t, dob, "nn")
                return _raw_dot(kt_ref[:, rows], ds_t, "nn")

            def finish(dq_t):
                dqb = dq_t.T
                g_r = dqb[:, MLA_D_NOPE:]
                dq_r = g_r * cos_t + jnp.dot(g_r * sin_t, perm_t, precision=lax.Precision.HIGHEST,
                                             preferred_element_type=F32)
                dq_ref[...] = jnp.concatenate([dqb[:, :MLA_D_NOPE], dq_r], axis=1)

            @pl.when(i == 0)
            def _():
                dkn_ref[...] = jnp.zeros_like(dkn_ref)
                dv_ref[...] = jnp.zeros_like(dv_ref)

            @pl.when(jnp.logical_and(i == 0, pl.program_id(0) == 0))
            def _():
                dkr_ref[...] = jnp.zeros_like(dkr_ref)

            @pl.when(i * tq < nx_rows)
            def _():
                size = t // MLA_KEY_BLOCKS
                dq_t = key_rows(pl.ds(0, size))
                for b in range(1, MLA_KEY_BLOCKS):
                    dq_t = dq_t + key_rows(pl.ds(b * size, size))
                finish(dq_t)

            @pl.when(i * tq >= nx_rows)
            def _():
                finish(key_rows(pl.ds(nx_rows, nctx)))

        stat_spec = pl.BlockSpec((None, 1, tq), lambda h, i: (h, 0, i))
        return _call(
            body, name="mla_bwd_" + tag, grid=(MLA_HEADS, t // tq),
            out_shape=(jax.ShapeDtypeStruct(q.shape, F32),
                       jax.ShapeDtypeStruct((t, MLA_HEADS * MLA_D_NOPE), F32),
                       jax.ShapeDtypeStruct((t, MLA_HEADS * MLA_D_V), F32),
                       jax.ShapeDtypeStruct((t, MLA_PAD), F32)),
            in_specs=[pl.BlockSpec((tq, hw), lambda h, i: (i, h)),
                      pl.BlockSpec((tq, MLA_PAD), lambda h, i: (i, 0)),
                      pl.BlockSpec((tq, MLA_PAD), lambda h, i: (i, 0)),
                      pl.BlockSpec((MLA_PAD, MLA_PAD), lambda h, i: (0, 0)),
                      pl.BlockSpec((t, hw), lambda h, i: (0, h)),
                      pl.BlockSpec((hw, t), lambda h, i: (h, 0)),
                      pl.BlockSpec((t, MLA_D_V), lambda h, i: (0, h)),
                      pl.BlockSpec((tq, MLA_D_V), lambda h, i: (i, h)),
                      stat_spec, stat_spec,
                      pl.BlockSpec((tq, MLA_D_V), lambda h, i: (i, h))],
            out_specs=(pl.BlockSpec((tq, hw), lambda h, i: (i, h)),
                       pl.BlockSpec((t, MLA_D_NOPE), lambda h, i: (0, h)),
                       pl.BlockSpec((t, MLA_D_V), lambda h, i: (0, h)),
                       pl.BlockSpec((t, MLA_PAD), lambda h, i: (0, 0))),
            operands=(q, cos, sin_signed, perm, k, k_t, v, o, m, linv, do),
            semantics=("arbitrary", "arbitrary"), riders=riders)

    def impl(q, kv, kr, cos, sin_signed, perm, *shards):
        k, k_t, v = operands(kv, kr)
        o, m, linv, *gathered = fwd_call(q, k_t, v, cos, sin_signed, perm,
                                         _Riders(gathers=[s.astype(BF16) for s in shards]))
        return (o, *gathered), (q, k, k_t, v, cos, sin_signed, perm, o, m, linv)

    @jax.custom_vjp
    def attn(q, kv, kr, cos, sin_signed, perm, *shards):
        return impl(q, kv, kr, cos, sin_signed, perm, *shards)[0]

    def fwd(q, kv, kr, cos, sin_signed, perm, *shards):
        return impl(q, kv, kr, cos, sin_signed, perm, *shards)

    def bwd(res, cts):
        q, k, k_t, v, cos, sin_signed, perm, o, m, linv = res
        dq, dkn, dv, dkr, *exchanged = bwd_call(q, k, k_t, v, cos, sin_signed, perm, o, m, linv, cts[0],
                                                _Riders(exchanges=list(cts[1:])))
        t = q.shape[0]
        dkv = jnp.concatenate([dkn.reshape(t, MLA_HEADS, MLA_D_NOPE), dv.reshape(t, MLA_HEADS, MLA_D_V)],
                              axis=2).reshape(t, -1)
        return (dq, dkv, dkr, jnp.zeros_like(cos), jnp.zeros_like(sin_signed), jnp.zeros_like(perm),
                *_sum_exchanged(exchanged, "mla_" + tag))

    attn.defvjp(fwd, bwd)
    return attn


def _adamw_call(w, g, m, v, name):
    rows, cols = w.shape
    tr = _div(rows, max(SUBLANE, (1 << 18) // cols), SUBLANE) if rows % SUBLANE == 0 else rows

    def body(w_ref, g_ref, m_ref, v_ref, d_ref, nm_ref, nv_ref):
        gg = g_ref[...]
        nm = ADAM_B1 * m_ref[...] + (1.0 - ADAM_B1) * gg
        nv = ADAM_B2 * v_ref[...] + (1.0 - ADAM_B2) * jnp.square(gg)
        m_hat = nm / (1.0 - ADAM_B1 ** ADAM_STEP)
        v_hat = nv / (1.0 - ADAM_B2 ** ADAM_STEP)
        d_ref[...] = -ADAM_LR * (m_hat / (jnp.sqrt(v_hat) + ADAM_EPS) + ADAM_WD * w_ref[...])
        nm_ref[...] = nm
        nv_ref[...] = nv

    spec = pl.BlockSpec((tr, cols), lambda i: (i, 0))
    return pl.pallas_call(
        body, name=name, grid=(rows // tr,),
        out_shape=tuple(jax.ShapeDtypeStruct((rows, cols), F32) for _ in range(3)),
        in_specs=[spec] * 4, out_specs=(spec,) * 3,
        compiler_params=_cparams(("parallel",)),
    )(w, g, m, v)


def _adamw(w, g, m, v, name):
    shape = w.shape
    if w.ndim >= 2 and shape[-1] % LANE == 0:
        two = (-1, shape[-1])
    elif w.size % LANE == 0:
        two = (-1, LANE)
    else:
        two = (1, w.size)
    outs = _adamw_call(w.reshape(two), g.reshape(two), m.reshape(two), v.reshape(two), name)
    return tuple(o.reshape(shape) for o in outs)


def _ret_tables(nx, nctx, dk):
    inv = RET_ROPE_BASE ** (-jnp.linspace(0.0, 1.0, dk // 2, dtype=F32))
    ang = jnp.arange(nx, dtype=F32)[:, None] * inv[None, :]
    cos = jnp.concatenate([jnp.cos(ang), jnp.ones((nctx, dk // 2), F32)], axis=0)
    sin = jnp.concatenate([jnp.sin(ang), jnp.zeros((nctx, dk // 2), F32)], axis=0)
    return cos, sin


def _mla_tables(nx, nctx):
    quarter = MLA_D_ROPE // 4
    inv = AXIAL_ROPE_BASE ** (-jnp.arange(quarter, dtype=F32) * 2.0 / (MLA_D_ROPE // 2))
    tok = jnp.arange(nx)
    r_ang = (tok // GRID_W).astype(F32)[:, None] * inv[None, :]
    c_ang = (tok % GRID_W).astype(F32)[:, None] * inv[None, :]
    pad = MLA_PAD - MLA_D_ROPE
    cos = jnp.concatenate([jnp.cos(r_ang), jnp.cos(r_ang), jnp.cos(c_ang), jnp.cos(c_ang),
                           jnp.ones((nx, pad), F32)], axis=1)
    sin = jnp.concatenate([-jnp.sin(r_ang), jnp.sin(r_ang), -jnp.sin(c_ang), jnp.sin(c_ang),
                           jnp.zeros((nx, pad), F32)], axis=1)
    cos = jnp.concatenate([cos, jnp.ones((nctx, MLA_PAD), F32)], axis=0)
    sin = jnp.concatenate([sin, jnp.zeros((nctx, MLA_PAD), F32)], axis=0)
    lane = jnp.arange(MLA_PAD)
    partner = jnp.where(lane % (2 * quarter) < quarter, lane + quarter, lane - quarter)
    perm = ((lane[:, None] == partner[None, :]) & (lane[None, :] < MLA_D_ROPE)).astype(F32)
    return cos, sin, perm


def _columns(gathered):
    return gathered


def _rows(gathered):
    return gathered.reshape(1, gathered.shape[0] * gathered.shape[1], gathered.shape[2])


def _unshard_vec(gathered):
    return jnp.moveaxis(gathered, 0, 1).reshape(gathered.shape[1], -1)


def _pad_rows(a, rows):
    return jnp.concatenate([a, jnp.zeros((rows - a.shape[0],) + a.shape[1:], a.dtype)], axis=0)


def _loss_fn(weights, x, c_all, ctx, tgt, me):
    nx, d_model = x.shape
    nctx = ctx.shape[0]
    dk = d_model // RET_HEADS
    ret_cos, ret_sin = _ret_tables(nx, nctx, dk)
    mla_cos, mla_sin, mla_perm = _mla_tables(nx, nctx)

    rw = functools.partial(_make_rowwise, nx_rows=nx)

    n_ln = weights["ln_g"].size
    small = jnp.concatenate([weights["ln_g"].reshape(-1), weights["ln_b"].reshape(-1),
                             weights["mla_g_q"].reshape(-1), weights["mla_g_kv"].reshape(-1)])
    n_small = small.size
    small = _pad_rows(small.reshape(-1, 1), -(-n_small // LANE) * LANE).reshape(-1, LANE)
    small = _make_gather("small", F32)(small).reshape(N_DEV, -1)
    ln_g = _unshard_vec(small[:, :n_ln].reshape(N_DEV, DEPTH * 2, -1)).reshape(DEPTH, 2, d_model)
    ln_b = _unshard_vec(small[:, n_ln:2 * n_ln].reshape(N_DEV, DEPTH * 2, -1)).reshape(DEPTH, 2, d_model)
    n_g = weights["mla_g_q"].size
    g_q = _unshard_vec(small[:, 2 * n_ln:2 * n_ln + n_g].reshape(N_DEV, DEPTH // 2, -1))
    g_kv = _unshard_vec(small[:, 2 * n_ln + n_g:2 * n_ln + 2 * n_g].reshape(N_DEV, DEPTH // 2, -1))

    n_b = weights["ada_b"].size
    n_dec = weights["ret_decay_logit"].size
    repl = jnp.concatenate([weights["c_ctx"].reshape(-1), weights["ada_b"].reshape(-1),
                            weights["ret_decay_logit"].reshape(-1)])
    n_repl = repl.size
    repl = _pad_rows(repl.reshape(-1, 1), -(-n_repl // LANE) * LANE).reshape(-1, LANE)
    repl = _make_replicated("repl")(repl).reshape(-1)
    c_ctx = repl[:d_model]
    ada_b = repl[d_model:d_model + n_b].reshape(DEPTH, 6 * d_model)
    decay = repl[d_model + n_b:d_model + n_b + n_dec].reshape(weights["ret_decay_logit"].shape)

    cond_rows = 2 * SUBLANE
    cond = _pad_rows(jnp.concatenate([c_all, c_ctx[None, :]], axis=0), cond_rows)
    s_cond = _make_rowwise_plain_silu(cond)
    mods = [_make_mm("ada%d" % i)(s_cond, weights["ada_w"][i][None])[0] for i in range(DEPTH)]
    mod = jnp.concatenate(mods, axis=0)
    mod = _make_gather("mod", F32)(mod)
    mod = jnp.moveaxis(mod.reshape(N_DEV, DEPTH, cond_rows, -1), 0, 2).reshape(DEPTH, cond_rows, -1)
    mod = mod + ada_b[:, None, :]
    mine = lax.dynamic_slice_in_dim(mod, me, 1, axis=1)
    mod = jnp.concatenate([mine, mod[:, N_DEV:N_DEV + 1]], axis=1)
    mod = mod.reshape(DEPTH, 2, 6, 1, d_model)

    def grp(i, j):
        return mod[i, :, j]

    h = jnp.concatenate([x, ctx], axis=0)
    ahead = {"ret_qkv": _make_gather("ret_qkv", BF16)(weights["ret_w_qkv"][0])}
    for i in range(DEPTH):
        j = i // 2
        retention_layer, last = i % 2 == 0, i + 1 == DEPTH
        (u,) = rw(_f_modulate, ("row", "grp", "grp"), (True, True, True), (d_model,), "mod%d" % i)(
            h, grp(i, 1), grp(i, 0))
        if retention_layer:
            lg = jax.nn.log_sigmoid(decay[j]).reshape(2 * RET_HEADS, 1, 1)
            lgb = jnp.broadcast_to(lg, (2 * RET_HEADS, SUBLANE, LANE))
            if "ret_g" in ahead:
                qkv, w_in = _make_mm("ret_qkv%d" % j, 1)(u, _columns(ahead.pop("ret_qkv")), weights["ffn_w_in"][i])
                gates, w_out = _make_mm("ret_g%d" % j, 1)(u, _columns(ahead.pop("ret_g")), weights["ffn_w_out"][i])
                w_o = ahead.pop("ret_o")
                (o,) = _make_retention("l%d" % j, nx)(qkv, ret_cos, ret_sin, lgb)
            else:
                qkv, w_g = _make_mm("ret_qkv%d" % j, 1)(u, _columns(ahead.pop("ret_qkv")), weights["ret_w_g"][j])
                gates, w_o, w_in = _make_mm("ret_g%d" % j, 2)(u, _columns(w_g), weights["ret_w_o"][j],
                                                              weights["ffn_w_in"][i])
                o, w_out = _make_retention("l%d" % j, nx)(qkv, ret_cos, ret_sin, lgb, weights["ffn_w_out"][i])
            (comb,) = rw(_f_ret_combine, ("row", "pair"), (True, True), (o.shape[2],),
                         "ret_comb%d" % j, tile=WIDE_ROW_TILE)(gates, o)
            nxt = [] if last else [weights[n][j] for n in MLA_MATRICES]
            y, *gathered = _make_mm("ret_o%d" % j, len(nxt))(comb, _rows(w_o), *nxt)
            if nxt:
                ahead["mla"] = gathered
        else:
            nxt = [] if last else [weights[n][j + 1] for n in ("ret_w_qkv", "ret_w_g", "ret_w_o")]
            y, *gathered = _mla_mixer(ahead.pop("mla"), j, u, g_q[j], g_kv[j], mla_cos, mla_sin, mla_perm, rw, nx, nxt)
            if nxt:
                ahead["ret_qkv"], ahead["ret_g"], ahead["ret_o"] = gathered
            w_in, w_out = ahead.pop("ffn_in"), ahead.pop("ffn_out")
        h1, u2 = rw(_f_ln_res_mod, ("row", "row", "grp", "par", "par", "grp", "grp"), (True,) * 7,
                    (d_model, d_model), "ln_a%d" % i)(
            h, y, grp(i, 2), ln_g[i, 0][None], ln_b[i, 0][None], grp(i, 4), grp(i, 3))
        if retention_layer and not last:
            ab, ahead["ffn_in"] = _make_mm("ffn_in%d" % i, 1)(u2, _columns(w_in), weights["ffn_w_in"][i + 1])
        else:
            (ab,) = _make_mm("ffn_in%d" % i)(u2, _columns(w_in))
        (act,) = rw(_f_swiglu, ("row",), (True,), (ab.shape[1] // 2,), "swiglu%d" % i, tile=WIDE_ROW_TILE)(ab)
        if retention_layer and not last:
            f, ahead["ffn_out"] = _make_mm("ffn_out%d" % i, 1)(act, _rows(w_out), weights["ffn_w_out"][i + 1])
        else:
            (f,) = _make_mm("ffn_out%d" % i)(act, _rows(w_out))
        (h,) = rw(_f_ln_res, ("row", "row", "grp", "par", "par"), (True,) * 5, (d_model,), "ln_f%d" % i)(
            h1, f, grp(i, 5), ln_g[i, 1][None], ln_b[i, 1][None])

    (rows,) = rw(_f_loss, ("row", "row"), (True, False), (LANE,), "loss")(h[:nx], tgt)
    return jnp.sum(rows[:, 0])


def _make_rowwise_plain_silu(cond):
    def body(c_ref, o_ref):
        o_ref[...] = _silu(c_ref[...])

    def call(c):
        return pl.pallas_call(body, name="silu_cond", out_shape=jax.ShapeDtypeStruct(c.shape, F32))(c)

    def bwd_body(c_ref, g_ref, o_ref):
        _, vjp = jax.vjp(_silu, c_ref[...])
        o_ref[...] = vjp(g_ref[...])[0]

    @jax.custom_vjp
    def op(c):
        return call(c)

    def fwd(c):
        return call(c), c

    def bwd(c, g):
        return (pl.pallas_call(bwd_body, name="silu_cond_bwd", out_shape=jax.ShapeDtypeStruct(c.shape, F32))(c, g),)

    op.defvjp(fwd, bwd)
    return op(cond)


MLA_MATRICES = ("mla_w_dq", "mla_w_uq", "mla_w_dkv", "mla_w_ukv", "mla_w_o")
LAYERED_MATRICES = ("ada_w", "ret_w_qkv", "ret_w_g", "ret_w_o", "ffn_w_in", "ffn_w_out") + MLA_MATRICES


def _mla_mixer(gathered, j, u, g_q, g_kv, cos, sin_signed, perm, rw, nx, ride):
    heads, dn, dr = MLA_HEADS, MLA_D_NOPE, MLA_D_ROPE
    w_dq, w_uq, w_dkv, w_ukv, w_o = gathered
    w_dq, w_dkv, w_ukv, w_o = _rows(w_dq), _rows(w_dkv), _columns(w_ukv), _rows(w_o)

    q_lora = w_uq.shape[1]
    w_uq = jnp.moveaxis(w_uq, 0, 1).reshape(q_lora, heads, dn + dr)
    w_uq = jnp.concatenate([w_uq, jnp.zeros((q_lora, heads, MLA_PAD - dr), w_uq.dtype)], axis=2)
    w_uq = w_uq.reshape(1, q_lora, heads * (dn + MLA_PAD))
    w_dkv = jnp.concatenate([w_dkv, jnp.zeros(w_dkv.shape[:2] + (MLA_PAD - dr,), w_dkv.dtype)], axis=2)

    (cq,) = _make_mm("mla_dq%d" % j)(u, w_dq)
    (cqn,) = rw(_f_rms, ("row", "par"), (True, True), (cq.shape[1],), "rms_q%d" % j)(cq, g_q[None])
    (q,) = _make_mm("mla_uq%d" % j)(cqn, w_uq)
    (ckv,) = _make_mm("mla_dkv%d" % j)(u, w_dkv)
    lora = ckv.shape[1] - MLA_PAD
    c_kv, kr = rw(_f_kv_latent, ("row", "par", "row", "row", "par"), (True, True, False, False, False),
                  (lora, MLA_PAD), "kv_lat%d" % j)(ckv, g_kv[None], cos, sin_signed, perm)
    (kv,) = _make_mm("mla_ukv%d" % j)(c_kv, w_ukv)
    o, *rode = _make_attention("l%d" % j, nx)(q, kv, kr, cos, sin_signed, perm, *ride)
    return (_make_mm("mla_o%d" % j)(o, w_o)[0], *rode)


WEIGHT_NAMES = ("c_ctx", "ada_w", "ada_b", "ln_g", "ln_b", "ret_w_qkv", "ret_w_g", "ret_decay_logit", "ret_w_o",
                "mla_w_dq", "mla_g_q", "mla_w_uq", "mla_w_dkv", "mla_g_kv", "mla_w_ukv", "mla_w_o",
                "ffn_w_in", "ffn_w_out")


def kernel(x, c, ctx, c_ctx, ada_w, ada_b, ln_g, ln_b, ret_w_qkv, ret_w_g, ret_decay_logit, ret_w_o, mla_w_dq, mla_g_q, mla_w_uq, mla_w_dkv, mla_g_kv, mla_w_ukv, mla_w_o, ffn_w_in, ffn_w_out, loss_target, m_c_ctx, m_ada_w, m_ada_b, m_ln_g, m_ln_b, m_ret_w_qkv, m_ret_w_g, m_ret_decay_logit, m_ret_w_o, m_mla_w_dq, m_mla_g_q, m_mla_w_uq, m_mla_w_dkv, m_mla_g_kv, m_mla_w_ukv, m_mla_w_o, m_ffn_w_in, m_ffn_w_out, v_c_ctx, v_ada_w, v_ada_b, v_ln_g, v_ln_b, v_ret_w_qkv, v_ret_w_g, v_ret_decay_logit, v_ret_w_o, v_mla_w_dq, v_mla_g_q, v_mla_w_uq, v_mla_w_dkv, v_mla_g_kv, v_mla_w_ukv, v_mla_w_o, v_ffn_w_in, v_ffn_w_out):
    given = dict(locals())
    weights = {n: given[n] for n in WEIGHT_NAMES}
    me = _my_index()
    c_all = _all_gather_call(c, "ag_cond").reshape(N_DEV, -1)

    per_layer = {n: [weights[n][l] for l in range(weights[n].shape[0])] for n in LAYERED_MATRICES}
    whole = {n: weights[n] for n in WEIGHT_NAMES if n not in LAYERED_MATRICES}

    def loss_of(layered, rest, x2):
        return _loss_fn({**rest, **layered}, x2, c_all, ctx[0], loss_target[0], me)

    loss, (grad_layers, grad_w, grad_x) = jax.value_and_grad(loss_of, argnums=(0, 1, 2))(per_layer, whole, x[0])
    grad_w = {**grad_w, **{n: jnp.stack(grad_layers[n]) for n in LAYERED_MATRICES}}
    loss = lax.psum(loss, AXES)
    delta, new_m, new_v = {}, {}, {}
    for n in WEIGHT_NAMES:
        delta[n], new_m[n], new_v[n] = _adamw(weights[n], grad_w[n], given["m_" + n], given["v_" + n], "adamw_" + n)
    return (loss, grad_x[None], *[grad_w[n] for n in WEIGHT_NAMES], *[delta[n] for n in WEIGHT_NAMES],
            *[new_m[n] for n in WEIGHT_NAMES], *[new_v[n] for n in WEIGHT_NAMES])
```

```python
import functools
import math

import jax
import jax.numpy as jnp
from jax import lax
from jax.experimental import pallas as pl
from jax.experimental.pallas import tpu as pltpu

F32 = jnp.float32
BF16 = jnp.bfloat16

AXES = ("x", "y", "c")
N_DEV = 8
MESH_IDS = pl.DeviceIdType.MESH

DEPTH = 4
GRID_W = 64
RET_HEADS = 8
RET_CHUNK = 128
RET_ROPE_BASE = 10000.0
GN_EPS = 1e-6
MLA_HEADS = 16
MLA_D_NOPE = 128
MLA_D_ROPE = 64
MLA_D_V = 128
AXIAL_ROPE_BASE = 10000.0
RMS_EPS = 1e-6
LN_EPS = 1e-5
DEEPNORM_ALPHA = (2 * DEPTH) ** 0.25

ADAM_LR = 0.001
ADAM_B1 = 0.9
ADAM_B2 = 0.999
ADAM_EPS = 1e-08
ADAM_WD = 0.01
ADAM_STEP = 10

LANE = 128
SUBLANE = 8
VMEM_LIMIT = 56 * 1024 * 1024

ROW_TILE = 256
WIDE_ROW_TILE = 64
MM_ROWS = 544
MLA_PAD = 128


def _div(n, cap, mult):
    best = None
    for d in range(mult, min(n, cap) + 1, mult):
        if n % d == 0:
            best = d
    return n if best is None else best


def _cparams(sem=None):
    kw = dict(vmem_limit_bytes=VMEM_LIMIT)
    if sem is not None:
        kw["dimension_semantics"] = sem
    return pltpu.CompilerParams(**kw)


_DN = {"nn": (((1,), (0,)), ((), ())), "nt": (((1,), (1,)), ((), ())), "tn": (((0,), (0,)), ((), ()))}


def _raw_dot(a, b, kind):
    return lax.dot_general(a.astype(BF16), b.astype(BF16), _DN[kind], preferred_element_type=F32)


@functools.partial(jax.custom_vjp, nondiff_argnums=(2,))
def _dot(a, b, kind):
    return _raw_dot(a, b, kind)


def _dot_fwd(a, b, kind):
    return _raw_dot(a, b, kind), (a, b)


def _dot_bwd(kind, res, g):
    a, b = res
    if kind == "nn":
        return _raw_dot(g, b, "nt"), _raw_dot(a, g, "tn")
    if kind == "nt":
        return _raw_dot(g, b, "nn"), _raw_dot(g, a, "tn")
    return _raw_dot(b, g, "nt"), _raw_dot(a, g, "nn")


_dot.defvjp(_dot_fwd, _dot_bwd)


def _sigmoid(x):
    return 1.0 / (1.0 + jnp.exp(-x))


def _silu(x):
    return x * _sigmoid(x)


def _my_index():
    return 4 * lax.axis_index("x") + 2 * lax.axis_index("y") + lax.axis_index("c")


def _all_gather_call(x, name):
    def body(x_ref, out_ref, send_sems, recv_sems, local_sem):
        x_, y_, c_ = lax.axis_index("x"), lax.axis_index("y"), lax.axis_index("c")
        me, sibling = (x_, y_, c_), (x_, y_, 1 - c_)
        chips = [(1 - x_, y_), (x_, 1 - y_), (1 - x_, 1 - y_)]

        def slot(px, py, pc):
            return out_ref.at[4 * px + 2 * py + pc]

        def copy(k, block, to, src=None):
            return pltpu.make_async_remote_copy(
                src_ref=slot(*block) if src is None else src, dst_ref=slot(*block),
                send_sem=send_sems.at[k], recv_sem=recv_sems.at[k],
                device_id=to, device_id_type=MESH_IDS)

        mine = pltpu.make_async_copy(x_ref, slot(*me), local_sem)
        mine.start()
        first = [copy(0, me, sibling, src=x_ref)]
        first += [copy(1 + j, me, (*chip, c_), src=x_ref) for j, chip in enumerate(chips)]
        for cp in first:
            cp.start()
        passed = [copy(4 + j, (*chip, c_), sibling) for j, chip in enumerate(chips)]
        for j, chip in enumerate(chips):
            copy(1 + j, (*chip, c_), me).wait_recv()
            passed[j].start()
        copy(0, sibling, me).wait_recv()
        for j, chip in enumerate(chips):
            copy(4 + j, (*chip, 1 - c_), me).wait_recv()
        for cp in first + passed:
            cp.wait_send()
        mine.wait()

    return pl.pallas_call(
        body, name=name,
        out_shape=jax.ShapeDtypeStruct((N_DEV,) + x.shape, x.dtype),
        in_specs=[pl.BlockSpec(memory_space=pl.ANY)],
        out_specs=pl.BlockSpec(memory_space=pl.ANY),
        scratch_shapes=[pltpu.SemaphoreType.DMA((7,)), pltpu.SemaphoreType.DMA((7,)), pltpu.SemaphoreType.DMA],
    )(x)


def _all_to_all_call(g, name):
    def body(g_ref, out_ref, send_sems, recv_sems, local_sem):
        x_, y_, c_ = lax.axis_index("x"), lax.axis_index("y"), lax.axis_index("c")
        my = 4 * x_ + 2 * y_ + c_
        peers = []
        for k in range(1, N_DEV):
            px = 1 - x_ if (k >> 2) & 1 else x_
            py = 1 - y_ if (k >> 1) & 1 else y_
            pc = 1 - c_ if k & 1 else c_
            peers.append((px, py, pc))

        def copy(k, peer):
            pid = 4 * peer[0] + 2 * peer[1] + peer[2]
            return pltpu.make_async_remote_copy(
                src_ref=g_ref.at[pid], dst_ref=out_ref.at[my],
                send_sem=send_sems.at[k], recv_sem=recv_sems.at[k],
                device_id=peer, device_id_type=MESH_IDS)

        def arrival(k, peer):
            pid = 4 * peer[0] + 2 * peer[1] + peer[2]
            return pltpu.make_async_remote_copy(
                src_ref=g_ref.at[pid], dst_ref=out_ref.at[pid],
                send_sem=send_sems.at[k], recv_sem=recv_sems.at[k],
                device_id=peer, device_id_type=MESH_IDS)

        mine = pltpu.make_async_copy(g_ref.at[my], out_ref.at[my], local_sem)
        mine.start()
        sends = [copy(k, peer) for k, peer in enumerate(peers)]
        for cp in sends:
            cp.start()
        for k, peer in enumerate(peers):
            arrival(k, peer).wait_recv()
        for cp in sends:
            cp.wait_send()
        mine.wait()

    return pl.pallas_call(
        body, name=name,
        out_shape=jax.ShapeDtypeStruct(g.shape, g.dtype),
        in_specs=[pl.BlockSpec(memory_space=pl.ANY)],
        out_specs=pl.BlockSpec(memory_space=pl.ANY),
        scratch_shapes=[pltpu.SemaphoreType.DMA((7,)), pltpu.SemaphoreType.DMA((7,)), pltpu.SemaphoreType.DMA],
    )(g)


def _sum_slots_call(g, name):
    slots, rows, cols = g.shape
    tr = _div(rows, max(SUBLANE, (1 << 19) // cols), 16) if rows % 16 == 0 else rows

    def body(g_ref, o_ref):
        acc = g_ref[0].astype(F32)
        for s in range(1, slots):
            acc = acc + g_ref[s].astype(F32)
        o_ref[...] = acc

    return pl.pallas_call(
        body, name=name, grid=(rows // tr,),
        out_shape=jax.ShapeDtypeStruct((rows, cols), F32),
        in_specs=[pl.BlockSpec((slots, tr, cols), lambda i: (0, i, 0))],
        out_specs=pl.BlockSpec((tr, cols), lambda i: (i, 0)),
        compiler_params=_cparams(("parallel",)),
    )(g)


def _pair_sum_call(g, half, name):
    _, rows, cols = g.shape
    tr = _div(rows, max(SUBLANE, (1 << 20) // cols), 16) if rows % 16 == 0 else rows
    core = lax.axis_index("c").astype(jnp.int32).reshape(1)

    def body(core_ref, g_ref, h_ref, o_ref):
        del core_ref
        o_ref[...] = (g_ref[...].astype(F32) + h_ref[...].astype(F32)).astype(o_ref.dtype)

    return pl.pallas_call(
        body, name=name,
        grid_spec=pltpu.PrefetchScalarGridSpec(
            num_scalar_prefetch=1, grid=(N_DEV // 2, rows // tr),
            in_specs=[pl.BlockSpec((None, tr, cols), lambda k, i, core_ref: (2 * k + core_ref[0], i, 0)),
                      pl.BlockSpec((None, tr, cols), lambda k, i, core_ref: (k, i, 0))],
            out_specs=pl.BlockSpec((None, tr, cols), lambda k, i, core_ref: (k, i, 0))),
        out_shape=jax.ShapeDtypeStruct(half.shape, half.dtype),
        compiler_params=_cparams(("parallel", "parallel")),
    )(core, g, half)


def _make_gather(tag, travel_dtype):
    @jax.custom_vjp
    def gather(w):
        return _all_gather_call(w.astype(travel_dtype), "ag_" + tag)

    def fwd(w):
        return gather(w), None

    def bwd(_, g):
        got = _all_to_all_call(g.astype(travel_dtype), "a2a_" + tag)
        return (_sum_slots_call(got, "sum_" + tag),)

    gather.defvjp(fwd, bwd)
    return gather


def _make_replicated(tag):
    @jax.custom_vjp
    def rep(p):
        return p

    def fwd(p):
        return p, None

    def bwd(_, g):
        return (_sum_slots_call(_all_gather_call(g, "ag_" + tag), "sum_" + tag),)

    rep.defvjp(fwd, bwd)
    return rep


class _Riders:
    def __init__(self, gathers=(), exchanges=(), sibling_swaps=(), chip_exchanges=()):
        self.gathers = list(gathers)
        self.exchanges = list(exchanges)
        self.sibling_swaps = list(sibling_swaps)
        self.chip_exchanges = list(chip_exchanges)
        self.n = len(self.operands())

    def operands(self):
        return self.gathers + self.exchanges + self.sibling_swaps + self.chip_exchanges

    def out_shape(self):
        return ([jax.ShapeDtypeStruct((N_DEV,) + g.shape, g.dtype) for g in self.gathers]
                + [jax.ShapeDtypeStruct(e.shape, e.dtype) for e in self.exchanges]
                + [jax.ShapeDtypeStruct((N_DEV // 2,) + s.shape[1:], s.dtype) for s in self.sibling_swaps]
                + [jax.ShapeDtypeStruct(p.shape, p.dtype) for p in self.chip_exchanges])

    def scratch(self):
        return [pltpu.SemaphoreType.DMA((7 * self.n,)), pltpu.SemaphoreType.DMA((7 * self.n,)),
                pltpu.SemaphoreType.DMA((self.n,))]

    def _gather_phase(self, phase, r, x_ref, out_ref, send_sems, recv_sems, local_sem):
        x_, y_, c_ = lax.axis_index("x"), lax.axis_index("y"), lax.axis_index("c")
        me, sibling = (x_, y_, c_), (x_, y_, 1 - c_)
        chips = [(1 - x_, y_), (x_, 1 - y_), (1 - x_, 1 - y_)]

        def slot(px, py, pc):
            return out_ref.at[4 * px + 2 * py + pc]

        def copy(k, block, to, src=None):
            return pltpu.make_async_remote_copy(
                src_ref=slot(*block) if src is None else src, dst_ref=slot(*block),
                send_sem=send_sems.at[7 * r + k], recv_sem=recv_sems.at[7 * r + k],
                device_id=to, device_id_type=MESH_IDS)

        def mine():
            return pltpu.make_async_copy(x_ref, slot(*me), local_sem.at[r])

        def first():
            return [copy(0, me, sibling, src=x_ref)] + [copy(1 + j, me, (*chip, c_), src=x_ref)
                                                        for j, chip in enumerate(chips)]

        def passed(j):
            return copy(4 + j, (*chips[j], c_), sibling)

        if phase == "start":
            mine().start()
            for cp in first():
                cp.start()
        elif phase == "pass":
            for j, chip in enumerate(chips):
                copy(1 + j, (*chip, c_), me).wait_recv()
                passed(j).start()
        else:
            copy(0, sibling, me).wait_recv()
            for j, chip in enumerate(chips):
                copy(4 + j, (*chip, 1 - c_), me).wait_recv()
            for cp in first() + [passed(j) for j in range(3)]:
                cp.wait_send()
            mine().wait()

    def _exchange_phase(self, phase, r, g_ref, out_ref, send_sems, recv_sems, local_sem):
        x_, y_, c_ = lax.axis_index("x"), lax.axis_index("y"), lax.axis_index("c")
        my = 4 * x_ + 2 * y_ + c_
        peers = []
        for k in range(1, N_DEV):
            peers.append((1 - x_ if (k >> 2) & 1 else x_, 1 - y_ if (k >> 1) & 1 else y_, 1 - c_ if k & 1 else c_))

        def copy(k, peer, arriving):
            pid = 4 * peer[0] + 2 * peer[1] + peer[2]
            return pltpu.make_async_remote_copy(
                src_ref=g_ref.at[pid], dst_ref=out_ref.at[pid if arriving else my],
                send_sem=send_sems.at[7 * r + k], recv_sem=recv_sems.at[7 * r + k],
                device_id=peer, device_id_type=MESH_IDS)

        def mine():
            return pltpu.make_async_copy(g_ref.at[my], out_ref.at[my], local_sem.at[r])

        if phase == "start":
            mine().start()
            for k, peer in enumerate(peers):
                copy(k, peer, False).start()
        elif phase == "finish":
            for k, peer in enumerate(peers):
                copy(k, peer, True).wait_recv()
            for k, peer in enumerate(peers):
                copy(k, peer, False).wait_send()
            mine().wait()

    def _sibling_phase(self, phase, r, g_ref, out_ref, send_sems, recv_sems):
        x_, y_, c_ = lax.axis_index("x"), lax.axis_index("y"), lax.axis_index("c")

        def copy(chip):
            return pltpu.make_async_remote_copy(
                src_ref=g_ref.at[2 * chip + 1 - c_], dst_ref=out_ref.at[chip],
                send_sem=send_sems.at[7 * r + chip], recv_sem=recv_sems.at[7 * r + chip],
                device_id=(x_, y_, 1 - c_), device_id_type=MESH_IDS)

        if phase == "start":
            for chip in range(N_DEV // 2):
                copy(chip).start()
        elif phase == "finish":
            for chip in range(N_DEV // 2):
                copy(chip).wait_recv()
            for chip in range(N_DEV // 2):
                copy(chip).wait_send()

    def _chip_phase(self, phase, r, p_ref, out_ref, send_sems, recv_sems, local_sem):
        x_, y_, c_ = lax.axis_index("x"), lax.axis_index("y"), lax.axis_index("c")
        my = 2 * x_ + y_
        peers = [(1 - x_ if (k >> 1) & 1 else x_, 1 - y_ if k & 1 else y_) for k in range(1, N_DEV // 2)]

        def copy(k, peer, arriving):
            pid = 2 * peer[0] + peer[1]
            return pltpu.make_async_remote_copy(
                src_ref=p_ref.at[pid], dst_ref=out_ref.at[pid if arriving else my],
                send_sem=send_sems.at[7 * r + k], recv_sem=recv_sems.at[7 * r + k],
                device_id=(*peer, c_), device_id_type=MESH_IDS)

        def mine():
            return pltpu.make_async_copy(p_ref.at[my], out_ref.at[my], local_sem.at[r])

        if phase == "start":
            mine().start()
            for k, peer in enumerate(peers):
                copy(k, peer, False).start()
        elif phase == "finish":
            for k, peer in enumerate(peers):
                copy(k, peer, True).wait_recv()
            for k, peer in enumerate(peers):
                copy(k, peer, False).wait_send()
            mine().wait()

    def phase(self, phase, in_refs, out_refs, send_sems, recv_sems, local_sem):
        ng, ne, ns = len(self.gathers), len(self.exchanges), len(self.sibling_swaps)
        for r in range(self.n):
            if r < ng:
                self._gather_phase(phase, r, in_refs[r], out_refs[r], send_sems, recv_sems, local_sem)
            elif r < ng + ne:
                self._exchange_phase(phase, r, in_refs[r], out_refs[r], send_sems, recv_sems, local_sem)
            elif r < ng + ne + ns:
                self._sibling_phase(phase, r, in_refs[r], out_refs[r], send_sems, recv_sems)
            else:
                self._chip_phase(phase, r, in_refs[r], out_refs[r], send_sems, recv_sems, local_sem)


def _call(body, *, name, grid, in_specs, out_specs, out_shape, operands, scratch_shapes=(), semantics, riders=None):
    if riders is None or riders.n == 0:
        return tuple(pl.pallas_call(
            body, name=name, grid=grid, out_shape=tuple(out_shape), in_specs=list(in_specs),
            out_specs=tuple(out_specs), scratch_shapes=list(scratch_shapes), compiler_params=_cparams(semantics),
        )(*operands))

    n_in, n_out, n_scr, rn = len(in_specs), len(out_specs), len(scratch_shapes), riders.n
    steps = math.prod(grid)

    def full_body(*refs):
        ins, r_in = refs[:n_in], refs[n_in:n_in + rn]
        outs, r_out = refs[n_in + rn:n_in + rn + n_out], refs[n_in + rn + n_out:n_in + 2 * rn + n_out]
        scr = refs[n_in + 2 * rn + n_out:n_in + 2 * rn + n_out + n_scr]
        sems = refs[n_in + 2 * rn + n_out + n_scr:]
        step = 0
        for axis, size in enumerate(grid):
            step = step * size + pl.program_id(axis)

        @pl.when(step == 0)
        def _():
            riders.phase("start", r_in, r_out, *sems)

        @pl.when(step == steps - 1 - steps // 8)
        def _():
            riders.phase("pass", r_in, r_out, *sems)

        body(*ins, *outs, *scr)

        @pl.when(step == steps - 1)
        def _():
            riders.phase("finish", r_in, r_out, *sems)

    hbm = pl.BlockSpec(memory_space=pl.ANY)
    return tuple(pl.pallas_call(
        full_body, name=name, grid=grid,
        out_shape=tuple(out_shape) + tuple(riders.out_shape()),
        in_specs=list(in_specs) + [hbm] * rn,
        out_specs=tuple(out_specs) + (hbm,) * rn,
        scratch_shapes=list(scratch_shapes) + riders.scratch(),
        compiler_params=_cparams(("arbitrary",) * len(grid)),
    )(*operands, *riders.operands()))


def _mm_nn_call(a, w3, name, riders=None):
    m, k = a.shape
    nb, _, ns = w3.shape
    tm = _div(m, MM_ROWS, SUBLANE)
    tn = _div(ns, max(LANE, (6 << 20) // (2 * k)), LANE)
    tps = ns // tn

    def body(a_ref, w_ref, o_ref):
        o_ref[...] = _raw_dot(a_ref[...], w_ref[...], "nn")

    return _call(
        body, name=name, grid=(nb * tps, m // tm),
        out_shape=[jax.ShapeDtypeStruct((m, nb * ns), F32)],
        in_specs=[pl.BlockSpec((tm, k), lambda j, i: (i, 0)),
                  pl.BlockSpec((None, k, tn), lambda j, i: (j // tps, 0, j % tps))],
        out_specs=[pl.BlockSpec((tm, tn), lambda j, i: (i, j))],
        operands=(a, w3), semantics=("parallel", "parallel"), riders=riders)


def _mm_nt_call(g, w3, name, riders=None):
    m, n = g.shape
    nb, k, ns = w3.shape
    tm = _div(m, MM_ROWS, 16) if m % 16 == 0 else m
    tk = _div(k, 2048, LANE)
    tn = _div(ns, max(LANE, (6 << 20) // (2 * tk)), LANE)
    tps = ns // tn
    nj = nb * tps
    emit = tk == k

    def body(g_ref, w_ref, o_ref, *rest):
        acc = rest[-1]
        j = pl.program_id(2)
        gb = g_ref[...].astype(BF16)
        if emit:
            rest[0][...] = gb
        p = _raw_dot(gb, w_ref[...], "nt")

        @pl.when(j == 0)
        def _():
            acc[...] = p

        @pl.when(j > 0)
        def _():
            acc[...] += p

        @pl.when(j == nj - 1)
        def _():
            o_ref[...] = acc[...]

    da, *rest = _call(
        body, name=name, grid=(m // tm, k // tk, nj),
        out_shape=[jax.ShapeDtypeStruct((m, k), F32)] + ([jax.ShapeDtypeStruct((m, n), BF16)] if emit else []),
        in_specs=[pl.BlockSpec((tm, tn), lambda i, kk, j: (i, j)),
                  pl.BlockSpec((None, tk, tn), lambda i, kk, j: (j // tps, kk, j % tps))],
        out_specs=[pl.BlockSpec((tm, tk), lambda i, kk, j: (i, kk))]
        + ([pl.BlockSpec((tm, tn), lambda i, kk, j: (i, j))] if emit else []),
        scratch_shapes=[pltpu.VMEM((tm, tk), F32)],
        operands=(g, w3), semantics=("parallel", "arbitrary", "arbitrary"), riders=riders)
    return (da, *rest) if emit else (da, g.astype(BF16), *rest)


def _mm_dw_call(a, g, nb, out_dtype, name, riders=None):
    m, k = a.shape
    n = g.shape[1]
    ns = n // nb
    tn = _div(ns, max(LANE, (13 << 20) // (2 * m)), LANE)
    tk = _div(k, max(LANE, (9 << 20) // (4 * m)), LANE)
    tps = ns // tn

    def body(a_ref, g_ref, o_ref):
        o_ref[...] = _raw_dot(a_ref[...], g_ref[...], "tn").astype(o_ref.dtype)

    return _call(
        body, name=name, grid=(nb * tps, k // tk),
        out_shape=[jax.ShapeDtypeStruct((nb, k, ns), out_dtype)],
        in_specs=[pl.BlockSpec((m, tk), lambda j, kk: (0, kk)),
                  pl.BlockSpec((m, tn), lambda j, kk: (0, j))],
        out_specs=[pl.BlockSpec((None, tk, tn), lambda j, kk: (j // tps, kk, j % tps))],
        operands=(a, g), semantics=("parallel", "parallel"), riders=riders)


def _sum_exchanged(exchanged, tag):
    return [_sum_slots_call(e, "sum_%s_%d" % (tag, k)) for k, e in enumerate(exchanged)]


def _make_mm(tag, n_ride=0):
    def impl(a, w3, *shards):
        return _mm_nn_call(a, w3, "mm_" + tag, _Riders(gathers=[s.astype(BF16) for s in shards]))

    @jax.custom_vjp
    def mm(a, w3, *shards):
        return impl(a, w3, *shards)

    def fwd(a, w3, *shards):
        return impl(a, w3, *shards), (a, w3)

    def bwd(res, cts):
        a, w3 = res
        d_gathered = list(cts[1:])
        da, gb, *halves = _mm_nt_call(cts[0], w3, "mm_da_" + tag, _Riders(sibling_swaps=d_gathered))
        pairs = [_pair_sum_call(d, h, "pair_%s_%d" % (tag, r)) for r, (d, h) in enumerate(zip(d_gathered, halves))]
        dw, *exchanged = _mm_dw_call(a, gb, w3.shape[0], w3.dtype, "mm_dw_" + tag,
                                     _Riders(chip_exchanges=pairs))
        return (da, dw, *_sum_exchanged(exchanged, tag))

    mm.defvjp(fwd, bwd)
    return mm


def _make_rowwise(f, kinds, diff, out_cols, tag, nx_rows, tile=ROW_TILE):
    n_in = len(kinds)
    nxt = nx_rows // tile

    def in_spec(kind, arr):
        if kind == "row":
            return pl.BlockSpec((tile, arr.shape[1]), lambda i: (i, 0))
        if kind == "pair":
            return pl.BlockSpec((arr.shape[0], tile, arr.shape[2]), lambda i: (0, i, 0))
        if kind == "grp":
            return pl.BlockSpec((None, 1, arr.shape[2]), lambda i: (i // nxt, 0, 0))
        return pl.BlockSpec(arr.shape, lambda i: (0,) * arr.ndim)

    def fwd_call(*args):
        t = next(a.shape[0] for a, kd in zip(args, kinds) if kd == "row")

        def body(*refs):
            outs = f(*[r[...] for r in refs[:n_in]])
            for r, o in zip(refs[n_in:], outs):
                r[...] = o

        return pl.pallas_call(
            body, name="rw_" + tag, grid=(t // tile,),
            out_shape=tuple(jax.ShapeDtypeStruct((t, c), F32) for c in out_cols),
            in_specs=[in_spec(kd, a) for kd, a in zip(kinds, args)],
            out_specs=tuple(pl.BlockSpec((tile, c), lambda i: (i, 0)) for c in out_cols),
            compiler_params=_cparams(("parallel",)),
        )(*args)

    def bwd_call(args, cts):
        t = cts[0].shape[0]
        didx = [i for i in range(n_in) if diff[i]]

        def body(*refs):
            i = pl.program_id(0)
            vals = [r[...] for r in refs[:n_in]]
            ct = tuple(r[...] for r in refs[n_in:n_in + len(out_cols)])
            outs = refs[n_in + len(out_cols):]

            def g(*dv):
                full = list(vals)
                for j, v in zip(didx, dv):
                    full[j] = v
                return tuple(f(*full))

            _, vjp = jax.vjp(g, *[vals[j] for j in didx])
            grads = vjp(ct)
            for j, o_ref, d in zip(didx, outs, grads):
                if kinds[j] in ("row", "pair"):
                    o_ref[...] = d
                else:
                    first = (i % nxt == 0) if kinds[j] == "grp" else (i == 0)

                    @pl.when(first)
                    def _(o_ref=o_ref, d=d):
                        o_ref[...] = d

                    @pl.when(jnp.logical_not(first))
                    def _(o_ref=o_ref, d=d):
                        o_ref[...] += d

        def out_spec(j):
            return in_spec(kinds[j], args[j])

        return pl.pallas_call(
            body, name="rw_bwd_" + tag, grid=(t // tile,),
            out_shape=tuple(jax.ShapeDtypeStruct(args[j].shape, F32) for j in didx),
            in_specs=[in_spec(kd, a) for kd, a in zip(kinds, args)]
            + [pl.BlockSpec((tile, c), lambda i: (i, 0)) for c in out_cols],
            out_specs=tuple(out_spec(j) for j in didx),
            compiler_params=_cparams(("arbitrary",)),
        )(*args, *cts)

    @jax.custom_vjp
    def op(*args):
        return fwd_call(*args)

    def fwd(*args):
        return fwd_call(*args), args

    def bwd(args, cts):
        grads = bwd_call(args, cts)
        full = [None] * n_in
        for j, gr in zip([i for i in range(n_in) if diff[i]], grads):
            full[j] = gr
        return tuple(jnp.zeros_like(a) if gfull is None else gfull for a, gfull in zip(args, full))

    op.defvjp(fwd, bwd)
    return op


def _layer_norm(z, g, b):
    mu = jnp.mean(z, axis=-1, keepdims=True)
    var = jnp.mean(jnp.square(z - mu), axis=-1, keepdims=True)
    return (z - mu) * lax.rsqrt(var + LN_EPS) * g + b


def _f_modulate(h, sc, sh):
    return (h * (1.0 + sc) + sh,)


def _f_ln_res(h, y, gate, g, b):
    return (_layer_norm(DEEPNORM_ALPHA * h + gate * y, g, b),)


def _f_ln_res_mod(h, y, gate, g, b, sc, sh):
    h1 = _layer_norm(DEEPNORM_ALPHA * h + gate * y, g, b)
    return h1, h1 * (1.0 + sc) + sh


def _f_swiglu(ab):
    half = ab.shape[1] // 2
    return (_silu(ab[:, :half]) * ab[:, half:],)


def _f_ret_combine(gates, o):
    o_f, o_b = o[0], o[1]
    hv = o_f.shape[1]
    dv = hv // RET_HEADS

    def gn(o):
        parts = []
        for h in range(RET_HEADS):
            oh = o[:, h * dv:(h + 1) * dv]
            mu = jnp.mean(oh, axis=-1, keepdims=True)
            var = jnp.mean(jnp.square(oh - mu), axis=-1, keepdims=True)
            parts.append((oh - mu) * lax.rsqrt(var + GN_EPS))
        return jnp.concatenate(parts, axis=1)

    return (_silu(gates[:, :hv]) * gn(o_f) + _silu(gates[:, hv:]) * gn(o_b),)


def _rms(x, g):
    return x * lax.rsqrt(jnp.mean(jnp.square(x), axis=-1, keepdims=True) + RMS_EPS) * g


def _f_rms(x, g):
    return (_rms(x, g),)


def _rot(x, cos, sin_signed, perm):
    swapped = jnp.dot(x, perm, precision=lax.Precision.HIGHEST, preferred_element_type=F32)
    return x * cos + swapped * sin_signed


def _f_kv_latent(ckv, g, cos, sin_signed, perm):
    lora = ckv.shape[1] - MLA_PAD
    return _rms(ckv[:, :lora], g), _rot(ckv[:, lora:], cos, sin_signed, perm)


def _f_loss(y, tgt):
    row = 0.5 * jnp.mean(jnp.square(y - tgt), axis=-1, keepdims=True)
    return (jnp.broadcast_to(row, (y.shape[0], LANE)),)


def _ret_step(q, k, v, state, lg, cos, sin, d):
    c, dk = q.shape
    half = dk // 2

    def rope(t):
        t1, t2 = t[:, :half], t[:, half:]
        return jnp.concatenate([t1 * cos - t2 * sin, t1 * sin + t2 * cos], axis=1)

    q = rope(q)
    k = rope(k * (dk ** -0.5))
    sgn = (1 - 2 * d).astype(F32)
    ii = lax.broadcasted_iota(jnp.int32, (c, c), 0).astype(F32)
    jj = lax.broadcasted_iota(jnp.int32, (c, c), 1).astype(F32)
    e = (ii - jj) * sgn
    intra = jnp.where(e >= 0, jnp.exp(lg * jnp.maximum(e, 0.0)), 0.0)
    idx = lax.broadcasted_iota(jnp.int32, (c, 1), 0).astype(F32)
    pos = jnp.where(d == 0, idx, c - 1.0 - idx)
    q_dec = jnp.exp(lg * (pos + 1.0))
    k_dec = jnp.exp(lg * (c - 1.0 - pos))
    c_dec = jnp.exp(lg * float(c))
    scores = _dot(q, k, "nt") * intra
    o = _dot(scores, v, "nn") + _dot(q * q_dec, state, "nn")
    new_state = state * c_dec + _dot(k * k_dec, v, "tn")
    return o, new_state


def _ret_chunk_of(d, p, nxc, nc):
    return (1 - d) * ((p + nxc) % nc) + d * (nc - 1 - p)


RET_GROUP = 8
RET_GROUPS = RET_HEADS // RET_GROUP


def _ret_specs(t, dk, dv, nxc, nc, step_of):
    c, g = RET_CHUNK, RET_GROUP

    def chunk(d, h, s):
        return _ret_chunk_of(d, step_of(s), nxc, nc)

    q_spec = pl.BlockSpec((c, g * dk), lambda d, h, s: (chunk(d, h, s), h))
    k_spec = pl.BlockSpec((c, g * dk), lambda d, h, s: (chunk(d, h, s), RET_GROUPS + h))
    v_spec = pl.BlockSpec((c, g * dv), lambda d, h, s: (chunk(d, h, s), (2 * RET_HEADS * dk) // (g * dv) + h))
    tab_spec = pl.BlockSpec((c, dk // 2), lambda d, h, s: (chunk(d, h, s), 0))
    lg_spec = pl.BlockSpec((g, SUBLANE, LANE), lambda d, h, s: (d * RET_GROUPS + h, 0, 0))
    o_spec = pl.BlockSpec((None, c, g * dv), lambda d, h, s: (d, chunk(d, h, s), h))
    st_spec = pl.BlockSpec((None, g, None, dk, dv), lambda d, h, s: (d, h, step_of(s), 0, 0))
    return q_spec, k_spec, v_spec, tab_spec, lg_spec, o_spec, st_spec


def _make_retention(tag, nx_rows):
    def dims(qkv):
        t, w = qkv.shape
        dk = w // (4 * RET_HEADS)
        return t, dk, 2 * dk, nx_rows // RET_CHUNK, t // RET_CHUNK

    def fwd_call(qkv, cos, sin, lgb, riders):
        t, dk, dv, nxc, nc = dims(qkv)
        q_spec, k_spec, v_spec, tab_spec, lg_spec, o_spec, st_spec = _ret_specs(t, dk, dv, nxc, nc, lambda s: s)

        def body(q_ref, k_ref, v_ref, cos_ref, sin_ref, lg_ref, o_ref, st_ref, state):
            d = pl.program_id(0)

            @pl.when(pl.program_id(2) == 0)
            def _():
                state[...] = jnp.zeros_like(state)

            st_ref[...] = state[...]
            cos_t, sin_t = cos_ref[...], sin_ref[...]
            for g in range(RET_GROUP):
                o, new_state = _ret_step(q_ref[:, g * dk:(g + 1) * dk], k_ref[:, g * dk:(g + 1) * dk],
                                         v_ref[:, g * dv:(g + 1) * dv], state[g], lg_ref[g, 0:1, 0:1],
                                         cos_t, sin_t, d)
                o_ref[:, g * dv:(g + 1) * dv] = o
                state[g] = new_state

        return _call(
            body, name="ret_" + tag, grid=(2, RET_GROUPS, nc),
            out_shape=(jax.ShapeDtypeStruct((2, t, RET_HEADS * dv), F32),
                       jax.ShapeDtypeStruct((2, RET_HEADS, nc, dk, dv), F32)),
            in_specs=[q_spec, k_spec, v_spec, tab_spec, tab_spec, lg_spec],
            out_specs=(o_spec, st_spec),
            scratch_shapes=[pltpu.VMEM((RET_GROUP, dk, dv), F32)],
            operands=(qkv, qkv, qkv, cos, sin, lgb),
            semantics=("parallel", "parallel", "arbitrary"), riders=riders)

    def bwd_call(qkv, cos, sin, lgb, states, do, riders):
        t, dk, dv, nxc, nc = dims(qkv)
        q_spec, k_spec, v_spec, tab_spec, lg_spec, o_spec, st_spec = _ret_specs(
            t, dk, dv, nxc, nc, lambda s: nc - 1 - s)
        c = RET_CHUNK

        def chunk(d, h, s):
            return _ret_chunk_of(d, nc - 1 - s, nxc, nc)

        dq_spec = pl.BlockSpec((None, c, RET_GROUP * dk), lambda d, h, s: (d, chunk(d, h, s), h))
        dv_spec = pl.BlockSpec((None, c, RET_GROUP * dv), lambda d, h, s: (d, chunk(d, h, s), h))

        def body(q_ref, k_ref, v_ref, cos_ref, sin_ref, lg_ref, st_ref, do_ref,
                 dq_ref, dk_ref, dv_ref, dlg_ref, dstate):
            d = pl.program_id(0)
            s = pl.program_id(2)

            @pl.when(s == 0)
            def _():
                dstate[...] = jnp.zeros_like(dstate)

            cos_t, sin_t = cos_ref[...], sin_ref[...]

            def step(q, k, v, state, lg):
                return _ret_step(q, k, v, state, lg, cos_t, sin_t, d)

            corner = jnp.logical_and(lax.broadcasted_iota(jnp.int32, (SUBLANE, LANE), 0) == 0,
                                     lax.broadcasted_iota(jnp.int32, (SUBLANE, LANE), 1) == 0)
            for g in range(RET_GROUP):
                qs, vs = slice(g * dk, (g + 1) * dk), slice(g * dv, (g + 1) * dv)
                _, vjp = jax.vjp(step, q_ref[:, qs], k_ref[:, qs], v_ref[:, vs], st_ref[g], lg_ref[g, 0:1, 0:1])
                dq, dkk, dvv, dst, dlg = vjp((do_ref[:, vs], dstate[g]))
                dq_ref[:, qs] = dq
                dk_ref[:, qs] = dkk
                dv_ref[:, vs] = dvv
                dstate[g] = dst
                dlg_full = jnp.where(corner, dlg, 0.0)

                @pl.when(s == 0)
                def _(g=g, dlg_full=dlg_full):
                    dlg_ref[g] = dlg_full

                @pl.when(s > 0)
                def _(g=g, dlg_full=dlg_full):
                    dlg_ref[g] += dlg_full

        return _call(
            body, name="ret_bwd_" + tag, grid=(2, RET_GROUPS, nc),
            out_shape=(jax.ShapeDtypeStruct((2, t, RET_HEADS * dk), F32),
                       jax.ShapeDtypeStruct((2, t, RET_HEADS * dk), F32),
                       jax.ShapeDtypeStruct((2, t, RET_HEADS * dv), F32),
                       jax.ShapeDtypeStruct(lgb.shape, F32)),
            in_specs=[q_spec, k_spec, v_spec, tab_spec, tab_spec, lg_spec, st_spec, o_spec],
            out_specs=(dq_spec, dq_spec, dv_spec, lg_spec),
            scratch_shapes=[pltpu.VMEM((RET_GROUP, dk, dv), F32)],
            operands=(qkv, qkv, qkv, cos, sin, lgb, states, do),
            semantics=("parallel", "parallel", "arbitrary"), riders=riders)

    def impl(qkv, cos, sin, lgb, *shards):
        return fwd_call(qkv, cos, sin, lgb, _Riders(gathers=[s.astype(BF16) for s in shards]))

    @jax.custom_vjp
    def ret(qkv, cos, sin, lgb, *shards):
        o, _, *gathered = impl(qkv, cos, sin, lgb, *shards)
        return (o, *gathered)

    def fwd(qkv, cos, sin, lgb, *shards):
        o, states, *gathered = impl(qkv, cos, sin, lgb, *shards)
        return (o, *gathered), (qkv, cos, sin, lgb, states)

    def bwd(res, cts):
        qkv, cos, sin, lgb, states = res
        dq, dkk, dvv, dlg, *exchanged = bwd_call(qkv, cos, sin, lgb, states, cts[0],
                                                 _Riders(exchanges=list(cts[1:])))
        dqkv = jnp.concatenate([dq[0] + dq[1], dkk[0] + dkk[1], dvv[0] + dvv[1]], axis=1)
        return (dqkv, jnp.zeros_like(cos), jnp.zeros_like(sin), dlg, *_sum_exchanged(exchanged, "ret_" + tag))

    ret.defvjp(fwd, bwd)
    return ret


MLA_SCALE = (MLA_D_NOPE + MLA_D_ROPE) ** -0.5
MLA_Q_ROWS = 256
MLA_KEY_BLOCKS = 2


def _attn_query(q, cos, sin_signed, perm):
    qr = _rot(q[:, MLA_D_NOPE:], cos, sin_signed, perm)
    return jnp.concatenate([q[:, :MLA_D_NOPE], qr], axis=1).astype(BF16)


def _make_attention(tag, nx_rows):
    hw = MLA_D_NOPE + MLA_PAD
    tq = MLA_Q_ROWS

    def operands(kv, kr):
        t = kv.shape[0]
        kv3 = kv.reshape(t, MLA_HEADS, MLA_D_NOPE + MLA_D_V)
        k = jnp.concatenate([kv3[..., :MLA_D_NOPE], jnp.broadcast_to(kr[:, None, :], (t, MLA_HEADS, MLA_PAD))],
                            axis=2).reshape(t, MLA_HEADS * hw).astype(BF16)
        v = kv3[..., MLA_D_NOPE:].reshape(t, MLA_HEADS * MLA_D_V).astype(BF16)
        return k, k.T, v

    def softmax_t(s_t):
        m = jnp.max(s_t, axis=0, keepdims=True)
        e = jnp.exp((s_t - m) * MLA_SCALE)
        return e, m, 1.0 / jnp.sum(e, axis=0, keepdims=True)

    def fwd_call(q, k_t, v, cos, sin_signed, perm, riders):
        t = q.shape[0]
        nctx = t - nx_rows

        def body(q_ref, cos_ref, sin_ref, perm_ref, kt_ref, v_ref, o_ref, m_ref, linv_ref):
            i = pl.program_id(1)
            qb = _attn_query(q_ref[...], cos_ref[...], sin_ref[...], perm_ref[...])

            def attend(kt_part, v_part):
                s = _raw_dot(qb, kt_part, "nn")
                m = jnp.max(s, axis=1, keepdims=True)
                e = jnp.exp((s - m) * MLA_SCALE)
                linv = 1.0 / jnp.sum(e, axis=1, keepdims=True)
                o_ref[...] = _raw_dot(e * linv, v_part, "nn")
                m_ref[...] = m
                linv_ref[...] = linv

            @pl.when(i * tq < nx_rows)
            def _():
                attend(kt_ref[...], v_ref[...])

            @pl.when(i * tq >= nx_rows)
            def _():
                attend(kt_ref[:, pl.ds(nx_rows, nctx)], v_ref[pl.ds(nx_rows, nctx), :])

        stat = jax.ShapeDtypeStruct((MLA_HEADS, t, 1), F32)
        stat_spec = pl.BlockSpec((None, tq, 1), lambda h, i: (h, i, 0))
        o, m, linv, *gathered = _call(
            body, name="mla_" + tag, grid=(MLA_HEADS, t // tq),
            out_shape=(jax.ShapeDtypeStruct((t, MLA_HEADS * MLA_D_V), F32), stat, stat),
            in_specs=[pl.BlockSpec((tq, hw), lambda h, i: (i, h)),
                      pl.BlockSpec((tq, MLA_PAD), lambda h, i: (i, 0)),
                      pl.BlockSpec((tq, MLA_PAD), lambda h, i: (i, 0)),
                      pl.BlockSpec((MLA_PAD, MLA_PAD), lambda h, i: (0, 0)),
                      pl.BlockSpec((hw, t), lambda h, i: (h, 0)),
                      pl.BlockSpec((t, MLA_D_V), lambda h, i: (0, h))],
            out_specs=(pl.BlockSpec((tq, MLA_D_V), lambda h, i: (i, h)), stat_spec, stat_spec),
            operands=(q, cos, sin_signed, perm, k_t, v), semantics=("parallel", "arbitrary"), riders=riders)
        return (o, m.reshape(MLA_HEADS, 1, t), linv.reshape(MLA_HEADS, 1, t), *gathered)

    def bwd_call(q, k, k_t, v, cos, sin_signed, perm, o, m, linv, do, riders):
        t = q.shape[0]
        nctx = t - nx_rows

        def body(q_ref, cos_ref, sin_ref, perm_ref, k_ref, kt_ref, v_ref, o_ref, m_ref, linv_ref, do_ref,
                 dq_ref, dkn_ref, dv_ref, dkr_ref):
            i = pl.program_id(1)
            cos_t, sin_t, perm_t = cos_ref[...], sin_ref[...], perm_ref[...]
            qb = _attn_query(q_ref[...], cos_t, sin_t, perm_t)
            do_t = do_ref[...]
            dob = do_t.astype(BF16)
            delta = jnp.sum((do_t * o_ref[...]).T, axis=0, keepdims=True)
            m_t, linv_t = m_ref[...], linv_ref[...]

            def key_rows(rows):
                e = jnp.exp((_raw_dot(k_ref[rows, :], qb, "nt") - m_t) * MLA_SCALE)
                p_t = e * linv_t
                dp_t = _raw_dot(v_ref[rows, :], dob, "nt")
                ds_t = (p_t * (dp_t - delta) * MLA_SCALE).astype(BF16)
                dk = _raw_dot(ds_t, qb, "nn")
                dkn_ref[rows, :] += dk[:, :MLA_D_NOPE]
                dkr_ref[rows, :] += dk[:, MLA_D_NOPE:]
                dv_ref[rows, :] += _raw_dot(p_t, dob, "nn")
                return _raw_dot(kt_ref[:, rows], ds_t, "nn")

            def finish(dq_t):
                dqb = dq_t.T
                g_r = dqb[:, MLA_D_NOPE:]
                dq_r = g_r * cos_t + jnp.dot(g_r * sin_t, perm_t, precision=lax.Precision.HIGHEST,
                                             preferred_element_type=F32)
                dq_ref[...] = jnp.concatenate([dqb[:, :MLA_D_NOPE], dq_r], axis=1)

            @pl.when(i == 0)
            def _():
                dkn_ref[...] = jnp.zeros_like(dkn_ref)
                dv_ref[...] = jnp.zeros_like(dv_ref)

            @pl.when(jnp.logical_and(i == 0, pl.program_id(0) == 0))
            def _():
                dkr_ref[...] = jnp.zeros_like(dkr_ref)

            @pl.when(i * tq < nx_rows)
            def _():
                size = t // MLA_KEY_BLOCKS
                dq_t = key_rows(pl.ds(0, size))
                for b in range(1, MLA_KEY_BLOCKS):
                    dq_t = dq_t + key_rows(pl.ds(b * size, size))
                finish(dq_t)

            @pl.when(i * tq >= nx_rows)
            def _():
                finish(key_rows(pl.ds(nx_rows, nctx)))

        stat_spec = pl.BlockSpec((None, 1, tq), lambda h, i: (h, 0, i))
        return _call(
            body, name="mla_bwd_" + tag, grid=(MLA_HEADS, t // tq),
            out_shape=(jax.ShapeDtypeStruct(q.shape, F32),
                       jax.ShapeDtypeStruct((t, MLA_HEADS * MLA_D_NOPE), F32),
                       jax.ShapeDtypeStruct((t, MLA_HEADS * MLA_D_V), F32),
                       jax.ShapeDtypeStruct((t, MLA_PAD), F32)),
            in_specs=[pl.BlockSpec((tq, hw), lambda h, i: (i, h)),
                      pl.BlockSpec((tq, MLA_PAD), lambda h, i: (i, 0)),
                      pl.BlockSpec((tq, MLA_PAD), lambda h, i: (i, 0)),
                      pl.BlockSpec((MLA_PAD, MLA_PAD), lambda h, i: (0, 0)),
                      pl.BlockSpec((t, hw), lambda h, i: (0, h)),
                      pl.BlockSpec((hw, t), lambda h, i: (h, 0)),
                      pl.BlockSpec((t, MLA_D_V), lambda h, i: (0, h)),
                      pl.BlockSpec((tq, MLA_D_V), lambda h, i: (i, h)),
                      stat_spec, stat_spec,
                      pl.BlockSpec((tq, MLA_D_V), lambda h, i: (i, h))],
            out_specs=(pl.BlockSpec((tq, hw), lambda h, i: (i, h)),
                       pl.BlockSpec((t, MLA_D_NOPE), lambda h, i: (0, h)),
                       pl.BlockSpec((t, MLA_D_V), lambda h, i: (0, h)),
                       pl.BlockSpec((t, MLA_PAD), lambda h, i: (0, 0))),
            operands=(q, cos, sin_signed, perm, k, k_t, v, o, m, linv, do),
            semantics=("arbitrary", "arbitrary"), riders=riders)

    def impl(q, kv, kr, cos, sin_signed, perm, *shards):
        k, k_t, v = operands(kv, kr)
        o, m, linv, *gathered = fwd_call(q, k_t, v, cos, sin_signed, perm,
                                         _Riders(gathers=[s.astype(BF16) for s in shards]))
        return (o, *gathered), (q, k, k_t, v, cos, sin_signed, perm, o, m, linv)

    @jax.custom_vjp
    def attn(q, kv, kr, cos, sin_signed, perm, *shards):
        return impl(q, kv, kr, cos, sin_signed, perm, *shards)[0]

    def fwd(q, kv, kr, cos, sin_signed, perm, *shards):
        return impl(q, kv, kr, cos, sin_signed, perm, *shards)

    def bwd(res, cts):
        q, k, k_t, v, cos, sin_signed, perm, o, m, linv = res
        dq, dkn, dv, dkr, *exchanged = bwd_call(q, k, k_t, v, cos, sin_signed, perm, o, m, linv, cts[0],
                                                _Riders(exchanges=list(cts[1:])))
        t = q.shape[0]
        dkv = jnp.concatenate([dkn.reshape(t, MLA_HEADS, MLA_D_NOPE), dv.reshape(t, MLA_HEADS, MLA_D_V)],
                              axis=2).reshape(t, -1)
        return (dq, dkv, dkr, jnp.zeros_like(cos), jnp.zeros_like(sin_signed), jnp.zeros_like(perm),
                *_sum_exchanged(exchanged, "mla_" + tag))

    attn.defvjp(fwd, bwd)
    return attn


def _adamw_call(w, g, m, v, name):
    rows, cols = w.shape
    tr = _div(rows, max(SUBLANE, (1 << 18) // cols), SUBLANE) if rows % SUBLANE == 0 else rows

    def body(w_ref, g_ref, m_ref, v_ref, d_ref, nm_ref, nv_ref):
        gg = g_ref[...]
        nm = ADAM_B1 * m_ref[...] + (1.0 - ADAM_B1) * gg
        nv = ADAM_B2 * v_ref[...] + (1.0 - ADAM_B2) * jnp.square(gg)
        m_hat = nm / (1.0 - ADAM_B1 ** ADAM_STEP)
        v_hat = nv / (1.0 - ADAM_B2 ** ADAM_STEP)
        d_ref[...] = -ADAM_LR * (m_hat / (jnp.sqrt(v_hat) + ADAM_EPS) + ADAM_WD * w_ref[...])
        nm_ref[...] = nm
        nv_ref[...] = nv

    spec = pl.BlockSpec((tr, cols), lambda i: (i, 0))
    return pl.pallas_call(
        body, name=name, grid=(rows // tr,),
        out_shape=tuple(jax.ShapeDtypeStruct((rows, cols), F32) for _ in range(3)),
        in_specs=[spec] * 4, out_specs=(spec,) * 3,
        compiler_params=_cparams(("parallel",)),
    )(w, g, m, v)


def _adamw(w, g, m, v, name):
    shape = w.shape
    if w.ndim >= 2 and shape[-1] % LANE == 0:
        two = (-1, shape[-1])
    elif w.size % LANE == 0:
        two = (-1, LANE)
    else:
        two = (1, w.size)
    outs = _adamw_call(w.reshape(two), g.reshape(two), m.reshape(two), v.reshape(two), name)
    return tuple(o.reshape(shape) for o in outs)


def _ret_tables(nx, nctx, dk):
    inv = RET_ROPE_BASE ** (-jnp.linspace(0.0, 1.0, dk // 2, dtype=F32))
    ang = jnp.arange(nx, dtype=F32)[:, None] * inv[None, :]
    cos = jnp.concatenate([jnp.cos(ang), jnp.ones((nctx, dk // 2), F32)], axis=0)
    sin = jnp.concatenate([jnp.sin(ang), jnp.zeros((nctx, dk // 2), F32)], axis=0)
    return cos, sin


def _mla_tables(nx, nctx):
    quarter = MLA_D_ROPE // 4
    inv = AXIAL_ROPE_BASE ** (-jnp.arange(quarter, dtype=F32) * 2.0 / (MLA_D_ROPE // 2))
    tok = jnp.arange(nx)
    r_ang = (tok // GRID_W).astype(F32)[:, None] * inv[None, :]
    c_ang = (tok % GRID_W).astype(F32)[:, None] * inv[None, :]
    pad = MLA_PAD - MLA_D_ROPE
    cos = jnp.concatenate([jnp.cos(r_ang), jnp.cos(r_ang), jnp.cos(c_ang), jnp.cos(c_ang),
                           jnp.ones((nx, pad), F32)], axis=1)
    sin = jnp.concatenate([-jnp.sin(r_ang), jnp.sin(r_ang), -jnp.sin(c_ang), jnp.sin(c_ang),
                           jnp.zeros((nx, pad), F32)], axis=1)
    cos = jnp.concatenate([cos, jnp.ones((nctx, MLA_PAD), F32)], axis=0)
    sin = jnp.concatenate([sin, jnp.zeros((nctx, MLA_PAD), F32)], axis=0)
    lane = jnp.arange(MLA_PAD)
    partner = jnp.where(lane % (2 * quarter) < quarter, lane + quarter, lane - quarter)
    perm = ((lane[:, None] == partner[None, :]) & (lane[None, :] < MLA_D_ROPE)).astype(F32)
    return cos, sin, perm


def _columns(gathered):
    return gathered


def _rows(gathered):
    return gathered.reshape(1, gathered.shape[0] * gathered.shape[1], gathered.shape[2])


def _unshard_vec(gathered):
    return jnp.moveaxis(gathered, 0, 1).reshape(gathered.shape[1], -1)


def _pad_rows(a, rows):
    return jnp.concatenate([a, jnp.zeros((rows - a.shape[0],) + a.shape[1:], a.dtype)], axis=0)


def _loss_fn(weights, x, c_all, ctx, tgt, me):
    nx, d_model = x.shape
    nctx = ctx.shape[0]
    dk = d_model // RET_HEADS
    ret_cos, ret_sin = _ret_tables(nx, nctx, dk)
    mla_cos, mla_sin, mla_perm = _mla_tables(nx, nctx)

    rw = functools.partial(_make_rowwise, nx_rows=nx)

    n_ln = weights["ln_g"].size
    small = jnp.concatenate([weights["ln_g"].reshape(-1), weights["ln_b"].reshape(-1),
                             weights["mla_g_q"].reshape(-1), weights["mla_g_kv"].reshape(-1)])
    n_small = small.size
    small = _pad_rows(small.reshape(-1, 1), -(-n_small // LANE) * LANE).reshape(-1, LANE)
    small = _make_gather("small", F32)(small).reshape(N_DEV, -1)
    ln_g = _unshard_vec(small[:, :n_ln].reshape(N_DEV, DEPTH * 2, -1)).reshape(DEPTH, 2, d_model)
    ln_b = _unshard_vec(small[:, n_ln:2 * n_ln].reshape(N_DEV, DEPTH * 2, -1)).reshape(DEPTH, 2, d_model)
    n_g = weights["mla_g_q"].size
    g_q = _unshard_vec(small[:, 2 * n_ln:2 * n_ln + n_g].reshape(N_DEV, DEPTH // 2, -1))
    g_kv = _unshard_vec(small[:, 2 * n_ln + n_g:2 * n_ln + 2 * n_g].reshape(N_DEV, DEPTH // 2, -1))

    n_b = weights["ada_b"].size
    n_dec = weights["ret_decay_logit"].size
    repl = jnp.concatenate([weights["c_ctx"].reshape(-1), weights["ada_b"].reshape(-1),
                            weights["ret_decay_logit"].reshape(-1)])
    n_repl = repl.size
    repl = _pad_rows(repl.reshape(-1, 1), -(-n_repl // LANE) * LANE).reshape(-1, LANE)
    repl = _make_replicated("repl")(repl).reshape(-1)
    c_ctx = repl[:d_model]
    ada_b = repl[d_model:d_model + n_b].reshape(DEPTH, 6 * d_model)
    decay = repl[d_model + n_b:d_model + n_b + n_dec].reshape(weights["ret_decay_logit"].shape)

    cond_rows = 2 * SUBLANE
    cond = _pad_rows(jnp.concatenate([c_all, c_ctx[None, :]], axis=0), cond_rows)
    s_cond = _make_rowwise_plain_silu(cond)
    mods = [_make_mm("ada%d" % i)(s_cond, weights["ada_w"][i][None])[0] for i in range(DEPTH)]
    mod = jnp.concatenate(mods, axis=0)
    mod = _make_gather("mod", F32)(mod)
    mod = jnp.moveaxis(mod.reshape(N_DEV, DEPTH, cond_rows, -1), 0, 2).reshape(DEPTH, cond_rows, -1)
    mod = mod + ada_b[:, None, :]
    mine = lax.dynamic_slice_in_dim(mod, me, 1, axis=1)
    mod = jnp.concatenate([mine, mod[:, N_DEV:N_DEV + 1]], axis=1)
    mod = mod.reshape(DEPTH, 2, 6, 1, d_model)

    def grp(i, j):
        return mod[i, :, j]

    h = jnp.concatenate([x, ctx], axis=0)
    ahead = {"ret_qkv": _make_gather("ret_qkv", BF16)(weights["ret_w_qkv"][0])}
    for i in range(DEPTH):
        j = i // 2
        retention_layer, last = i % 2 == 0, i + 1 == DEPTH
        (u,) = rw(_f_modulate, ("row", "grp", "grp"), (True, True, True), (d_model,), "mod%d" % i)(
            h, grp(i, 1), grp(i, 0))
        if retention_layer:
            lg = jax.nn.log_sigmoid(decay[j]).reshape(2 * RET_HEADS, 1, 1)
            lgb = jnp.broadcast_to(lg, (2 * RET_HEADS, SUBLANE, LANE))
            if "ret_g" in ahead:
                qkv, w_in = _make_mm("ret_qkv%d" % j, 1)(u, _columns(ahead.pop("ret_qkv")), weights["ffn_w_in"][i])
                gates, w_out = _make_mm("ret_g%d" % j, 1)(u, _columns(ahead.pop("ret_g")), weights["ffn_w_out"][i])
                w_o = ahead.pop("ret_o")
                (o,) = _make_retention("l%d" % j, nx)(qkv, ret_cos, ret_sin, lgb)
            else:
                qkv, w_g = _make_mm("ret_qkv%d" % j, 1)(u, _columns(ahead.pop("ret_qkv")), weights["ret_w_g"][j])
                gates, w_o, w_in = _make_mm("ret_g%d" % j, 2)(u, _columns(w_g), weights["ret_w_o"][j],
                                                              weights["ffn_w_in"][i])
                o, w_out = _make_retention("l%d" % j, nx)(qkv, ret_cos, ret_sin, lgb, weights["ffn_w_out"][i])
            (comb,) = rw(_f_ret_combine, ("row", "pair"), (True, True), (o.shape[2],),
                         "ret_comb%d" % j, tile=WIDE_ROW_TILE)(gates, o)
            nxt = [] if last else [weights[n][j] for n in MLA_MATRICES]
            y, *gathered = _make_mm("ret_o%d" % j, len(nxt))(comb, _rows(w_o), *nxt)
            if nxt:
                ahead["mla"] = gathered
        else:
            nxt = [] if last else [weights[n][j + 1] for n in ("ret_w_qkv", "ret_w_g", "ret_w_o")]
            y, *gathered = _mla_mixer(ahead.pop("mla"), j, u, g_q[j], g_kv[j], mla_cos, mla_sin, mla_perm, rw, nx, nxt)
            if nxt:
                ahead["ret_qkv"], ahead["ret_g"], ahead["ret_o"] = gathered
            w_in, w_out = ahead.pop("ffn_in"), ahead.pop("ffn_out")
        h1, u2 = rw(_f_ln_res_mod, ("row", "row", "grp", "par", "par", "grp", "grp"), (True,) * 7,
                    (d_model, d_model), "ln_a%d" % i)(
            h, y, grp(i, 2), ln_g[i, 0][None], ln_b[i, 0][None], grp(i, 4), grp(i, 3))
        if retention_layer and not last:
            ab, ahead["ffn_in"] = _make_mm("ffn_in%d" % i, 1)(u2, _columns(w_in), weights["ffn_w_in"][i + 1])
        else:
            (ab,) = _make_mm("ffn_in%d" % i)(u2, _columns(w_in))
        (act,) = rw(_f_swiglu, ("row",), (True,), (ab.shape[1] // 2,), "swiglu%d" % i, tile=WIDE_ROW_TILE)(ab)
        if retention_layer and not last:
            f, ahead["ffn_out"] = _make_mm("ffn_out%d" % i, 1)(act, _rows(w_out), weights["ffn_w_out"][i + 1])
        else:
            (f,) = _make_mm("ffn_out%d" % i)(act, _rows(w_out))
        (h,) = rw(_f_ln_res, ("row", "row", "grp", "par", "par"), (True,) * 5, (d_model,), "ln_f%d" % i)(
            h1, f, grp(i, 5), ln_g[i, 1][None], ln_b[i, 1][None])

    (rows,) = rw(_f_loss, ("row", "row"), (True, False), (LANE,), "loss")(h[:nx], tgt)
    return jnp.sum(rows[:, 0])


def _make_rowwise_plain_silu(cond):
    def body(c_ref, o_ref):
        o_ref[...] = _silu(c_ref[...])

    def call(c):
        return pl.pallas_call(body, name="silu_cond", out_shape=jax.ShapeDtypeStruct(c.shape, F32))(c)

    def bwd_body(c_ref, g_ref, o_ref):
        _, vjp = jax.vjp(_silu, c_ref[...])
        o_ref[...] = vjp(g_ref[...])[0]

    @jax.custom_vjp
    def op(c):
        return call(c)

    def fwd(c):
        return call(c), c

    def bwd(c, g):
        return (pl.pallas_call(bwd_body, name="silu_cond_bwd", out_shape=jax.ShapeDtypeStruct(c.shape, F32))(c, g),)

    op.defvjp(fwd, bwd)
    return op(cond)


MLA_MATRICES = ("mla_w_dq", "mla_w_uq", "mla_w_dkv", "mla_w_ukv", "mla_w_o")


def _mla_mixer(gathered, j, u, g_q, g_kv, cos, sin_signed, perm, rw, nx, ride):
    heads, dn, dr = MLA_HEADS, MLA_D_NOPE, MLA_D_ROPE
    w_dq, w_uq, w_dkv, w_ukv, w_o = gathered
    w_dq, w_dkv, w_ukv, w_o = _rows(w_dq), _rows(w_dkv), _columns(w_ukv), _rows(w_o)

    q_lora = w_uq.shape[1]
    w_uq = jnp.moveaxis(w_uq, 0, 1).reshape(q_lora, heads, dn + dr)
    w_uq = jnp.concatenate([w_uq, jnp.zeros((q_lora, heads, MLA_PAD - dr), w_uq.dtype)], axis=2)
    w_uq = w_uq.reshape(1, q_lora, heads * (dn + MLA_PAD))
    w_dkv = jnp.concatenate([w_dkv, jnp.zeros(w_dkv.shape[:2] + (MLA_PAD - dr,), w_dkv.dtype)], axis=2)

    (cq,) = _make_mm("mla_dq%d" % j)(u, w_dq)
    (cqn,) = rw(_f_rms, ("row", "par"), (True, True), (cq.shape[1],), "rms_q%d" % j)(cq, g_q[None])
    (q,) = _make_mm("mla_uq%d" % j)(cqn, w_uq)
    (ckv,) = _make_mm("mla_dkv%d" % j)(u, w_dkv)
    lora = ckv.shape[1] - MLA_PAD
    c_kv, kr = rw(_f_kv_latent, ("row", "par", "row", "row", "par"), (True, True, False, False, False),
                  (lora, MLA_PAD), "kv_lat%d" % j)(ckv, g_kv[None], cos, sin_signed, perm)
    (kv,) = _make_mm("mla_ukv%d" % j)(c_kv, w_ukv)
    o, *rode = _make_attention("l%d" % j, nx)(q, kv, kr, cos, sin_signed, perm, *ride)
    return (_make_mm("mla_o%d" % j)(o, w_o)[0], *rode)


WEIGHT_NAMES = ("c_ctx", "ada_w", "ada_b", "ln_g", "ln_b", "ret_w_qkv", "ret_w_g", "ret_decay_logit", "ret_w_o",
                "mla_w_dq", "mla_g_q", "mla_w_uq", "mla_w_dkv", "mla_g_kv", "mla_w_ukv", "mla_w_o",
                "ffn_w_in", "ffn_w_out")


def kernel(x, c, ctx, c_ctx, ada_w, ada_b, ln_g, ln_b, ret_w_qkv, ret_w_g, ret_decay_logit, ret_w_o, mla_w_dq, mla_g_q, mla_w_uq, mla_w_dkv, mla_g_kv, mla_w_ukv, mla_w_o, ffn_w_in, ffn_w_out, loss_target, m_c_ctx, m_ada_w, m_ada_b, m_ln_g, m_ln_b, m_ret_w_qkv, m_ret_w_g, m_ret_decay_logit, m_ret_w_o, m_mla_w_dq, m_mla_g_q, m_mla_w_uq, m_mla_w_dkv, m_mla_g_kv, m_mla_w_ukv, m_mla_w_o, m_ffn_w_in, m_ffn_w_out, v_c_ctx, v_ada_w, v_ada_b, v_ln_g, v_ln_b, v_ret_w_qkv, v_ret_w_g, v_ret_decay_logit, v_ret_w_o, v_mla_w_dq, v_mla_g_q, v_mla_w_uq, v_mla_w_dkv, v_mla_g_kv, v_mla_w_ukv, v_mla_w_o, v_ffn_w_in, v_ffn_w_out):
    given = dict(locals())
    weights = {n: given[n] for n in WEIGHT_NAMES}
    me = _my_index()
    c_all = _all_gather_call(c, "ag_cond").reshape(N_DEV, -1)

    def loss_of(wts, x2):
        return _loss_fn(wts, x2, c_all, ctx[0], loss_target[0], me)

    loss, (grad_w, grad_x) = jax.value_and_grad(loss_of, argnums=(0, 1))(weights, x[0])
    loss = lax.psum(loss, AXES)
    delta, new_m, new_v = {}, {}, {}
    for n in WEIGHT_NAMES:
        delta[n], new_m[n], new_v[n] = _adamw(weights[n], grad_w[n], given["m_" + n], given["v_" + n], "adamw_" + n)
    return (loss, grad_x[None], *[grad_w[n] for n in WEIGHT_NAMES], *[delta[n] for n in WEIGHT_NAMES],
            *[new_m[n] for n in WEIGHT_NAMES], *[new_v[n] for n in WEIGHT_NAMES])
```
